```python
import jax, jax.numpy as jnp
from jax import lax
import numpy as np


D_MODEL = 4096
BATCH = 16
SEQ = 2048
DEPTH = 1

GM_WIDTH = 2048
GM_GROUPS = 8
GM_GROUP_W = GM_WIDTH // GM_GROUPS
CHUNK = 128
MLA_HEADS = 32
QK_NOPE = 128
QK_ROPE = 64
QK_HEAD = QK_NOPE + QK_ROPE
V_HEAD = 128
Q_LORA = 1024
KV_LORA = 512
ROPE_THETA = 10000.0
Q_BLOCK = 128
D_FF = 4 * D_MODEL
N_BRANCH = 2
N_MOD = 6
EPS = 1e-6
OFF_Q = 2 * GM_WIDTH
OFF_KV = OFF_Q + Q_LORA
OFF_KPE = OFF_KV + KV_LORA
OFF_GATE = OFF_KPE + QK_ROPE
IN_COLS = OFF_GATE + N_BRANCH * D_MODEL

kernel_name = 'hybrid_gmlp_mla_block'


def rms_norm(x, g):
    xf = x.astype(jnp.float32)
    y = xf * lax.rsqrt(jnp.mean(xf * xf, axis=-1, keepdims=True) + EPS)
    return (y * g.astype(jnp.float32)).astype(x.dtype)


def modulate(h, shift, scale):
    return h * (1 + scale[:, None, :]) + shift[:, None, :]


def rope_tables(positions, dtype):
    inv_freq = 1.0 / (ROPE_THETA ** (jnp.arange(0, QK_ROPE, 2, dtype=jnp.float32) / QK_ROPE))
    ang = positions.astype(jnp.float32)[..., None] * inv_freq
    return (jnp.cos(ang)[:, :, None, :].astype(dtype),
            jnp.sin(ang)[:, :, None, :].astype(dtype))


def apply_rope(x, cos, sin):
    x1, x2 = jnp.split(x, 2, axis=-1)
    return jnp.concatenate([x1 * cos - x2 * sin, x2 * cos + x1 * sin], axis=-1)


def gmlp_branch(uv, g_v, w_s, b_s):
    z = jax.nn.gelu(uv, approximate=False)
    u, v = z[..., :GM_WIDTH], z[..., GM_WIDTH:]
    v = rms_norm(v, g_v)
    b, s, _ = v.shape
    v = v.reshape(b, s // CHUNK, CHUNK, GM_GROUPS, GM_GROUP_W)
    causal = jnp.tril(jnp.ones((CHUNK, CHUNK), dtype=bool))
    w = jnp.where(causal[None], w_s, 0.0)
    mixed = jnp.einsum('gts,bnsgc->bntgc', w, v) + b_s.T[None, None, :, :, None]
    return u * mixed.reshape(b, s, GM_WIDTH)


def mla_branch(q_lat, kv_lat, k_pe, cos, sin, g_q_lat, g_kv_lat, w_uq, w_ukv, g_qn, g_kn):
    b, s, _ = q_lat.shape
    q = (rms_norm(q_lat, g_q_lat) @ w_uq).reshape(b, s, MLA_HEADS, QK_HEAD)
    kv = (rms_norm(kv_lat, g_kv_lat) @ w_ukv).reshape(b, s, MLA_HEADS, QK_NOPE + V_HEAD)
    k_nope, v = kv[..., :QK_NOPE], kv[..., QK_NOPE:]
    k = jnp.concatenate([k_nope, jnp.broadcast_to(k_pe[:, :, None, :], (b, s, MLA_HEADS, QK_ROPE))], axis=-1)
    q = rms_norm(q, g_qn)
    k = rms_norm(k, g_kn)
    q = jnp.concatenate([q[..., :QK_NOPE], apply_rope(q[..., QK_NOPE:], cos, sin)], axis=-1)
    k = jnp.concatenate([k[..., :QK_NOPE], apply_rope(k[..., QK_NOPE:], cos, sin)], axis=-1)
    q = q.transpose(0, 2, 1, 3)
    k = k.transpose(0, 2, 1, 3)
    v = v.transpose(0, 2, 1, 3)
    nb = s // Q_BLOCK
    q_blocks = q.reshape(b, MLA_HEADS, nb, Q_BLOCK, QK_HEAD).transpose(2, 0, 1, 3, 4)
    key_idx = jnp.arange(s)
    scale = QK_HEAD ** -0.5

    def attend(args):
        q_blk, i = args
        q_idx = i * Q_BLOCK + jnp.arange(Q_BLOCK)
        scores = jnp.einsum('bhqd,bhkd->bhqk', q_blk, k).astype(jnp.float32) * scale
        scores = jnp.where(key_idx[None, :] <= q_idx[:, None], scores, -jnp.inf)
        p = jax.nn.softmax(scores, axis=-1).astype(v.dtype)
        return jnp.einsum('bhqk,bhkd->bhqd', p, v)

    out = lax.map(attend, (q_blocks, jnp.arange(nb)))
    return out.transpose(1, 0, 3, 2, 4).reshape(b, s, MLA_HEADS * V_HEAD)


def _fwd_setup_inputs(seed: int = 0) -> dict:
    key = jax.random.key(seed)
    ks = jax.random.split(key, 24)
    f32 = jnp.float32

    def nrm(k, shape, fan_in):
        return jax.random.normal(k, shape, f32) * (fan_in ** -0.5)

    def gain(k, shape):
        return 1.0 + 0.02 * jax.random.normal(k, shape, f32)

    L = DEPTH
    x = jax.random.normal(ks[0], (BATCH, SEQ, D_MODEL), f32)
    c = jax.random.normal(ks[1], (BATCH, D_MODEL), f32)
    positions = (jnp.arange(SEQ, dtype=jnp.int32)[None, :]
                 + jax.random.randint(ks[2], (BATCH, 1), 0, 4096, dtype=jnp.int32))
    return {
        'x': x,
        'c': c,
        'positions': positions,
        'w_ada': nrm(ks[3], (L, D_MODEL, N_MOD * D_MODEL), D_MODEL),
        'b_ada': 0.02 * jax.random.normal(ks[4], (L, N_MOD * D_MODEL), f32),
        'g_norm1': gain(ks[5], (L, D_MODEL)),
        'w_in': nrm(ks[6], (L, D_MODEL, IN_COLS), D_MODEL),
        'g_v': gain(ks[7], (L, GM_WIDTH)),
        'w_s': nrm(ks[8], (L, GM_GROUPS, CHUNK, CHUNK), CHUNK),
        'b_s': 1.0 + 0.1 * jax.random.normal(ks[9], (L, GM_GROUPS, CHUNK), f32),
        'g_q_lat': gain(ks[10], (L, Q_LORA)),
        'g_kv_lat': gain(ks[11], (L, KV_LORA)),
        'w_uq': nrm(ks[12], (L, Q_LORA, MLA_HEADS * QK_HEAD), Q_LORA),
        'w_ukv': nrm(ks[13], (L, KV_LORA, MLA_HEADS * (QK_NOPE + V_HEAD)), KV_LORA),
        'g_qn': gain(ks[14], (L, QK_HEAD)),
        'g_kn': gain(ks[15], (L, QK_HEAD)),
        'w_branch_a': nrm(ks[16], (L, GM_WIDTH, D_MODEL), GM_WIDTH),
        'w_branch_b': nrm(ks[17], (L, MLA_HEADS * V_HEAD, D_MODEL), MLA_HEADS * V_HEAD),
        'w_out': nrm(ks[18], (L, D_MODEL, D_MODEL), D_MODEL),
        'g_norm2': gain(ks[19], (L, D_MODEL)),
        'w_ff1': nrm(ks[20], (L, D_MODEL, D_FF), D_MODEL),
        'w_ff2': nrm(ks[21], (L, D_FF, D_MODEL), D_FF),
    }


def _fwd_reference(x, c, positions, w_ada, b_ada, g_norm1, w_in, g_v, w_s, b_s, g_q_lat, g_kv_lat,
              w_uq, w_ukv, g_qn, g_kn, w_branch_a, w_branch_b, w_out, g_norm2, w_ff1, w_ff2):
    cos, sin = rope_tables(positions, x.dtype)
    cond = jax.nn.silu(c)
    for l in range(DEPTH):
        mod = cond @ w_ada[l] + b_ada[l]
        sh1, sc1, ga1, sh2, sc2, ga2 = jnp.split(mod, N_MOD, axis=-1)
        h = modulate(rms_norm(x, g_norm1[l]), sh1, sc1)
        proj = h @ w_in[l]
        y_a = gmlp_branch(proj[..., :OFF_Q], g_v[l], w_s[l], b_s[l]) @ w_branch_a[l]
        y_b = mla_branch(proj[..., OFF_Q:OFF_KV], proj[..., OFF_KV:OFF_KPE], proj[..., OFF_KPE:OFF_GATE],
                         cos, sin, g_q_lat[l], g_kv_lat[l], w_uq[l], w_ukv[l], g_qn[l], g_kn[l]) @ w_branch_b[l]
        gate_a = jax.nn.sigmoid(proj[..., OFF_GATE:OFF_GATE + D_MODEL])
        gate_b = jax.nn.sigmoid(proj[..., OFF_GATE + D_MODEL:])
        mixed = gate_a * y_a + gate_b * y_b
        x = x + ga1[:, None, :] * (mixed @ w_out[l])
        h = modulate(rms_norm(x, g_norm2[l]), sh2, sc2)
        ff = jnp.square(jax.nn.relu(h @ w_ff1[l])) @ w_ff2[l]
        x = x + ga2[:, None, :] * ff
    return x


import jax as _jax
import jax.numpy as _jnp

TWIN_FORMAT = 'train_step'
FWD_PARAMS = ['x', 'c', 'positions', 'w_ada', 'b_ada', 'g_norm1', 'w_in', 'g_v', 'w_s', 'b_s', 'g_q_lat', 'g_kv_lat', 'w_uq', 'w_ukv', 'g_qn', 'g_kn', 'w_branch_a', 'w_branch_b', 'w_out', 'g_norm2', 'w_ff1', 'w_ff2']
TWIN_WEIGHTS = ['w_ada', 'b_ada', 'g_norm1', 'w_in', 'g_v', 'w_s', 'b_s', 'g_q_lat', 'g_kv_lat', 'w_uq', 'w_ukv', 'g_qn', 'g_kn', 'w_branch_a', 'w_branch_b', 'w_out', 'g_norm2', 'w_ff1', 'w_ff2']
TWIN_DIFF_INPUT = 'x'
TWIN_INPUTS = ['x', 'c', 'positions', 'w_ada', 'b_ada', 'g_norm1', 'w_in', 'g_v', 'w_s', 'b_s', 'g_q_lat', 'g_kv_lat', 'w_uq', 'w_ukv', 'g_qn', 'g_kn', 'w_branch_a', 'w_branch_b', 'w_out', 'g_norm2', 'w_ff1', 'w_ff2', 'loss_target', 'm_w_ada', 'm_b_ada', 'm_g_norm1', 'm_w_in', 'm_g_v', 'm_w_s', 'm_b_s', 'm_g_q_lat', 'm_g_kv_lat', 'm_w_uq', 'm_w_ukv', 'm_g_qn', 'm_g_kn', 'm_w_branch_a', 'm_w_branch_b', 'm_w_out', 'm_g_norm2', 'm_w_ff1', 'm_w_ff2', 'v_w_ada', 'v_b_ada', 'v_g_norm1', 'v_w_in', 'v_g_v', 'v_w_s', 'v_b_s', 'v_g_q_lat', 'v_g_kv_lat', 'v_w_uq', 'v_w_ukv', 'v_g_qn', 'v_g_kn', 'v_w_branch_a', 'v_w_branch_b', 'v_w_out', 'v_g_norm2', 'v_w_ff1', 'v_w_ff2']
TWIN_OUTPUTS = ['loss', 'grad_x', 'grad_w_ada', 'grad_b_ada', 'grad_g_norm1', 'grad_w_in', 'grad_g_v', 'grad_w_s', 'grad_b_s', 'grad_g_q_lat', 'grad_g_kv_lat', 'grad_w_uq', 'grad_w_ukv', 'grad_g_qn', 'grad_g_kn', 'grad_w_branch_a', 'grad_w_branch_b', 'grad_w_out', 'grad_g_norm2', 'grad_w_ff1', 'grad_w_ff2', 'delta_w_ada', 'delta_b_ada', 'delta_g_norm1', 'delta_w_in', 'delta_g_v', 'delta_w_s', 'delta_b_s', 'delta_g_q_lat', 'delta_g_kv_lat', 'delta_w_uq', 'delta_w_ukv', 'delta_g_qn', 'delta_g_kn', 'delta_w_branch_a', 'delta_w_branch_b', 'delta_w_out', 'delta_g_norm2', 'delta_w_ff1', 'delta_w_ff2', 'new_m_w_ada', 'new_m_b_ada', 'new_m_g_norm1', 'new_m_w_in', 'new_m_g_v', 'new_m_w_s', 'new_m_b_s', 'new_m_g_q_lat', 'new_m_g_kv_lat', 'new_m_w_uq', 'new_m_w_ukv', 'new_m_g_qn', 'new_m_g_kn', 'new_m_w_branch_a', 'new_m_w_branch_b', 'new_m_w_out', 'new_m_g_norm2', 'new_m_w_ff1', 'new_m_w_ff2', 'new_v_w_ada', 'new_v_b_ada', 'new_v_g_norm1', 'new_v_w_in', 'new_v_g_v', 'new_v_w_s', 'new_v_b_s', 'new_v_g_q_lat', 'new_v_g_kv_lat', 'new_v_w_uq', 'new_v_w_ukv', 'new_v_g_qn', 'new_v_g_kn', 'new_v_w_branch_a', 'new_v_w_branch_b', 'new_v_w_out', 'new_v_g_norm2', 'new_v_w_ff1', 'new_v_w_ff2']
TWIN_LEAF_KINDS = {'loss': 'loss', 'grad_x': 'grad_x', 'grad_w_ada': 'grad_w', 'grad_b_ada': 'grad_w', 'grad_g_norm1': 'grad_w', 'grad_w_in': 'grad_w', 'grad_g_v': 'grad_w', 'grad_w_s': 'grad_w', 'grad_b_s': 'grad_w', 'grad_g_q_lat': 'grad_w', 'grad_g_kv_lat': 'grad_w', 'grad_w_uq': 'grad_w', 'grad_w_ukv': 'grad_w', 'grad_g_qn': 'grad_w', 'grad_g_kn': 'grad_w', 'grad_w_branch_a': 'grad_w', 'grad_w_branch_b': 'grad_w', 'grad_w_out': 'grad_w', 'grad_g_norm2': 'grad_w', 'grad_w_ff1': 'grad_w', 'grad_w_ff2': 'grad_w', 'delta_w_ada': 'delta_w', 'delta_b_ada': 'delta_w', 'delta_g_norm1': 'delta_w', 'delta_w_in': 'delta_w', 'delta_g_v': 'delta_w', 'delta_w_s': 'delta_w', 'delta_b_s': 'delta_w', 'delta_g_q_lat': 'delta_w', 'delta_g_kv_lat': 'delta_w', 'delta_w_uq': 'delta_w', 'delta_w_ukv': 'delta_w', 'delta_g_qn': 'delta_w', 'delta_g_kn': 'delta_w', 'delta_w_branch_a': 'delta_w', 'delta_w_branch_b': 'delta_w', 'delta_w_out': 'delta_w', 'delta_g_norm2': 'delta_w', 'delta_w_ff1': 'delta_w', 'delta_w_ff2': 'delta_w', 'new_m_w_ada': 'new_m', 'new_m_b_ada': 'new_m', 'new_m_g_norm1': 'new_m', 'new_m_w_in': 'new_m', 'new_m_g_v': 'new_m', 'new_m_w_s': 'new_m', 'new_m_b_s': 'new_m', 'new_m_g_q_lat': 'new_m', 'new_m_g_kv_lat': 'new_m', 'new_m_w_uq': 'new_m', 'new_m_w_ukv': 'new_m', 'new_m_g_qn': 'new_m', 'new_m_g_kn': 'new_m', 'new_m_w_branch_a': 'new_m', 'new_m_w_branch_b': 'new_m', 'new_m_w_out': 'new_m', 'new_m_g_norm2': 'new_m', 'new_m_w_ff1': 'new_m', 'new_m_w_ff2': 'new_m', 'new_v_w_ada': 'new_v', 'new_v_b_ada': 'new_v', 'new_v_g_norm1': 'new_v', 'new_v_w_in': 'new_v', 'new_v_g_v': 'new_v', 'new_v_w_s': 'new_v', 'new_v_b_s': 'new_v', 'new_v_g_q_lat': 'new_v', 'new_v_g_kv_lat': 'new_v', 'new_v_w_uq': 'new_v', 'new_v_w_ukv': 'new_v', 'new_v_g_qn': 'new_v', 'new_v_g_kn': 'new_v', 'new_v_w_branch_a': 'new_v', 'new_v_w_branch_b': 'new_v', 'new_v_w_out': 'new_v', 'new_v_g_norm2': 'new_v', 'new_v_w_ff1': 'new_v', 'new_v_w_ff2': 'new_v'}


def _forward(args):
    return _fwd_reference(*[args[k] for k in FWD_PARAMS])


def _output_shape():
    def fwd():
        inp = _fwd_setup_inputs(0)
        return _fwd_reference(*[inp[k] for k in FWD_PARAMS])
    out = _jax.eval_shape(fwd)
    return out.shape, out.dtype

N_MICROBATCH = 1
ADAM_LR = 0.001
ADAM_B1 = 0.9
ADAM_B2 = 0.999
ADAM_EPS = 1e-08
ADAM_WD = 0.01
ADAM_STEP = 10
PER_EXAMPLE_BATCH_AXIS = {'x': 0, 'c': 0, 'positions': 0, 'loss_target': 0}
SHARED_INPUTS = []
_WEIGHT_DTYPES = {'w_ada': _jnp.float32, 'b_ada': _jnp.float32, 'g_norm1': _jnp.float32, 'w_in': _jnp.float32, 'g_v': _jnp.float32, 'w_s': _jnp.float32, 'b_s': _jnp.float32, 'g_q_lat': _jnp.float32, 'g_kv_lat': _jnp.float32, 'w_uq': _jnp.float32, 'w_ukv': _jnp.float32, 'g_qn': _jnp.float32, 'g_kn': _jnp.float32, 'w_branch_a': _jnp.float32, 'w_branch_b': _jnp.float32, 'w_out': _jnp.float32, 'g_norm2': _jnp.float32, 'w_ff1': _jnp.float32, 'w_ff2': _jnp.float32}
MOMENT_SCALE = {'w_ada': 3.263055e+00, 'b_ada': 8.509122e+00, 'g_norm1': 9.819292e-01, 'w_in': 6.124097e-01, 'g_v': 8.412785e-01, 'w_s': 1.081310e+00, 'b_s': 3.033561e+00, 'g_q_lat': 4.149439e-02, 'g_kv_lat': 2.900027e+00, 'w_uq': 1.732561e-02, 'w_ukv': 6.069171e-01, 'g_qn': 1.269296e-01, 'g_kn': 1.281145e-01, 'w_branch_a': 1.038719e+00, 'w_branch_b': 8.576951e-01, 'w_out': 1.327821e+00, 'g_norm2': 2.216148e+01, 'w_ff1': 1.288882e+00, 'w_ff2': 3.100389e+00}


def _to_microbatches(a, axis):
    t = _jnp.moveaxis(a, axis, 0)
    t = t.reshape((N_MICROBATCH, t.shape[0] // N_MICROBATCH) + t.shape[1:])
    return _jnp.moveaxis(t, 1, axis + 1)


def setup_inputs(seed: int = 0) -> dict:
    inp = _fwd_setup_inputs(seed)
    key = _jax.random.fold_in(_jax.random.key(seed), 7919)
    shape, _ = _output_shape()
    out = dict(inp)
    out["loss_target"] = _jax.random.normal(_jax.random.fold_in(key, 0), shape, _jnp.float32)
    for i, name in enumerate(TWIN_WEIGHTS):
        w = inp[name].astype(_jnp.float32)
        if MOMENT_SCALE is None:
            s = _jnp.sqrt(_jnp.mean(_jnp.square(w)) + 1e-30)
        else:
            s = MOMENT_SCALE[name]
        km, kv = _jax.random.split(_jax.random.fold_in(key, i + 1))
        out[name] = w
        out["m_" + name] = s * _jax.random.normal(km, w.shape, _jnp.float32)
        out["v_" + name] = (s * s) * _jax.random.uniform(kv, w.shape, _jnp.float32, 0.5, 1.5)
    if N_MICROBATCH > 1:
        for name, axis in PER_EXAMPLE_BATCH_AXIS.items():
            out[name] = _to_microbatches(out[name], axis)
    return {'x': out['x'], 'c': out['c'], 'positions': out['positions'], 'w_ada': out['w_ada'], 'b_ada': out['b_ada'], 'g_norm1': out['g_norm1'], 'w_in': out['w_in'], 'g_v': out['g_v'], 'w_s': out['w_s'], 'b_s': out['b_s'], 'g_q_lat': out['g_q_lat'], 'g_kv_lat': out['g_kv_lat'], 'w_uq': out['w_uq'], 'w_ukv': out['w_ukv'], 'g_qn': out['g_qn'], 'g_kn': out['g_kn'], 'w_branch_a': out['w_branch_a'], 'w_branch_b': out['w_branch_b'], 'w_out': out['w_out'], 'g_norm2': out['g_norm2'], 'w_ff1': out['w_ff1'], 'w_ff2': out['w_ff2'], 'loss_target': out['loss_target'], 'm_w_ada': out['m_w_ada'], 'm_b_ada': out['m_b_ada'], 'm_g_norm1': out['m_g_norm1'], 'm_w_in': out['m_w_in'], 'm_g_v': out['m_g_v'], 'm_w_s': out['m_w_s'], 'm_b_s': out['m_b_s'], 'm_g_q_lat': out['m_g_q_lat'], 'm_g_kv_lat': out['m_g_kv_lat'], 'm_w_uq': out['m_w_uq'], 'm_w_ukv': out['m_w_ukv'], 'm_g_qn': out['m_g_qn'], 'm_g_kn': out['m_g_kn'], 'm_w_branch_a': out['m_w_branch_a'], 'm_w_branch_b': out['m_w_branch_b'], 'm_w_out': out['m_w_out'], 'm_g_norm2': out['m_g_norm2'], 'm_w_ff1': out['m_w_ff1'], 'm_w_ff2': out['m_w_ff2'], 'v_w_ada': out['v_w_ada'], 'v_b_ada': out['v_b_ada'], 'v_g_norm1': out['v_g_norm1'], 'v_w_in': out['v_w_in'], 'v_g_v': out['v_g_v'], 'v_w_s': out['v_w_s'], 'v_b_s': out['v_b_s'], 'v_g_q_lat': out['v_g_q_lat'], 'v_g_kv_lat': out['v_g_kv_lat'], 'v_w_uq': out['v_w_uq'], 'v_w_ukv': out['v_w_ukv'], 'v_g_qn': out['v_g_qn'], 'v_g_kn': out['v_g_kn'], 'v_w_branch_a': out['v_w_branch_a'], 'v_w_branch_b': out['v_w_branch_b'], 'v_w_out': out['v_w_out'], 'v_g_norm2': out['v_g_norm2'], 'v_w_ff1': out['v_w_ff1'], 'v_w_ff2': out['v_w_ff2']}


def _loss(weights, diff, rest, loss_target):
    with _jax.named_scope("forward"):
        args = {**rest, TWIN_DIFF_INPUT: diff, **{k: w.astype(_WEIGHT_DTYPES[k]) for k, w in weights.items()}}
        y = _forward(args)
    with _jax.named_scope("loss_head"):
        err = _jnp.square(y.astype(_jnp.float32) - loss_target)
        return 0.5 * _jnp.sum(_jnp.mean(err, axis=-1)) if err.ndim else 0.5 * err


def _adamw(w, g, m, v):
    m = ADAM_B1 * m + (1.0 - ADAM_B1) * g
    v = ADAM_B2 * v + (1.0 - ADAM_B2) * _jnp.square(g)
    m_hat = m / (1.0 - ADAM_B1 ** ADAM_STEP)
    v_hat = v / (1.0 - ADAM_B2 ** ADAM_STEP)
    delta = -ADAM_LR * (m_hat / (_jnp.sqrt(v_hat) + ADAM_EPS) + ADAM_WD * w)
    return delta, m, v


def reference(x, c, positions, w_ada, b_ada, g_norm1, w_in, g_v, w_s, b_s, g_q_lat, g_kv_lat, w_uq, w_ukv, g_qn, g_kn, w_branch_a, w_branch_b, w_out, g_norm2, w_ff1, w_ff2, loss_target, m_w_ada, m_b_ada, m_g_norm1, m_w_in, m_g_v, m_w_s, m_b_s, m_g_q_lat, m_g_kv_lat, m_w_uq, m_w_ukv, m_g_qn, m_g_kn, m_w_branch_a, m_w_branch_b, m_w_out, m_g_norm2, m_w_ff1, m_w_ff2, v_w_ada, v_b_ada, v_g_norm1, v_w_in, v_g_v, v_w_s, v_b_s, v_g_q_lat, v_g_kv_lat, v_w_uq, v_w_ukv, v_g_qn, v_g_kn, v_w_branch_a, v_w_branch_b, v_w_out, v_g_norm2, v_w_ff1, v_w_ff2):
    given = dict(x=x, c=c, positions=positions, w_ada=w_ada, b_ada=b_ada, g_norm1=g_norm1, w_in=w_in, g_v=g_v, w_s=w_s, b_s=b_s, g_q_lat=g_q_lat, g_kv_lat=g_kv_lat, w_uq=w_uq, w_ukv=w_ukv, g_qn=g_qn, g_kn=g_kn, w_branch_a=w_branch_a, w_branch_b=w_branch_b, w_out=w_out, g_norm2=g_norm2, w_ff1=w_ff1, w_ff2=w_ff2, loss_target=loss_target, m_w_ada=m_w_ada, m_b_ada=m_b_ada, m_g_norm1=m_g_norm1, m_w_in=m_w_in, m_g_v=m_g_v, m_w_s=m_w_s, m_b_s=m_b_s, m_g_q_lat=m_g_q_lat, m_g_kv_lat=m_g_kv_lat, m_w_uq=m_w_uq, m_w_ukv=m_w_ukv, m_g_qn=m_g_qn, m_g_kn=m_g_kn, m_w_branch_a=m_w_branch_a, m_w_branch_b=m_w_branch_b, m_w_out=m_w_out, m_g_norm2=m_g_norm2, m_w_ff1=m_w_ff1, m_w_ff2=m_w_ff2, v_w_ada=v_w_ada, v_b_ada=v_b_ada, v_g_norm1=v_g_norm1, v_w_in=v_w_in, v_g_v=v_g_v, v_w_s=v_w_s, v_b_s=v_b_s, v_g_q_lat=v_g_q_lat, v_g_kv_lat=v_g_kv_lat, v_w_uq=v_w_uq, v_w_ukv=v_w_ukv, v_g_qn=v_g_qn, v_g_kn=v_g_kn, v_w_branch_a=v_w_branch_a, v_w_branch_b=v_w_branch_b, v_w_out=v_w_out, v_g_norm2=v_g_norm2, v_w_ff1=v_w_ff1, v_w_ff2=v_w_ff2)
    weights = {n: given[n] for n in TWIN_WEIGHTS}
    shared = {n: given[n] for n in SHARED_INPUTS}
    per_example = {n: given[n] for n in ['x', 'c', 'positions']}
    grad_fn = _jax.value_and_grad(_loss, argnums=(0, 1))

    def one_microbatch(ex, loss_target):
        ex = dict(ex)
        diff = ex.pop(TWIN_DIFF_INPUT)
        return grad_fn(weights, diff, {**shared, **ex}, loss_target)

    if N_MICROBATCH == 1:
        loss, (grad_w, grad_x) = one_microbatch(per_example, given["loss_target"])
    else:
        def body(carry, xs):
            loss_sum, grad_sum = carry
            l_k, (gw_k, gx_k) = one_microbatch(xs[0], xs[1])
            with _jax.named_scope("update"):
                return (loss_sum + l_k, _jax.tree.map(_jnp.add, grad_sum, gw_k)), gx_k

        init = (_jnp.zeros((), _jnp.float32), _jax.tree.map(_jnp.zeros_like, weights))
        (loss, grad_w), grad_x = _jax.lax.scan(body, init, (per_example, given["loss_target"]))
    with _jax.named_scope("update"):
        delta_w, new_m, new_v = {}, {}, {}
        for n in TWIN_WEIGHTS:
            delta_w[n], new_m[n], new_v[n] = _adamw(weights[n], grad_w[n], given["m_" + n], given["v_" + n])
    return (loss, grad_x, *[grad_w[n] for n in TWIN_WEIGHTS], *[delta_w[n] for n in TWIN_WEIGHTS],
            *[new_m[n] for n in TWIN_WEIGHTS], *[new_v[n] for n in TWIN_WEIGHTS])
```

```python
import functools
import math

import jax
import jax.numpy as jnp
from jax import lax
from jax.experimental import pallas as pl
from jax.experimental.pallas import tpu as pltpu

F32 = jnp.float32
BF16 = jnp.bfloat16

GM_WIDTH = 2048
GM_GROUPS = 8
CHUNK = 128
MLA_HEADS = 32
QK_NOPE = 128
QK_ROPE = 64
V_HEAD = 128
Q_LORA = 1024
KV_LORA = 512
ROPE_THETA = 10000.0
N_MOD = 6
EPS = 1e-6
ADAM_LR = 0.001
ADAM_B1 = 0.9
ADAM_B2 = 0.999
ADAM_EPS = 1e-08
ADAM_WD = 0.01
ADAM_STEP = 10

N_CHIP = 4
N_DEV = 8
LANE = 128
QK_PAD = 256
VMEM_LIMIT = 56 * 1024 * 1024
MESH = pl.DeviceIdType.MESH


def _round_up(n, m):
    return (n + m - 1) // m * m


def _tile(n, target, align):
    t = min(target, n) // align * align
    while t >= align:
        if n % t == 0:
            return t
        t -= align
    return n


def _params(*sem):
    return pltpu.CompilerParams(dimension_semantics=sem, vmem_limit_bytes=VMEM_LIMIT)


def _mesh_pos():
    return lax.axis_index("x"), lax.axis_index("y"), lax.axis_index("c")


def _mm(a, b, *, name, ta=False, tb=False, out_dtypes=(F32,), epilogue=None, extras=(),
        tm=1024, tn=1024, tk=512):
    m, k = (a.shape[1], a.shape[0]) if ta else a.shape
    n = b.shape[0] if tb else b.shape[1]
    assert k == (b.shape[1] if tb else b.shape[0]), (a.shape, b.shape)
    tm = _tile(m, tm, LANE if ta else 16)
    tn = _tile(n, tn, LANE)
    tk = _tile(k, tk, LANE)
    nk = k // tk
    ne, no = len(extras), len(out_dtypes)
    dims = (((0 if ta else 1,), (1 if tb else 0,)), ((), ()))

    def body(*refs):
        a_ref, b_ref = refs[:2]
        ex = refs[2:2 + ne]
        outs = refs[2 + ne:2 + ne + no]
        acc = refs[-1]
        kk = pl.program_id(2)

        @pl.when(kk == 0)
        def _():
            acc[...] = jnp.zeros_like(acc)

        acc[...] += lax.dot_general(a_ref[...].astype(BF16), b_ref[...].astype(BF16), dims,
                                    preferred_element_type=F32)

        @pl.when(kk == nk - 1)
        def _():
            r = acc[...]
            res = epilogue(r, *[e[...] for e in ex]) if epilogue is not None else (r,)
            for o, val in zip(outs, res):
                o[...] = val.astype(o.dtype)

    a_spec = pl.BlockSpec((tk, tm), lambda i, j, q: (q, i)) if ta else pl.BlockSpec((tm, tk), lambda i, j, q: (i, q))
    b_spec = pl.BlockSpec((tn, tk), lambda i, j, q: (j, q)) if tb else pl.BlockSpec((tk, tn), lambda i, j, q: (q, j))
    o_spec = pl.BlockSpec((tm, tn), lambda i, j, q: (i, j))
    res = pl.pallas_call(
        body, name=name,
        grid=(m // tm, n // tn, nk),
        in_specs=[a_spec, b_spec] + [o_spec] * ne,
        out_specs=[o_spec] * no,
        out_shape=[jax.ShapeDtypeStruct((m, n), dt) for dt in out_dtypes],
        scratch_shapes=[pltpu.VMEM((tm, tn), F32)],
        compiler_params=_params("parallel", "parallel", "arbitrary"),
    )(a, b, *extras)
    return res[0] if no == 1 else res


def _relu2_epilogue(r):
    return r, jnp.square(jnp.maximum(r, 0.0))


def _relu2_bwd_epilogue(r, f1):
    return (r * (2.0 * jnp.maximum(f1.astype(F32), 0.0)),)


def _row_tile(seq, width, nbytes=2 * 1024 * 1024):
    return _tile(seq, max(8, nbytes // (4 * width)), 8)


def _silu(c_all):
    def body(c_ref, o_ref):
        v = c_ref[...]
        o_ref[...] = v * jax.nn.sigmoid(v)

    return pl.pallas_call(body, name="silu", out_shape=jax.ShapeDtypeStruct(c_all.shape, F32))(c_all)


def _norm_mod_fwd(x, g, shift, scale, seq, *, name, resid=None):
    t, w = x.shape
    tr = _row_tile(seq, w, (1 if resid is not None else 2) * 1024 * 1024)
    nb = seq // tr
    has_res = resid is not None

    def body(*refs):
        if has_res:
            x_ref, o_ref, gate_ref, g_ref, sh_ref, sc_ref, x1_ref, h_ref = refs
            xv = x_ref[...] + gate_ref[0] * o_ref[...]
            x1_ref[...] = xv
        else:
            x_ref, g_ref, sh_ref, sc_ref, h_ref = refs
            xv = x_ref[...]
        r = lax.rsqrt(jnp.mean(xv * xv, axis=-1, keepdims=True) + EPS)
        nrm = xv * r * g_ref[...]
        h_ref[...] = (nrm * (1.0 + sc_ref[0]) + sh_ref[0]).astype(BF16)

    row = pl.BlockSpec((tr, w), lambda i: (i, 0))
    vec = pl.BlockSpec((1, w), lambda i: (0, 0))
    per_ex = pl.BlockSpec((1, 1, w), lambda i: (i // nb, 0, 0))
    if has_res:
        o, gate = resid
        ins, in_specs = (x, o, gate, g, shift, scale), [row, row, per_ex, vec, per_ex, per_ex]
        out_shape = [jax.ShapeDtypeStruct((t, w), F32), jax.ShapeDtypeStruct((t, w), BF16)]
        out_specs = [row, row]
    else:
        ins, in_specs = (x, g, shift, scale), [row, vec, per_ex, per_ex]
        out_shape = jax.ShapeDtypeStruct((t, w), BF16)
        out_specs = row
    return pl.pallas_call(body, name=name, grid=(t // tr,), in_specs=in_specs, out_specs=out_specs,
                          out_shape=out_shape, compiler_params=_params("parallel"))(*ins)


def _norm_mod_bwd(x, g, scale, dh, seq, *, name, dres=None, gate_o=None):
    t, w = x.shape
    nex = t // seq
    tr = _row_tile(seq, w, 1024 * 1024)
    nb = seq // tr
    has_res, has_gate = dres is not None, gate_o is not None

    def body(*refs):
        refs = list(refs)
        x_ref, g_ref, sc_ref, dh_ref = refs[:4]
        p = 4
        dres_ref = None
        if has_res:
            dres_ref = refs[p]
            p += 1
        if has_gate:
            o_ref, gate_ref = refs[p:p + 2]
            p += 2
        dx_ref, dsh_ref, dsc_ref, dg_ref = refs[p:p + 4]
        p += 4
        i = pl.program_id(0)

        @pl.when(i % nb == 0)
        def _():
            dsh_ref[...] = jnp.zeros_like(dsh_ref)
            dsc_ref[...] = jnp.zeros_like(dsc_ref)
            if has_gate:
                refs[p + 1][...] = jnp.zeros_like(refs[p + 1])

        @pl.when(i == 0)
        def _():
            dg_ref[...] = jnp.zeros_like(dg_ref)

        xv = x_ref[...]
        gv = g_ref[...]
        dhv = dh_ref[...]
        r = lax.rsqrt(jnp.mean(xv * xv, axis=-1, keepdims=True) + EPS)
        xh = xv * r
        dsh_ref[0] += jnp.sum(dhv, axis=0, keepdims=True)
        dsc_ref[0] += jnp.sum(dhv * (xh * gv), axis=0, keepdims=True)
        dn = dhv * (1.0 + sc_ref[0])
        dg_ref[...] += jnp.sum(dn * xh, axis=0, keepdims=True)
        dxh = dn * gv
        dx = r * (dxh - xh * jnp.mean(dxh * xh, axis=-1, keepdims=True))
        if has_res:
            dx = dx + dres_ref[...]
        dx_ref[...] = dx
        if has_gate:
            do_ref, dgate_ref = refs[p:p + 2]
            do_ref[...] = (dx * gate_ref[0]).astype(BF16)
            dgate_ref[0] += jnp.sum(dx * o_ref[...], axis=0, keepdims=True)

    row = pl.BlockSpec((tr, w), lambda i: (i, 0))
    vec = pl.BlockSpec((1, w), lambda i: (0, 0))
    per_ex = pl.BlockSpec((1, 1, w), lambda i: (i // nb, 0, 0))
    ins, in_specs = [x, g, scale, dh], [row, vec, per_ex, row]
    if has_res:
        ins.append(dres)
        in_specs.append(row)
    if has_gate:
        ins += list(gate_o)
        in_specs += [row, per_ex]
    ex_shape = jax.ShapeDtypeStruct((nex, 1, w), F32)
    out_shape = [jax.ShapeDtypeStruct((t, w), F32), ex_shape, ex_shape, jax.ShapeDtypeStruct((1, w), F32)]
    out_specs = [row, per_ex, per_ex, vec]
    if has_gate:
        out_shape += [jax.ShapeDtypeStruct((t, w), BF16), ex_shape]
        out_specs += [row, per_ex]
    return pl.pallas_call(body, name=name, grid=(t // tr,), in_specs=in_specs, out_specs=out_specs,
                          out_shape=out_shape, compiler_params=_params("arbitrary"))(*ins)


def _gelu_parts(xv):
    cdf = 0.5 * (1.0 + lax.erf(xv * (1.0 / math.sqrt(2.0))))
    return cdf


def _tril_mask():
    r = lax.broadcasted_iota(jnp.int32, (CHUNK, CHUNK), 0)
    c = lax.broadcasted_iota(jnp.int32, (CHUNK, CHUNK), 1)
    return c <= r


def _gmlp_fwd(uv, g_v, w_s, b_col):
    t = uv.shape[0]
    gw = GM_WIDTH // GM_GROUPS
    nck = 2
    tr = nck * CHUNK

    def body(uv_ref, gv_ref, w_ref, b_ref, o_ref):
        mask = _tril_mask()
        for ck in range(nck):
            rows = pl.ds(ck * CHUNK, CHUNK)
            xv = uv_ref[rows, :]
            z = xv * _gelu_parts(xv)
            u, v = z[:, :GM_WIDTH], z[:, GM_WIDTH:]
            r = lax.rsqrt(jnp.mean(v * v, axis=-1, keepdims=True) + EPS)
            vn = (v * r * gv_ref[...]).astype(BF16)
            for gi in range(GM_GROUPS):
                cols = slice(gi * gw, (gi + 1) * gw)
                wg = jnp.where(mask, w_ref[gi], 0.0).astype(BF16)
                mx = jnp.dot(wg, vn[:, cols], preferred_element_type=F32) + b_ref[gi]
                o_ref[rows, cols] = (u[:, cols] * mx).astype(BF16)

    return pl.pallas_call(
        body, name="gmlp_fwd", grid=(t // tr,),
        in_specs=[pl.BlockSpec((tr, 2 * GM_WIDTH), lambda i: (i, 0)),
                  pl.BlockSpec((1, GM_WIDTH), lambda i: (0, 0)),
                  pl.BlockSpec((GM_GROUPS, CHUNK, CHUNK), lambda i: (0, 0, 0)),
                  pl.BlockSpec((GM_GROUPS, CHUNK, 1), lambda i: (0, 0, 0))],
        out_specs=pl.BlockSpec((tr, GM_WIDTH), lambda i: (i, 0)),
        out_shape=jax.ShapeDtypeStruct((t, GM_WIDTH), BF16),
        compiler_params=_params("parallel"))(uv, g_v, w_s, b_col)


def _gmlp_bwd(uv, g_v, w_s, b_col, da):
    t = uv.shape[0]
    gw = GM_WIDTH // GM_GROUPS
    nck = 2
    tr = nck * CHUNK
    inv_sqrt_2pi = 1.0 / math.sqrt(2.0 * math.pi)

    def body(uv_ref, gv_ref, w_ref, b_ref, da_ref, duv_ref, dgv_ref, dw_ref, db_ref, dvn_ref):
        @pl.when(pl.program_id(0) == 0)
        def _():
            dgv_ref[...] = jnp.zeros_like(dgv_ref)
            dw_ref[...] = jnp.zeros_like(dw_ref)
            db_ref[...] = jnp.zeros_like(db_ref)

        mask = _tril_mask()
        for ck in range(nck):
            rows = pl.ds(ck * CHUNK, CHUNK)
            xv = uv_ref[rows, :]
            cdf = _gelu_parts(xv)
            z = xv * cdf
            u, v = z[:, :GM_WIDTH], z[:, GM_WIDTH:]
            r = lax.rsqrt(jnp.mean(v * v, axis=-1, keepdims=True) + EPS)
            vh = v * r
            gv = gv_ref[...]
            vn = (vh * gv).astype(BF16)
            dav = da_ref[rows, :]
            du_parts = []
            for gi in range(GM_GROUPS):
                cols = slice(gi * gw, (gi + 1) * gw)
                wg = jnp.where(mask, w_ref[gi], 0.0).astype(BF16)
                vng = vn[:, cols]
                mx = jnp.dot(wg, vng, preferred_element_type=F32) + b_ref[gi]
                du_parts.append(dav[:, cols] * mx)
                dmx = dav[:, cols] * u[:, cols]
                db_ref[gi] += jnp.sum(dmx, axis=1, keepdims=True)
                dmb = dmx.astype(BF16)
                dwg = lax.dot_general(dmb, vng, (((1,), (1,)), ((), ())), preferred_element_type=F32)
                dw_ref[gi] += jnp.where(mask, dwg, 0.0)
                dvn_ref[:, cols] = lax.dot_general(wg, dmb, (((0,), (0,)), ((), ())),
                                                   preferred_element_type=F32)
            dvn = dvn_ref[...]
            dgv_ref[...] += jnp.sum(dvn * vh, axis=0, keepdims=True)
            dvh = dvn * gv
            dv = r * (dvh - vh * jnp.mean(dvh * vh, axis=-1, keepdims=True))
            dz = jnp.concatenate(du_parts + [dv], axis=1)
            dgelu = cdf + xv * (jnp.exp(-0.5 * xv * xv) * inv_sqrt_2pi)
            duv_ref[rows, :] = (dz * dgelu).astype(BF16)

    return pl.pallas_call(
        body, name="gmlp_bwd", grid=(t // tr,),
        in_specs=[pl.BlockSpec((tr, 2 * GM_WIDTH), lambda i: (i, 0)),
                  pl.BlockSpec((1, GM_WIDTH), lambda i: (0, 0)),
                  pl.BlockSpec((GM_GROUPS, CHUNK, CHUNK), lambda i: (0, 0, 0)),
                  pl.BlockSpec((GM_GROUPS, CHUNK, 1), lambda i: (0, 0, 0)),
                  pl.BlockSpec((tr, GM_WIDTH), lambda i: (i, 0))],
        out_specs=[pl.BlockSpec((tr, 2 * GM_WIDTH), lambda i: (i, 0)),
                   pl.BlockSpec((1, GM_WIDTH), lambda i: (0, 0)),
                   pl.BlockSpec((GM_GROUPS, CHUNK, CHUNK), lambda i: (0, 0, 0)),
                   pl.BlockSpec((GM_GROUPS, CHUNK, 1), lambda i: (0, 0, 0))],
        out_shape=[jax.ShapeDtypeStruct((t, 2 * GM_WIDTH), BF16),
                   jax.ShapeDtypeStruct((1, GM_WIDTH), F32),
                   jax.ShapeDtypeStruct((GM_GROUPS, CHUNK, CHUNK), F32),
                   jax.ShapeDtypeStruct((GM_GROUPS, CHUNK, 1), F32)],
        scratch_shapes=[pltpu.VMEM((CHUNK, GM_WIDTH), F32)],
        compiler_params=_params("arbitrary"))(uv, g_v, w_s, b_col, da)


def _rope_tables(pos_ref, invf_ref):
    ang = pos_ref[0] * invf_ref[...]
    cs, sn = jnp.cos(ang), jnp.sin(ang)
    lane = lax.broadcasted_iota(jnp.int32, (1, LANE), 1)
    half = QK_ROPE // 2
    c_tab = jnp.where(lane < QK_ROPE, cs, 0.0)
    sp_tab = jnp.where((lane >= half) & (lane < QK_ROPE), sn, 0.0)
    sm_tab = jnp.where(lane < half, -sn, 0.0)
    return c_tab, sp_tab, sm_tab


def _headnorm_fwd(x3, g_pad, pos, invf, *, name):
    bh, s, _ = x3.shape
    ts = _tile(s, 512, 8)
    half = QK_ROPE // 2
    width = QK_NOPE + QK_ROPE

    def body(x_ref, g_ref, pos_ref, invf_ref, o_ref):
        xv = x_ref[0]
        r = lax.rsqrt(jnp.sum(xv * xv, axis=-1, keepdims=True) * (1.0 / width) + EPS)
        y = xv * r * g_ref[...]
        c_tab, sp_tab, sm_tab = _rope_tables(pos_ref, invf_ref)
        hi = y[:, QK_NOPE:]
        hi = hi * c_tab + pltpu.roll(hi, half, 1) * sp_tab + pltpu.roll(hi, LANE - half, 1) * sm_tab
        o_ref[0, :, :QK_NOPE] = y[:, :QK_NOPE].astype(BF16)
        o_ref[0, :, QK_NOPE:] = hi.astype(BF16)

    return pl.pallas_call(
        body, name=name, grid=(bh, s // ts),
        in_specs=[pl.BlockSpec((1, ts, QK_PAD), lambda b, i: (b, i, 0)),
                  pl.BlockSpec((1, QK_PAD), lambda b, i: (0, 0)),
                  pl.BlockSpec((1, ts, 1), lambda b, i: (b // MLA_HEADS, i, 0)),
                  pl.BlockSpec((1, LANE), lambda b, i: (0, 0))],
        out_specs=pl.BlockSpec((1, ts, QK_PAD), lambda b, i: (b, i, 0)),
        out_shape=jax.ShapeDtypeStruct(x3.shape, BF16),
        compiler_params=_params("parallel", "parallel"))(x3, g_pad, pos, invf)


def _headnorm_bwd(x3, g_pad, pos, invf, dout, *, name):
    bh, s, _ = x3.shape
    ts = _tile(s, 512, 8)
    half = QK_ROPE // 2
    width = QK_NOPE + QK_ROPE

    def body(x_ref, g_ref, pos_ref, invf_ref, do_ref, dx_ref, dg_ref):
        @pl.when((pl.program_id(0) == 0) & (pl.program_id(1) == 0))
        def _():
            dg_ref[...] = jnp.zeros_like(dg_ref)

        xv = x_ref[0]
        gv = g_ref[...]
        r = lax.rsqrt(jnp.sum(xv * xv, axis=-1, keepdims=True) * (1.0 / width) + EPS)
        xh = xv * r
        c_tab, sp_tab, sm_tab = _rope_tables(pos_ref, invf_ref)
        dov = do_ref[0]
        dhi = dov[:, QK_NOPE:]
        dhi = dhi * c_tab + pltpu.roll(dhi * sp_tab, LANE - half, 1) + pltpu.roll(dhi * sm_tab, half, 1)
        dy = jnp.concatenate([dov[:, :QK_NOPE], dhi], axis=1)
        dg_ref[...] += jnp.sum(dy * xh, axis=0, keepdims=True)
        dyh = dy * gv
        dx_ref[0] = r * (dyh - xh * (jnp.sum(dyh * xh, axis=-1, keepdims=True) * (1.0 / width)))

    return pl.pallas_call(
        body, name=name, grid=(bh, s // ts),
        in_specs=[pl.BlockSpec((1, ts, QK_PAD), lambda b, i: (b, i, 0)),
                  pl.BlockSpec((1, QK_PAD), lambda b, i: (0, 0)),
                  pl.BlockSpec((1, ts, 1), lambda b, i: (b // MLA_HEADS, i, 0)),
                  pl.BlockSpec((1, LANE), lambda b, i: (0, 0)),
                  pl.BlockSpec((1, ts, QK_PAD), lambda b, i: (b, i, 0))],
        out_specs=[pl.BlockSpec((1, ts, QK_PAD), lambda b, i: (b, i, 0)),
                   pl.BlockSpec((1, QK_PAD), lambda b, i: (0, 0))],
        out_shape=[jax.ShapeDtypeStruct(x3.shape, F32), jax.ShapeDtypeStruct((1, QK_PAD), F32)],
        compiler_params=_params("arbitrary", "arbitrary"))(x3, g_pad, pos, invf, dout)


def _causal_probs(q, k, q0):
    scale = (QK_NOPE + QK_ROPE) ** -0.5
    sc = lax.dot_general(q, k, (((1,), (1,)), ((), ())), preferred_element_type=F32) * scale
    qi = q0 + lax.broadcasted_iota(jnp.int32, sc.shape, 0)
    ki = lax.broadcasted_iota(jnp.int32, sc.shape, 1)
    sc = jnp.where(ki <= qi, sc, -1e30)
    e = jnp.exp(sc - jnp.max(sc, axis=-1, keepdims=True))
    return e / jnp.sum(e, axis=-1, keepdims=True)


def _attn_fwd(qn, kn, v3):
    bh, s, _ = qn.shape
    tq = _tile(s, 256, 8)

    def body(q_ref, k_ref, v_ref, o_ref):
        p = _causal_probs(q_ref[0], k_ref[0], pl.program_id(1) * tq)
        o_ref[0] = jnp.dot(p.astype(BF16), v_ref[0], preferred_element_type=F32).astype(BF16)

    return pl.pallas_call(
        body, name="attn_fwd", grid=(bh, s // tq),
        in_specs=[pl.BlockSpec((1, tq, QK_PAD), lambda b, i: (b, i, 0)),
                  pl.BlockSpec((1, s, QK_PAD), lambda b, i: (b, 0, 0)),
                  pl.BlockSpec((1, s, V_HEAD), lambda b, i: (b, 0, 0))],
        out_specs=pl.BlockSpec((1, tq, V_HEAD), lambda b, i: (b, i, 0)),
        out_shape=jax.ShapeDtypeStruct((bh, s, V_HEAD), BF16),
        compiler_params=_params("parallel", "parallel"))(qn, kn, v3)


def _attn_bwd(qn, kn, v3, do3):
    bh, s, _ = qn.shape
    tq = _tile(s, 256, 8)
    scale = (QK_NOPE + QK_ROPE) ** -0.5

    def body(q_ref, k_ref, v_ref, do_ref, dq_ref, dk_ref, dv_ref):
        @pl.when(pl.program_id(1) == 0)
        def _():
            dk_ref[...] = jnp.zeros_like(dk_ref)
            dv_ref[...] = jnp.zeros_like(dv_ref)

        q, k, v, do = q_ref[0], k_ref[0], v_ref[0], do_ref[0]
        p = _causal_probs(q, k, pl.program_id(1) * tq)
        dv_ref[0] += lax.dot_general(p.astype(BF16), do, (((0,), (0,)), ((), ())), preferred_element_type=F32)
        dp = lax.dot_general(do, v, (((1,), (1,)), ((), ())), preferred_element_type=F32)
        ds = (p * (dp - jnp.sum(p * dp, axis=-1, keepdims=True)) * scale).astype(BF16)
        dq_ref[0] = jnp.dot(ds, k, preferred_element_type=F32)
        dk_ref[0] += lax.dot_general(ds, q, (((0,), (0,)), ((), ())), preferred_element_type=F32)

    return pl.pallas_call(
        body, name="attn_bwd", grid=(bh, s // tq),
        in_specs=[pl.BlockSpec((1, tq, QK_PAD), lambda b, i: (b, i, 0)),
                  pl.BlockSpec((1, s, QK_PAD), lambda b, i: (b, 0, 0)),
                  pl.BlockSpec((1, s, V_HEAD), lambda b, i: (b, 0, 0)),
                  pl.BlockSpec((1, tq, V_HEAD), lambda b, i: (b, i, 0))],
        out_specs=[pl.BlockSpec((1, tq, QK_PAD), lambda b, i: (b, i, 0)),
                   pl.BlockSpec((1, s, QK_PAD), lambda b, i: (b, 0, 0)),
                   pl.BlockSpec((1, s, V_HEAD), lambda b, i: (b, 0, 0))],
        out_shape=[jax.ShapeDtypeStruct((bh, s, QK_PAD), F32),
                   jax.ShapeDtypeStruct((bh, s, QK_PAD), F32),
                   jax.ShapeDtypeStruct((bh, s, V_HEAD), F32)],
        compiler_params=_params("parallel", "arbitrary"))(qn, kn, v3, do3)


def _gatemix_fwd(gpa, gpb, ya, yb):
    t, w = ya.shape
    tr = _tile(t, 128, 8)

    def body(ga_ref, gb_ref, ya_ref, yb_ref, o_ref):
        o_ref[...] = (jax.nn.sigmoid(ga_ref[...]) * ya_ref[...]
                      + jax.nn.sigmoid(gb_ref[...]) * yb_ref[...]).astype(BF16)

    row = pl.BlockSpec((tr, w), lambda i: (i, 0))
    return pl.pallas_call(body, name="gatemix_fwd", grid=(t // tr,), in_specs=[row] * 4, out_specs=row,
                          out_shape=jax.ShapeDtypeStruct((t, w), BF16),
                          compiler_params=_params("parallel"))(gpa, gpb, ya, yb)


def _gatemix_bwd(gpa, gpb, ya, yb, dmix):
    t, w = ya.shape
    tr = _tile(t, 128, 8)

    def body(ga_ref, gb_ref, ya_ref, yb_ref, dm_ref, dya_ref, dyb_ref, dga_ref, dgb_ref):
        dm = dm_ref[...]
        sa = jax.nn.sigmoid(ga_ref[...])
        sb = jax.nn.sigmoid(gb_ref[...])
        dya_ref[...] = (dm * sa).astype(BF16)
        dyb_ref[...] = (dm * sb).astype(BF16)
        dga_ref[...] = (dm * ya_ref[...] * sa * (1.0 - sa)).astype(BF16)
        dgb_ref[...] = (dm * yb_ref[...] * sb * (1.0 - sb)).astype(BF16)

    row = pl.BlockSpec((tr, w), lambda i: (i, 0))
    return pl.pallas_call(body, name="gatemix_bwd", grid=(t // tr,), in_specs=[row] * 5, out_specs=[row] * 4,
                          out_shape=[jax.ShapeDtypeStruct((t, w), BF16)] * 4,
                          compiler_params=_params("parallel"))(gpa, gpb, ya, yb, dmix)


def _final(x1, f2, gate, target, seq):
    t, w = x1.shape
    nex = t // seq
    tr = _row_tile(seq, w, 1024 * 1024)
    nb = seq // tr

    def body(x_ref, f_ref, gate_ref, t_ref, loss_ref, dy_ref, dff_ref, dgate_ref):
        i = pl.program_id(0)

        @pl.when(i == 0)
        def _():
            loss_ref[...] = jnp.zeros_like(loss_ref)

        @pl.when(i % nb == 0)
        def _():
            dgate_ref[...] = jnp.zeros_like(dgate_ref)

        fv = f_ref[...]
        gv = gate_ref[0]
        err = x_ref[...] + gv * fv - t_ref[...]
        sq = jnp.sum(err * err, axis=1, keepdims=True)
        loss_ref[...] += jnp.sum(sq, axis=0, keepdims=True) * (0.5 / w)
        dy = err * (1.0 / w)
        dy_ref[...] = dy
        dff_ref[...] = (dy * gv).astype(BF16)
        dgate_ref[0] += jnp.sum(dy * fv, axis=0, keepdims=True)

    row = pl.BlockSpec((tr, w), lambda i: (i, 0))
    per_ex = pl.BlockSpec((1, 1, w), lambda i: (i // nb, 0, 0))
    return pl.pallas_call(
        body, name="loss_head", grid=(t // tr,),
        in_specs=[row, row, per_ex, row],
        out_specs=[pl.BlockSpec((1, LANE), lambda i: (0, 0)), row, row, per_ex],
        out_shape=[jax.ShapeDtypeStruct((1, LANE), F32), jax.ShapeDtypeStruct((t, w), F32),
                   jax.ShapeDtypeStruct((t, w), BF16), jax.ShapeDtypeStruct((nex, 1, w), F32)],
        compiler_params=_params("arbitrary"))(x1, f2, gate, target)


def _sum4(r4, *, name):
    _, r, c = r4.shape
    tr, tc = _tile(r, 256, 16), _tile(c, 1024, LANE)

    def body(r_ref, o_ref):
        acc = r_ref[0].astype(F32)
        for j in range(1, N_CHIP):
            acc = acc + r_ref[j].astype(F32)
        o_ref[...] = acc

    return pl.pallas_call(body, name=name, grid=(r // tr, c // tc),
                          in_specs=[pl.BlockSpec((N_CHIP, tr, tc), lambda i, j: (0, i, j))],
                          out_specs=pl.BlockSpec((tr, tc), lambda i, j: (i, j)),
                          out_shape=jax.ShapeDtypeStruct((r, c), F32),
                          compiler_params=_params("parallel", "parallel"))(r4)


def _adamw(w, m, v, parts, *, name):
    r, c = w.shape
    tc = _tile(c, 1024, LANE) if c % LANE == 0 else c
    tr = _tile(r, max(8, (256 * 1024) // tc // 8 * 8), 8)
    npart = len(parts)
    b1c = 1.0 - ADAM_B1 ** ADAM_STEP
    b2c = 1.0 - ADAM_B2 ** ADAM_STEP

    def body(*refs):
        w_ref, m_ref, v_ref = refs[:3]
        g_ref, d_ref, mo_ref, vo_ref = refs[3 + npart:]
        g = refs[3][...].astype(F32)
        for p_ref in refs[4:3 + npart]:
            g = g + p_ref[...].astype(F32)
        m2 = ADAM_B1 * m_ref[...] + (1.0 - ADAM_B1) * g
        v2 = ADAM_B2 * v_ref[...] + (1.0 - ADAM_B2) * jnp.square(g)
        m_hat = m2 / b1c
        v_hat = v2 / b2c
        g_ref[...] = g
        d_ref[...] = -ADAM_LR * (m_hat / (jnp.sqrt(v_hat) + ADAM_EPS) + ADAM_WD * w_ref[...])
        mo_ref[...] = m2
        vo_ref[...] = v2

    blk = pl.BlockSpec((tr, tc), lambda i, j: (i, j))
    return pl.pallas_call(body, name=name, grid=(r // tr, c // tc),
                          in_specs=[blk] * (3 + npart), out_specs=[blk] * 4,
                          out_shape=[jax.ShapeDtypeStruct((r, c), F32)] * 4,
                          compiler_params=_params("parallel", "parallel"))(w, m, v, *parts)


def _flip(pos, k):
    ix, iy, ic = pos
    return (1 - ix if k & 4 else ix, 1 - iy if k & 2 else iy, 1 - ic if k & 1 else ic)


def _allgather8(x, *, name):
    def body(x_ref, o_ref, ssem, rsem):
        pos = _mesh_pos()
        me = 4 * pos[0] + 2 * pos[1] + pos[2]
        o_ref[me] = x_ref[...]
        cps = []
        for k in range(1, N_DEV):
            cp = pltpu.make_async_remote_copy(src_ref=x_ref, dst_ref=o_ref.at[me], send_sem=ssem.at[k - 1],
                                              recv_sem=rsem.at[k - 1], device_id=_flip(pos, k), device_id_type=MESH)
            cp.start()
            cps.append(cp)
        for cp in cps:
            cp.wait()

    return pl.pallas_call(
        body, name=name,
        out_shape=jax.ShapeDtypeStruct((N_DEV,) + x.shape, x.dtype),
        in_specs=[pl.BlockSpec(memory_space=pltpu.VMEM)],
        out_specs=pl.BlockSpec(memory_space=pltpu.VMEM),
        scratch_shapes=[pltpu.SemaphoreType.DMA((N_DEV - 1,)), pltpu.SemaphoreType.DMA((N_DEV - 1,))],
        compiler_params=pltpu.CompilerParams(vmem_limit_bytes=VMEM_LIMIT),
    )(x)


def _window(ref, axis, chip, size):
    start = pl.multiple_of(chip * size, LANE if axis == 1 else 16)
    if axis == 1:
        return ref.at[:, pl.ds(start, size)]
    return ref.at[pl.ds(start, size), :]


def _gather_weight(wsh, axis, *, name):
    size = wsh.shape[axis]
    full = tuple(d * N_CHIP if a == axis else d for a, d in enumerate(wsh.shape))

    def body(w_ref, o_ref, ssem, rsem, lsem):
        pos = _mesh_pos()
        chip = 2 * pos[0] + pos[1]
        mine = _window(o_ref, axis, chip, size)
        loc = pltpu.make_async_copy(w_ref, mine, lsem)
        loc.start()
        cps = []
        for j in range(1, N_CHIP):
            cp = pltpu.make_async_remote_copy(src_ref=w_ref, dst_ref=mine, send_sem=ssem.at[j - 1],
                                              recv_sem=rsem.at[j - 1], device_id=_flip(pos, 2 * j), device_id_type=MESH)
            cp.start()
            cps.append(cp)
        for cp in cps:
            cp.wait()
        loc.wait()

    return pl.pallas_call(
        body, name=name,
        out_shape=jax.ShapeDtypeStruct(full, wsh.dtype),
        in_specs=[pl.BlockSpec(memory_space=pl.ANY)],
        out_specs=pl.BlockSpec(memory_space=pl.ANY),
        scratch_shapes=[pltpu.SemaphoreType.DMA((N_CHIP - 1,)), pltpu.SemaphoreType.DMA((N_CHIP - 1,)),
                        pltpu.SemaphoreType.DMA],
    )(wsh)


def _scatter_grad(g, axis, *, name):
    size = g.shape[axis] // N_CHIP
    shard = tuple(size if a == axis else d for a, d in enumerate(g.shape))

    def body(g_ref, o_ref, ssem, rsem, lsem):
        pos = _mesh_pos()
        loc = pltpu.make_async_copy(_window(g_ref, axis, 2 * pos[0] + pos[1], size), o_ref.at[0], lsem)
        loc.start()
        cps = []
        for j in range(1, N_CHIP):
            peer = _flip(pos, 2 * j)
            cp = pltpu.make_async_remote_copy(src_ref=_window(g_ref, axis, 2 * peer[0] + peer[1], size),
                                              dst_ref=o_ref.at[j], send_sem=ssem.at[j - 1],
                                              recv_sem=rsem.at[j - 1], device_id=peer, device_id_type=MESH)
            cp.start()
            cps.append(cp)
        for cp in cps:
            cp.wait()
        loc.wait()

    return pl.pallas_call(
        body, name=name,
        out_shape=jax.ShapeDtypeStruct((N_CHIP,) + shard, g.dtype),
        in_specs=[pl.BlockSpec(memory_space=pl.ANY)],
        out_specs=pl.BlockSpec(memory_space=pl.ANY),
        scratch_shapes=[pltpu.SemaphoreType.DMA((N_CHIP - 1,)), pltpu.SemaphoreType.DMA((N_CHIP - 1,)),
                        pltpu.SemaphoreType.DMA],
    )(g)


def _sibling_swap(p, *, name):
    def body(p_ref, o_ref, ssem, rsem):
        cp = pltpu.make_async_remote_copy(src_ref=p_ref, dst_ref=o_ref, send_sem=ssem, recv_sem=rsem,
                                          device_id=_flip(_mesh_pos(), 1), device_id_type=MESH)
        cp.start()
        cp.wait()

    return pl.pallas_call(
        body, name=name,
        out_shape=jax.ShapeDtypeStruct(p.shape, p.dtype),
        in_specs=[pl.BlockSpec(memory_space=pl.ANY)],
        out_specs=pl.BlockSpec(memory_space=pl.ANY),
        scratch_shapes=[pltpu.SemaphoreType.DMA, pltpu.SemaphoreType.DMA],
    )(p)


_SMALL = ("b_ada", "g_norm1", "g_v", "w_s", "b_s", "g_q_lat", "g_kv_lat", "g_qn", "g_kn", "g_norm2")
_BIG = ("w_in", "w_uq", "w_ukv", "w_branch_a", "w_branch_b", "w_out", "w_ff1", "w_ff2")
_SHARD_AXIS = {"w_in": 1, "w_uq": 1, "w_ukv": 1, "w_branch_a": 1, "w_branch_b": 0, "w_out": 0, "w_ff1": 1, "w_ff2": 0}
_WEIGHTS = ("w_ada", "b_ada", "g_norm1", "w_in", "g_v", "w_s", "b_s", "g_q_lat", "g_kv_lat", "w_uq", "w_ukv",
            "g_qn", "g_kn", "w_branch_a", "w_branch_b", "w_out", "g_norm2", "w_ff1", "w_ff2")


def kernel(x, c, positions, w_ada, b_ada, g_norm1, w_in, g_v, w_s, b_s, g_q_lat, g_kv_lat, w_uq, w_ukv, g_qn, g_kn, w_branch_a, w_branch_b, w_out, g_norm2, w_ff1, w_ff2, loss_target, m_w_ada, m_b_ada, m_g_norm1, m_w_in, m_g_v, m_w_s, m_b_s, m_g_q_lat, m_g_kv_lat, m_w_uq, m_w_ukv, m_g_qn, m_g_kn, m_w_branch_a, m_w_branch_b, m_w_out, m_g_norm2, m_w_ff1, m_w_ff2, v_w_ada, v_b_ada, v_g_norm1, v_w_in, v_g_v, v_w_s, v_b_s, v_g_q_lat, v_g_kv_lat, v_w_uq, v_w_ukv, v_g_qn, v_g_kn, v_w_branch_a, v_w_branch_b, v_w_out, v_g_norm2, v_w_ff1, v_w_ff2):
    wts = dict(w_ada=w_ada, b_ada=b_ada, g_norm1=g_norm1, w_in=w_in, g_v=g_v, w_s=w_s, b_s=b_s, g_q_lat=g_q_lat,
               g_kv_lat=g_kv_lat, w_uq=w_uq, w_ukv=w_ukv, g_qn=g_qn, g_kn=g_kn, w_branch_a=w_branch_a,
               w_branch_b=w_branch_b, w_out=w_out, g_norm2=g_norm2, w_ff1=w_ff1, w_ff2=w_ff2)
    mom1 = dict(w_ada=m_w_ada, b_ada=m_b_ada, g_norm1=m_g_norm1, w_in=m_w_in, g_v=m_g_v, w_s=m_w_s, b_s=m_b_s,
                g_q_lat=m_g_q_lat, g_kv_lat=m_g_kv_lat, w_uq=m_w_uq, w_ukv=m_w_ukv, g_qn=m_g_qn, g_kn=m_g_kn,
                w_branch_a=m_w_branch_a, w_branch_b=m_w_branch_b, w_out=m_w_out, g_norm2=m_g_norm2,
                w_ff1=m_w_ff1, w_ff2=m_w_ff2)
    mom2 = dict(w_ada=v_w_ada, b_ada=v_b_ada, g_norm1=v_g_norm1, w_in=v_w_in, g_v=v_g_v, w_s=v_w_s, b_s=v_b_s,
                g_q_lat=v_g_q_lat, g_kv_lat=v_g_kv_lat, w_uq=v_w_uq, w_ukv=v_w_ukv, g_qn=v_g_qn, g_kn=v_g_kn,
                w_branch_a=v_w_branch_a, w_branch_b=v_w_branch_b, w_out=v_w_out, g_norm2=v_g_norm2,
                w_ff1=v_w_ff1, w_ff2=v_w_ff2)

    nb, seq, dm = x.shape
    t = nb * seq
    nh, qkh = MLA_HEADS, QK_NOPE + QK_ROPE
    bh = nb * nh
    off_q = 2 * GM_WIDTH
    off_kv = off_q + Q_LORA
    off_kpe = off_kv + KV_LORA
    off_gate = off_kpe + QK_ROPE
    in_cols = off_gate + 2 * dm
    ins = in_cols // N_CHIP
    insp = _round_up(ins, LANE)
    ada_cols = N_MOD * dm // N_CHIP

    ix, iy, ic = _mesh_pos()
    me = 4 * ix + 2 * iy + ic
    chip = 2 * ix + iy
    xf = x.reshape(t, dm)
    tgt = loss_target.reshape(t, dm)

    c_all = _allgather8(c, name="ag_cond").reshape(N_DEV * nb, dm)
    cond = _silu(c_all)
    b_sh = lax.dynamic_slice(b_ada, (0, chip * ada_cols), (1, ada_cols))
    mod_sh = _mm(cond, w_ada[0], name="ada_fwd", tn=2048) + b_sh
    mod8 = _allgather8(mod_sh, name="ag_mod")
    mod_all = jnp.concatenate([mod8[2 * s] for s in range(N_CHIP)], axis=1)
    mod = lax.dynamic_slice(mod_all, (nb * me, 0), (nb, N_MOD * dm))
    sh1, sc1, ga1, sh2, sc2, ga2 = [mod[:, j * dm:(j + 1) * dm].reshape(nb, 1, dm) for j in range(N_MOD)]

    shards = {n: wts[n][0].astype(BF16) for n in _BIG}
    shards["w_in"] = jnp.pad(shards["w_in"], ((0, 0), (0, insp - ins)))
    full = {n: _gather_weight(shards[n], _SHARD_AXIS[n], name="ag_" + n) for n in _BIG}

    h1 = _norm_mod_fwd(xf, g_norm1, sh1, sc1, seq, name="norm1_fwd")
    proj_p = _mm(h1, full["w_in"], name="in_fwd")
    proj = proj_p.reshape(t, N_CHIP, insp)[:, :, :ins].reshape(t, in_cols)
    uv, q_lat, kv_lat = proj[:, :off_q], proj[:, off_q:off_kv], proj[:, off_kv:off_kpe]
    k_pe, gpa, gpb = proj[:, off_kpe:off_gate], proj[:, off_gate:off_gate + dm], proj[:, off_gate + dm:]

    b_col = b_s[0].reshape(GM_GROUPS, CHUNK, 1)
    a_out = _gmlp_fwd(uv, g_v, w_s[0], b_col)
    y_a = _mm(a_out, full["w_branch_a"], name="ba_fwd")

    zq = jnp.zeros((nb, 1, Q_LORA), F32)
    zkv = jnp.zeros((nb, 1, KV_LORA), F32)
    ql = _norm_mod_fwd(q_lat, g_q_lat, zq, zq, seq, name="qlat_norm_fwd")
    kvl = _norm_mod_fwd(kv_lat, g_kv_lat, zkv, zkv, seq, name="kvlat_norm_fwd")
    q_raw = _mm(ql, full["w_uq"], name="uq_fwd")
    kv_raw = _mm(kvl, full["w_ukv"], name="ukv_fwd")
    q3 = jnp.pad(q_raw.reshape(nb, seq, nh, qkh).transpose(0, 2, 1, 3),
                 ((0, 0), (0, 0), (0, 0), (0, QK_PAD - qkh))).reshape(bh, seq, QK_PAD)
    kv4 = kv_raw.reshape(nb, seq, nh, QK_NOPE + V_HEAD).transpose(0, 2, 1, 3)
    k3 = jnp.concatenate([kv4[..., :QK_NOPE],
                          jnp.broadcast_to(k_pe.reshape(nb, 1, seq, QK_ROPE), (nb, nh, seq, QK_ROPE)),
                          jnp.zeros((nb, nh, seq, QK_PAD - qkh), F32)], axis=-1).reshape(bh, seq, QK_PAD)
    v3 = kv4[..., QK_NOPE:].astype(BF16).reshape(bh, seq, V_HEAD)
    pos = positions.astype(F32).reshape(nb, seq, 1)
    inv_freq = 1.0 / (ROPE_THETA ** (jnp.arange(0, QK_ROPE, 2, dtype=F32) / QK_ROPE))
    invf = jnp.concatenate([inv_freq, inv_freq, jnp.zeros((LANE - QK_ROPE,), F32)]).reshape(1, LANE)
    gq_pad = jnp.pad(g_qn, ((0, 0), (0, QK_PAD - qkh)))
    gk_pad = jnp.pad(g_kn, ((0, 0), (0, QK_PAD - qkh)))
    qn = _headnorm_fwd(q3, gq_pad, pos, invf, name="qnorm_fwd")
    kn = _headnorm_fwd(k3, gk_pad, pos, invf, name="knorm_fwd")
    o3 = _attn_fwd(qn, kn, v3)
    attn = o3.reshape(nb, nh, seq, V_HEAD).transpose(0, 2, 1, 3).reshape(t, nh * V_HEAD)
    y_b = _mm(attn, full["w_branch_b"], name="bb_fwd")

    mixed = _gatemix_fwd(gpa, gpb, y_a, y_b)
    o1 = _mm(mixed, full["w_out"], name="out_fwd")
    x1, h2 = _norm_mod_fwd(xf, g_norm2, sh2, sc2, seq, name="norm2_fwd", resid=(o1, ga1))
    f1, act = _mm(h2, full["w_ff1"], name="ff1_fwd", out_dtypes=(BF16, BF16), epilogue=_relu2_epilogue)
    f2 = _mm(act, full["w_ff2"], name="ff2_fwd")
    loss_acc, dy, dff, dga2 = _final(x1, f2, ga2, tgt, seq)
    loss = lax.psum(loss_acc[0, 0], ("x", "y", "c"))

    gw = {}
    gw["w_ff2"] = _mm(act, dff, name="ff2_dw", ta=True, out_dtypes=(BF16,))
    df1 = _mm(dff, full["w_ff2"], name="ff2_dx", tb=True, out_dtypes=(BF16,), epilogue=_relu2_bwd_epilogue,
              extras=(f1,))
    gw["w_ff1"] = _mm(h2, df1, name="ff1_dw", ta=True, out_dtypes=(BF16,))
    dh2 = _mm(df1, full["w_ff1"], name="ff1_dx", tb=True)
    dx1, dsh2, dsc2, dg_norm2, do1, dga1 = _norm_mod_bwd(x1, g_norm2, sc2, dh2, seq, name="norm2_bwd", dres=dy,
                                                         gate_o=(o1, ga1))
    gw["w_out"] = _mm(mixed, do1, name="out_dw", ta=True, out_dtypes=(BF16,))
    dmixed = _mm(do1, full["w_out"], name="out_dx", tb=True)
    dya, dyb, dgpa, dgpb = _gatemix_bwd(gpa, gpb, y_a, y_b, dmixed)

    gw["w_branch_b"] = _mm(attn, dyb, name="bb_dw", ta=True, out_dtypes=(BF16,))
    dattn = _mm(dyb, full["w_branch_b"], name="bb_dx", tb=True, out_dtypes=(BF16,))
    do3 = dattn.reshape(nb, seq, nh, V_HEAD).transpose(0, 2, 1, 3).reshape(bh, seq, V_HEAD)
    dqn, dkn, dv3 = _attn_bwd(qn, kn, v3, do3)
    dq3, dg_qn = _headnorm_bwd(q3, gq_pad, pos, invf, dqn, name="qnorm_bwd")
    dk3, dg_kn = _headnorm_bwd(k3, gk_pad, pos, invf, dkn, name="knorm_bwd")
    dq_raw = dq3.reshape(nb, nh, seq, QK_PAD)[..., :qkh].transpose(0, 2, 1, 3).reshape(t, nh * qkh).astype(BF16)
    dk4 = dk3.reshape(nb, nh, seq, QK_PAD)
    dk_pe = jnp.sum(dk4[..., QK_NOPE:qkh], axis=1).reshape(t, QK_ROPE)
    dkv_raw = jnp.concatenate([dk4[..., :QK_NOPE], dv3.reshape(nb, nh, seq, V_HEAD)], axis=-1)
    dkv_raw = dkv_raw.transpose(0, 2, 1, 3).reshape(t, nh * (QK_NOPE + V_HEAD)).astype(BF16)
    gw["w_uq"] = _mm(ql, dq_raw, name="uq_dw", ta=True, out_dtypes=(BF16,))
    dql = _mm(dq_raw, full["w_uq"], name="uq_dx", tb=True)
    gw["w_ukv"] = _mm(kvl, dkv_raw, name="ukv_dw", ta=True, out_dtypes=(BF16,))
    dkvl = _mm(dkv_raw, full["w_ukv"], name="ukv_dx", tb=True)
    dq_lat, _, _, dg_q_lat = _norm_mod_bwd(q_lat, g_q_lat, zq, dql, seq, name="qlat_norm_bwd")
    dkv_lat, _, _, dg_kv_lat = _norm_mod_bwd(kv_lat, g_kv_lat, zkv, dkvl, seq, name="kvlat_norm_bwd")

    gw["w_branch_a"] = _mm(a_out, dya, name="ba_dw", ta=True, out_dtypes=(BF16,))
    da = _mm(dya, full["w_branch_a"], name="ba_dx", tb=True)
    duv, dg_v, dw_s, db_col = _gmlp_bwd(uv, g_v, w_s[0], b_col, da)

    dproj = jnp.concatenate([duv, dq_lat.astype(BF16), dkv_lat.astype(BF16), dk_pe.astype(BF16), dgpa, dgpb], axis=1)
    dproj_p = jnp.pad(dproj.reshape(t, N_CHIP, ins), ((0, 0), (0, 0), (0, insp - ins))).reshape(t, N_CHIP * insp)
    gw["w_in"] = _mm(h1, dproj_p, name="in_dw", ta=True, out_dtypes=(BF16,))
    dh1 = _mm(dproj_p, full["w_in"], name="in_dx", tb=True)
    dx, dsh1, dsc1, dg_norm1 = _norm_mod_bwd(xf, g_norm1, sc1, dh1, seq, name="norm1_bwd", dres=dx1)
    grad_x = dx.reshape(nb, seq, dm)

    gmod = jnp.concatenate([dsh1, dsc1, dga1, dsh2, dsc2, dga2], axis=-1).reshape(nb, N_MOD * dm)
    gmod_all = _allgather8(gmod, name="ag_gmod").reshape(N_DEV * nb, N_MOD * dm)
    gmod_sh = lax.dynamic_slice(gmod_all, (0, chip * ada_cols), (N_DEV * nb, ada_cols))
    g_w_ada = _mm(cond, gmod_sh, name="ada_dw", ta=True, tn=2048)

    out_g, out_d, out_m, out_v = {}, {}, {}, {}
    out_g["w_ada"], out_d["w_ada"], out_m["w_ada"], out_v["w_ada"] = _adamw(
        w_ada[0], m_w_ada[0], v_w_ada[0], [g_w_ada], name="adamw_w_ada")
    for n in _BIG:
        r4 = _scatter_grad(gw[n], _SHARD_AXIS[n], name="rs_" + n)
        part = _sum4(r4, name="sum4_" + n)
        other = _sibling_swap(part, name="swap_" + n)
        if n == "w_in":
            part, other = part[:, :ins], other[:, :ins]
        out_g[n], out_d[n], out_m[n], out_v[n] = _adamw(wts[n][0], mom1[n][0], mom2[n][0], [part, other],
                                                        name="adamw_" + n)

    small_g = dict(b_ada=jnp.sum(gmod, axis=0), g_norm1=dg_norm1, g_v=dg_v, w_s=dw_s, b_s=db_col,
                   g_q_lat=dg_q_lat, g_kv_lat=dg_kv_lat, g_qn=dg_qn[:, :qkh], g_kn=dg_kn[:, :qkh],
                   g_norm2=dg_norm2)

    def pack(d):
        flat = jnp.concatenate([d[n].reshape(-1) for n in _SMALL])
        return jnp.pad(flat, (0, _round_up(flat.shape[0], 8 * LANE) - flat.shape[0])).reshape(-1, LANE)

    g8 = _allgather8(pack(small_g), name="ag_small_grads")
    sg, sd, sm, sv = _adamw(pack({n: wts[n] for n in _SMALL}), pack({n: mom1[n] for n in _SMALL}),
                            pack({n: mom2[n] for n in _SMALL}), [g8[d] for d in range(N_DEV)], name="adamw_small")
    off = 0
    for n in _SMALL:
        size = wts[n].size
        for dst, src in ((out_g, sg), (out_d, sd), (out_m, sm), (out_v, sv)):
            dst[n] = src.reshape(-1)[off:off + size].reshape(wts[n].shape)
        off += size

    def shaped(d, n):
        return d[n].reshape(wts[n].shape)

    return (loss, grad_x, *[shaped(out_g, n) for n in _WEIGHTS], *[shaped(out_d, n) for n in _WEIGHTS],
            *[shaped(out_m, n) for n in _WEIGHTS], *[shaped(out_v, n) for n in _WEIGHTS])
```

```python
import functools
import math

import jax
import jax.numpy as jnp
from jax import lax
from jax.experimental import pallas as pl
from jax.experimental.pallas import tpu as pltpu

F32 = jnp.float32
BF16 = jnp.bfloat16

GM_WIDTH = 2048
GM_GROUPS = 8
CHUNK = 128
MLA_HEADS = 32
QK_NOPE = 128
QK_ROPE = 64
V_HEAD = 128
Q_LORA = 1024
KV_LORA = 512
ROPE_THETA = 10000.0
N_MOD = 6
EPS = 1e-6
ADAM_LR = 0.001
ADAM_B1 = 0.9
ADAM_B2 = 0.999
ADAM_EPS = 1e-08
ADAM_WD = 0.01
ADAM_STEP = 10

N_CHIP = 4
N_DEV = 8
LANE = 128
QK_PAD = 256
VMEM_LIMIT = 56 * 1024 * 1024
MESH = pl.DeviceIdType.MESH


def _round_up(n, m):
    return (n + m - 1) // m * m


def _tile(n, target, align):
    t = min(target, n) // align * align
    while t >= align:
        if n % t == 0:
            return t
        t -= align
    return n


def _params(*sem):
    return pltpu.CompilerParams(dimension_semantics=sem, vmem_limit_bytes=VMEM_LIMIT)


def _mesh_pos():
    return lax.axis_index("x"), lax.axis_index("y"), lax.axis_index("c")


def _flip(pos, k):
    ix, iy, ic = pos
    return (1 - ix if k & 4 else ix, 1 - iy if k & 2 else iy, 1 - ic if k & 1 else ic)


def _chip_of(pos):
    return 2 * pos[0] + pos[1]


def _window(ref, axis, chip, size):
    start = pl.multiple_of(chip * size, LANE if axis == 1 else 16)
    if axis == 1:
        return ref.at[:, pl.ds(start, size)]
    return ref.at[pl.ds(start, size), :]


def _remote(src, dst, sem, send, recv, peer):
    return pltpu.make_async_remote_copy(src_ref=src, dst_ref=dst, send_sem=sem.at[send], recv_sem=sem.at[recv],
                                        device_id=peer, device_id_type=MESH)


class _Gather:
    n_sem = 13

    def __init__(self, wsh, axis):
        self.axis, self.size, self.rows = axis, wsh.shape[axis], wsh.shape[0]
        self.inputs = [wsh]
        self.out_shape = [jax.ShapeDtypeStruct(
            tuple(d * N_CHIP if a == axis else d for a, d in enumerate(wsh.shape)), wsh.dtype)]

    def _half(self, ref, chip, c, local=False):
        h = self.rows // 2
        if self.axis == 1:
            rows = pl.ds(pl.multiple_of(c * h, 16), h)
            return ref.at[rows, :] if local else ref.at[rows, pl.ds(pl.multiple_of(chip * self.size, LANE), self.size)]
        base = 0 if local else chip * self.size
        return ref.at[pl.ds(pl.multiple_of(base + c * h, 16), h), :]

    def start(self, ins, outs, sem, s0):
        (w_ref,), (o_ref,) = ins, outs
        pos = _mesh_pos()
        chip, c = _chip_of(pos), pos[2]
        pltpu.make_async_copy(w_ref, _window(o_ref, self.axis, chip, self.size), sem.at[s0]).start()
        for j in range(1, N_CHIP):
            _remote(self._half(w_ref, chip, c, local=True), self._half(o_ref, chip, c), sem, s0 + j, s0 + 3 + j,
                    _flip(pos, 2 * j)).start()

    def relay(self, ins, outs, sem, s0):
        (o_ref,) = outs
        pos = _mesh_pos()
        for j in range(1, N_CHIP):
            peer = _flip(pos, 2 * j)
            blk = self._half(o_ref, _chip_of(peer), pos[2])
            _remote(blk, blk, sem, s0 + j, s0 + 3 + j, peer).wait_recv()
            _remote(blk, blk, sem, s0 + 6 + j, s0 + 9 + j, _flip(pos, 1)).start()

    def finish(self, ins, outs, sem, s0):
        (w_ref,), (o_ref,) = ins, outs
        pos = _mesh_pos()
        chip, c = _chip_of(pos), pos[2]
        sibling = _flip(pos, 1)
        for j in range(1, N_CHIP):
            peer = _flip(pos, 2 * j)
            _remote(self._half(w_ref, chip, c, local=True), self._half(o_ref, chip, c), sem, s0 + j, s0 + 3 + j,
                    peer).wait_send()
            blk = self._half(o_ref, _chip_of(peer), c)
            _remote(blk, blk, sem, s0 + 6 + j, s0 + 9 + j, sibling).wait_send()
            got = self._half(o_ref, _chip_of(peer), 1 - c)
            _remote(got, got, sem, s0 + 6 + j, s0 + 9 + j, sibling).wait_recv()
        pltpu.make_async_copy(w_ref, _window(o_ref, self.axis, chip, self.size), sem.at[s0]).wait()


class _Scatter:
    def __init__(self, g, axis, peers, own):
        self.axis, self.size, self.peers, self.own = axis, g.shape[axis] // N_CHIP, tuple(peers), int(own)
        shard = tuple(self.size if a == axis else d for a, d in enumerate(g.shape))
        self.n_sem = self.own + 2 * len(self.peers)
        self.inputs = [g]
        self.out_shape = [jax.ShapeDtypeStruct((self.own + len(self.peers),) + shard, g.dtype)]

    def _copies(self, ins, outs, sem, s0):
        (g_ref,), (o_ref,) = ins, outs
        pos = _mesh_pos()
        cps = []
        if self.own:
            cps.append(pltpu.make_async_copy(_window(g_ref, self.axis, _chip_of(pos), self.size), o_ref.at[0],
                                             sem.at[s0]))
        for n, j in enumerate(self.peers):
            peer = _flip(pos, 2 * j)
            cps.append(_remote(_window(g_ref, self.axis, _chip_of(peer), self.size), o_ref.at[self.own + n], sem,
                               s0 + self.own + 2 * n, s0 + self.own + 2 * n + 1, peer))
        return cps

    def start(self, ins, outs, sem, s0):
        for cp in self._copies(ins, outs, sem, s0):
            cp.start()

    def relay(self, ins, outs, sem, s0):
        pass

    def finish(self, ins, outs, sem, s0):
        for cp in self._copies(ins, outs, sem, s0):
            cp.wait()


class _Side:
    def __init__(self, jobs):
        self.jobs = list(jobs)
        self.inputs = [a for job in self.jobs for a in job.inputs]
        self.out_shape = [s for job in self.jobs for s in job.out_shape]

    def scratch(self):
        return [pltpu.SemaphoreType.DMA((sum(job.n_sem for job in self.jobs),))]

    def _run(self, phase, ins, outs, sem):
        i = o = s = 0
        for job in self.jobs:
            ni, no = len(job.inputs), len(job.out_shape)
            getattr(job, phase)(ins[i:i + ni], outs[o:o + no], sem, s)
            i, o, s = i + ni, o + no, s + job.n_sem

    def begin(self, ins, outs, sem, step, nsteps):
        if nsteps == 1:
            self._run("start", ins, outs, sem)
        else:
            pl.when(step == 0)(lambda: self._run("start", ins, outs, sem))

    def end(self, ins, outs, sem, step, nsteps):
        if nsteps == 1:
            self._run("relay", ins, outs, sem)
            self._run("finish", ins, outs, sem)
        else:
            pl.when(step == nsteps - max(1, nsteps // 8))(lambda: self._run("relay", ins, outs, sem))
            pl.when(step == nsteps - 1)(lambda: self._run("finish", ins, outs, sem))


def _comm_call(side, *, name):
    n_in = len(side.inputs)

    def body(*refs):
        ins, outs, sem = refs[:n_in], refs[n_in:-1], refs[-1]
        side.begin(ins, outs, sem, 0, 1)
        side.end(ins, outs, sem, 0, 1)

    any_spec = pl.BlockSpec(memory_space=pl.ANY)
    return pl.pallas_call(body, name=name, out_shape=list(side.out_shape), in_specs=[any_spec] * n_in,
                          out_specs=[any_spec] * len(side.out_shape), scratch_shapes=side.scratch())(*side.inputs)


def _refs_split(refs, counts):
    out, p = [], 0
    for n in counts:
        out.append(refs[p:p + n])
        p += n
    return out


def _mm(a, b, *, name, ta=False, tb=False, out_dtypes=(F32,), epilogue=None, extras=(),
        tm=1024, tn=1024, tk=512, m_rows=None, m_start=0, side=None):
    m, k = (a.shape[1], a.shape[0]) if ta else a.shape
    n = b.shape[0] if tb else b.shape[1]
    assert k == (b.shape[1] if tb else b.shape[0]), (a.shape, b.shape)
    m = m if m_rows is None else m_rows
    tm = _tile(m, tm, LANE if ta else 16)
    tn = _tile(n, tn, LANE)
    tk = _tile(k, tk, LANE)
    assert m_start % tm == 0
    mo = m_start // tm
    gi, gj, nk = m // tm, n // tn, k // tk
    nsteps = gi * gj * nk
    ne, no = len(extras), len(out_dtypes)
    ns_in, ns_out = (len(side.inputs), len(side.out_shape)) if side else (0, 0)
    dims = (((0 if ta else 1,), (1 if tb else 0,)), ((), ()))

    def body(*refs):
        (a_ref, b_ref), ex, sins, outs, souts, (acc,), sems = _refs_split(
            refs, (2, ne, ns_in, no, ns_out, 1, 1 if side else 0))
        kk = pl.program_id(2)
        step = (pl.program_id(0) * gj + pl.program_id(1)) * nk + kk
        if side:
            side.begin(sins, souts, sems[0], step, nsteps)

        @pl.when(kk == 0)
        def _():
            acc[...] = jnp.zeros_like(acc)

        acc[...] += lax.dot_general(a_ref[...].astype(BF16), b_ref[...].astype(BF16), dims,
                                    preferred_element_type=F32)

        @pl.when(kk == nk - 1)
        def _():
            r = acc[...]
            res = epilogue(r, *[e[...] for e in ex]) if epilogue is not None else (r,)
            for o, val in zip(outs, res):
                o[...] = val.astype(o.dtype)

        if side:
            side.end(sins, souts, sems[0], step, nsteps)

    a_spec = (pl.BlockSpec((tk, tm), lambda i, j, q: (q, i + mo)) if ta
              else pl.BlockSpec((tm, tk), lambda i, j, q: (i + mo, q)))
    b_spec = pl.BlockSpec((tn, tk), lambda i, j, q: (j, q)) if tb else pl.BlockSpec((tk, tn), lambda i, j, q: (q, j))
    o_spec = pl.BlockSpec((tm, tn), lambda i, j, q: (i, j))
    any_spec = pl.BlockSpec(memory_space=pl.ANY)
    res = pl.pallas_call(
        body, name=name,
        grid=(gi, gj, nk),
        in_specs=[a_spec, b_spec] + [o_spec] * ne + [any_spec] * ns_in,
        out_specs=[o_spec] * no + [any_spec] * ns_out,
        out_shape=[jax.ShapeDtypeStruct((m, n), dt) for dt in out_dtypes] + (list(side.out_shape) if side else []),
        scratch_shapes=[pltpu.VMEM((tm, tn), F32)] + (side.scratch() if side else []),
        compiler_params=_params(*(("arbitrary",) * 3 if side else ("parallel", "parallel", "arbitrary"))),
    )(a, b, *extras, *(side.inputs if side else ()))
    return res[0] if len(res) == 1 else res


def _relu2_epilogue(r):
    return r, jnp.square(jnp.maximum(r, 0.0))


def _relu2_bwd_epilogue(r, f1):
    return (r * (2.0 * jnp.maximum(f1.astype(F32), 0.0)),)


def _row_tile(seq, width, nbytes=2 * 1024 * 1024):
    return _tile(seq, max(8, nbytes // (4 * width)), 8)


def _silu(c_all):
    def body(c_ref, o_ref):
        v = c_ref[...]
        o_ref[...] = v * jax.nn.sigmoid(v)

    return pl.pallas_call(body, name="silu", out_shape=jax.ShapeDtypeStruct(c_all.shape, F32))(c_all)


def _norm_mod_fwd(x, g, shift, scale, seq, *, name, resid=None):
    t, w = x.shape
    tr = _row_tile(seq, w, (1 if resid is not None else 2) * 1024 * 1024)
    nb = seq // tr
    has_res = resid is not None

    def body(*refs):
        if has_res:
            x_ref, o_ref, gate_ref, g_ref, sh_ref, sc_ref, x1_ref, h_ref = refs
            xv = x_ref[...] + gate_ref[0] * o_ref[...]
            x1_ref[...] = xv
        else:
            x_ref, g_ref, sh_ref, sc_ref, h_ref = refs
            xv = x_ref[...]
        r = lax.rsqrt(jnp.mean(xv * xv, axis=-1, keepdims=True) + EPS)
        nrm = xv * r * g_ref[...]
        h_ref[...] = (nrm * (1.0 + sc_ref[0]) + sh_ref[0]).astype(BF16)

    row = pl.BlockSpec((tr, w), lambda i: (i, 0))
    vec = pl.BlockSpec((1, w), lambda i: (0, 0))
    per_ex = pl.BlockSpec((1, 1, w), lambda i: (i // nb, 0, 0))
    if has_res:
        o, gate = resid
        ins, in_specs = (x, o, gate, g, shift, scale), [row, row, per_ex, vec, per_ex, per_ex]
        out_shape = [jax.ShapeDtypeStruct((t, w), F32), jax.ShapeDtypeStruct((t, w), BF16)]
        out_specs = [row, row]
    else:
        ins, in_specs = (x, g, shift, scale), [row, vec, per_ex, per_ex]
        out_shape = jax.ShapeDtypeStruct((t, w), BF16)
        out_specs = row
    return pl.pallas_call(body, name=name, grid=(t // tr,), in_specs=in_specs, out_specs=out_specs,
                          out_shape=out_shape, compiler_params=_params("parallel"))(*ins)


def _norm_mod_bwd(x, g, scale, dh, seq, *, name, dres=None, gate_o=None):
    t, w = x.shape
    nex = t // seq
    tr = _row_tile(seq, w, 1024 * 1024)
    nb = seq // tr
    has_res, has_gate = dres is not None, gate_o is not None

    def body(*refs):
        refs = list(refs)
        x_ref, g_ref, sc_ref, dh_ref = refs[:4]
        p = 4
        dres_ref = None
        if has_res:
            dres_ref = refs[p]
            p += 1
        if has_gate:
            o_ref, gate_ref = refs[p:p + 2]
            p += 2
        dx_ref, dsh_ref, dsc_ref, dg_ref = refs[p:p + 4]
        p += 4
        i = pl.program_id(0)

        @pl.when(i % nb == 0)
        def _():
            dsh_ref[...] = jnp.zeros_like(dsh_ref)
            dsc_ref[...] = jnp.zeros_like(dsc_ref)
            if has_gate:
                refs[p + 1][...] = jnp.zeros_like(refs[p + 1])

        @pl.when(i == 0)
        def _():
            dg_ref[...] = jnp.zeros_like(dg_ref)

        xv = x_ref[...]
        gv = g_ref[...]
        dhv = dh_ref[...]
        r = lax.rsqrt(jnp.mean(xv * xv, axis=-1, keepdims=True) + EPS)
        xh = xv * r
        dsh_ref[0] += jnp.sum(dhv, axis=0, keepdims=True)
        dsc_ref[0] += jnp.sum(dhv * (xh * gv), axis=0, keepdims=True)
        dn = dhv * (1.0 + sc_ref[0])
        dg_ref[...] += jnp.sum(dn * xh, axis=0, keepdims=True)
        dxh = dn * gv
        dx = r * (dxh - xh * jnp.mean(dxh * xh, axis=-1, keepdims=True))
        if has_res:
            dx = dx + dres_ref[...]
        dx_ref[...] = dx
        if has_gate:
            do_ref, dgate_ref = refs[p:p + 2]
            do_ref[...] = (dx * gate_ref[0]).astype(BF16)
            dgate_ref[0] += jnp.sum(dx * o_ref[...], axis=0, keepdims=True)

    row = pl.BlockSpec((tr, w), lambda i: (i, 0))
    vec = pl.BlockSpec((1, w), lambda i: (0, 0))
    per_ex = pl.BlockSpec((1, 1, w), lambda i: (i // nb, 0, 0))
    ins, in_specs = [x, g, scale, dh], [row, vec, per_ex, row]
    if has_res:
        ins.append(dres)
        in_specs.append(row)
    if has_gate:
        ins += list(gate_o)
        in_specs += [row, per_ex]
    ex_shape = jax.ShapeDtypeStruct((nex, 1, w), F32)
    out_shape = [jax.ShapeDtypeStruct((t, w), F32), ex_shape, ex_shape, jax.ShapeDtypeStruct((1, w), F32)]
    out_specs = [row, per_ex, per_ex, vec]
    if has_gate:
        out_shape += [jax.ShapeDtypeStruct((t, w), BF16), ex_shape]
        out_specs += [row, per_ex]
    return pl.pallas_call(body, name=name, grid=(t // tr,), in_specs=in_specs, out_specs=out_specs,
                          out_shape=out_shape, compiler_params=_params("arbitrary"))(*ins)


def _gelu_parts(xv):
    cdf = 0.5 * (1.0 + lax.erf(xv * (1.0 / math.sqrt(2.0))))
    return cdf


def _tril_mask():
    r = lax.broadcasted_iota(jnp.int32, (CHUNK, CHUNK), 0)
    c = lax.broadcasted_iota(jnp.int32, (CHUNK, CHUNK), 1)
    return c <= r


def _gmlp_fwd(uv, g_v, w_s, b_col):
    t = uv.shape[0]
    gw = GM_WIDTH // GM_GROUPS
    nck = 2
    tr = nck * CHUNK

    def body(uv_ref, gv_ref, w_ref, b_ref, o_ref):
        mask = _tril_mask()
        for ck in range(nck):
            rows = pl.ds(ck * CHUNK, CHUNK)
            xv = uv_ref[rows, :]
            z = xv * _gelu_parts(xv)
            u, v = z[:, :GM_WIDTH], z[:, GM_WIDTH:]
            r = lax.rsqrt(jnp.mean(v * v, axis=-1, keepdims=True) + EPS)
            vn = (v * r * gv_ref[...]).astype(BF16)
            for gi in range(GM_GROUPS):
                cols = slice(gi * gw, (gi + 1) * gw)
                wg = jnp.where(mask, w_ref[gi], 0.0).astype(BF16)
                mx = jnp.dot(wg, vn[:, cols], preferred_element_type=F32) + b_ref[gi]
                o_ref[rows, cols] = (u[:, cols] * mx).astype(BF16)

    return pl.pallas_call(
        body, name="gmlp_fwd", grid=(t // tr,),
        in_specs=[pl.BlockSpec((tr, 2 * GM_WIDTH), lambda i: (i, 0)),
                  pl.BlockSpec((1, GM_WIDTH), lambda i: (0, 0)),
                  pl.BlockSpec((GM_GROUPS, CHUNK, CHUNK), lambda i: (0, 0, 0)),
                  pl.BlockSpec((GM_GROUPS, CHUNK, 1), lambda i: (0, 0, 0))],
        out_specs=pl.BlockSpec((tr, GM_WIDTH), lambda i: (i, 0)),
        out_shape=jax.ShapeDtypeStruct((t, GM_WIDTH), BF16),
        compiler_params=_params("parallel"))(uv, g_v, w_s, b_col)


def _gmlp_bwd(uv, g_v, w_s, b_col, da):
    t = uv.shape[0]
    gw = GM_WIDTH // GM_GROUPS
    nck = 2
    tr = nck * CHUNK
    inv_sqrt_2pi = 1.0 / math.sqrt(2.0 * math.pi)

    def body(uv_ref, gv_ref, w_ref, b_ref, da_ref, duv_ref, dgv_ref, dw_ref, db_ref, dvn_ref):
        @pl.when(pl.program_id(0) == 0)
        def _():
            dgv_ref[...] = jnp.zeros_like(dgv_ref)
            dw_ref[...] = jnp.zeros_like(dw_ref)
            db_ref[...] = jnp.zeros_like(db_ref)

        mask = _tril_mask()
        for ck in range(nck):
            rows = pl.ds(ck * CHUNK, CHUNK)
            xv = uv_ref[rows, :]
            cdf = _gelu_parts(xv)
            z = xv * cdf
            u, v = z[:, :GM_WIDTH], z[:, GM_WIDTH:]
            r = lax.rsqrt(jnp.mean(v * v, axis=-1, keepdims=True) + EPS)
            vh = v * r
            gv = gv_ref[...]
            vn = (vh * gv).astype(BF16)
            dav = da_ref[rows, :]
            du_parts = []
            for gi in range(GM_GROUPS):
                cols = slice(gi * gw, (gi + 1) * gw)
                wg = jnp.where(mask, w_ref[gi], 0.0).astype(BF16)
                vng = vn[:, cols]
                mx = jnp.dot(wg, vng, preferred_element_type=F32) + b_ref[gi]
                du_parts.append(dav[:, cols] * mx)
                dmx = dav[:, cols] * u[:, cols]
                db_ref[gi] += jnp.sum(dmx, axis=1, keepdims=True)
                dmb = dmx.astype(BF16)
                dwg = lax.dot_general(dmb, vng, (((1,), (1,)), ((), ())), preferred_element_type=F32)
                dw_ref[gi] += jnp.where(mask, dwg, 0.0)
                dvn_ref[:, cols] = lax.dot_general(wg, dmb, (((0,), (0,)), ((), ())),
                                                   preferred_element_type=F32)
            dvn = dvn_ref[...]
            dgv_ref[...] += jnp.sum(dvn * vh, axis=0, keepdims=True)
            dvh = dvn * gv
            dv = r * (dvh - vh * jnp.mean(dvh * vh, axis=-1, keepdims=True))
            dz = jnp.concatenate(du_parts + [dv], axis=1)
            dgelu = cdf + xv * (jnp.exp(-0.5 * xv * xv) * inv_sqrt_2pi)
            duv_ref[rows, :] = (dz * dgelu).astype(BF16)

    return pl.pallas_call(
        body, name="gmlp_bwd", grid=(t // tr,),
        in_specs=[pl.BlockSpec((tr, 2 * GM_WIDTH), lambda i: (i, 0)),
                  pl.BlockSpec((1, GM_WIDTH), lambda i: (0, 0)),
                  pl.BlockSpec((GM_GROUPS, CHUNK, CHUNK), lambda i: (0, 0, 0)),
                  pl.BlockSpec((GM_GROUPS, CHUNK, 1), lambda i: (0, 0, 0)),
                  pl.BlockSpec((tr, GM_WIDTH), lambda i: (i, 0))],
        out_specs=[pl.BlockSpec((tr, 2 * GM_WIDTH), lambda i: (i, 0)),
                   pl.BlockSpec((1, GM_WIDTH), lambda i: (0, 0)),
                   pl.BlockSpec((GM_GROUPS, CHUNK, CHUNK), lambda i: (0, 0, 0)),
                   pl.BlockSpec((GM_GROUPS, CHUNK, 1), lambda i: (0, 0, 0))],
        out_shape=[jax.ShapeDtypeStruct((t, 2 * GM_WIDTH), BF16),
                   jax.ShapeDtypeStruct((1, GM_WIDTH), F32),
                   jax.ShapeDtypeStruct((GM_GROUPS, CHUNK, CHUNK), F32),
                   jax.ShapeDtypeStruct((GM_GROUPS, CHUNK, 1), F32)],
        scratch_shapes=[pltpu.VMEM((CHUNK, GM_WIDTH), F32)],
        compiler_params=_params("arbitrary"))(uv, g_v, w_s, b_col, da)


def _rope_tables(pos, invf):
    nb, s, _ = pos.shape
    ts = _tile(s, 512, 8)
    half = QK_ROPE // 2

    def body(pos_ref, invf_ref, o_ref):
        ang = pos_ref[0] * invf_ref[...]
        cs, sn = jnp.cos(ang), jnp.sin(ang)
        lane = lax.broadcasted_iota(jnp.int32, (1, LANE), 1)
        o_ref[0, :, :LANE] = jnp.where(lane < QK_ROPE, cs, 0.0)
        o_ref[0, :, LANE:2 * LANE] = jnp.where((lane >= half) & (lane < QK_ROPE), sn, 0.0)
        o_ref[0, :, 2 * LANE:] = jnp.where(lane < half, -sn, 0.0)

    return pl.pallas_call(
        body, name="rope_tables", grid=(nb, s // ts),
        in_specs=[pl.BlockSpec((1, ts, 1), lambda b, i: (b, i, 0)), pl.BlockSpec((1, LANE), lambda b, i: (0, 0))],
        out_specs=pl.BlockSpec((1, ts, 3 * LANE), lambda b, i: (b, i, 0)),
        out_shape=jax.ShapeDtypeStruct((nb, s, 3 * LANE), F32),
        compiler_params=_params("parallel", "parallel"))(pos, invf)


def _rope_split(tab_ref):
    return tab_ref[0, :, :LANE], tab_ref[0, :, LANE:2 * LANE], tab_ref[0, :, 2 * LANE:]


def _headnorm_fwd(x3, g_pad, tabs, *, name):
    bh, s, _ = x3.shape
    ts = _tile(s, 512, 8)
    half = QK_ROPE // 2
    width = QK_NOPE + QK_ROPE

    def body(x_ref, g_ref, tab_ref, o_ref):
        xv = x_ref[0]
        r = lax.rsqrt(jnp.sum(xv * xv, axis=-1, keepdims=True) * (1.0 / width) + EPS)
        y = xv * r * g_ref[...]
        c_tab, sp_tab, sm_tab = _rope_split(tab_ref)
        hi = y[:, QK_NOPE:]
        hi = hi * c_tab + pltpu.roll(hi, half, 1) * sp_tab + pltpu.roll(hi, LANE - half, 1) * sm_tab
        o_ref[0, :, :QK_NOPE] = y[:, :QK_NOPE].astype(BF16)
        o_ref[0, :, QK_NOPE:] = hi.astype(BF16)

    return pl.pallas_call(
        body, name=name, grid=(bh, s // ts),
        in_specs=[pl.BlockSpec((1, ts, QK_PAD), lambda b, i: (b, i, 0)),
                  pl.BlockSpec((1, QK_PAD), lambda b, i: (0, 0)),
                  pl.BlockSpec((1, ts, 3 * LANE), lambda b, i: (b // MLA_HEADS, i, 0))],
        out_specs=pl.BlockSpec((1, ts, QK_PAD), lambda b, i: (b, i, 0)),
        out_shape=jax.ShapeDtypeStruct(x3.shape, BF16),
        compiler_params=_params("parallel", "parallel"))(x3, g_pad, tabs)


def _headnorm_bwd(x3, g_pad, tabs, dout, *, name):
    bh, s, _ = x3.shape
    ts = _tile(s, 512, 8)
    half = QK_ROPE // 2
    width = QK_NOPE + QK_ROPE

    def body(x_ref, g_ref, tab_ref, do_ref, dx_ref, dg_ref):
        @pl.when((pl.program_id(0) == 0) & (pl.program_id(1) == 0))
        def _():
            dg_ref[...] = jnp.zeros_like(dg_ref)

        xv = x_ref[0]
        gv = g_ref[...]
        r = lax.rsqrt(jnp.sum(xv * xv, axis=-1, keepdims=True) * (1.0 / width) + EPS)
        xh = xv * r
        c_tab, sp_tab, sm_tab = _rope_split(tab_ref)
        dov = do_ref[0]
        dhi = dov[:, QK_NOPE:]
        dhi = dhi * c_tab + pltpu.roll(dhi * sp_tab, LANE - half, 1) + pltpu.roll(dhi * sm_tab, half, 1)
        dy = jnp.concatenate([dov[:, :QK_NOPE], dhi], axis=1)
        dg_ref[...] += jnp.sum(dy * xh, axis=0, keepdims=True)
        dyh = dy * gv
        dx_ref[0] = r * (dyh - xh * (jnp.sum(dyh * xh, axis=-1, keepdims=True) * (1.0 / width)))

    return pl.pallas_call(
        body, name=name, grid=(bh, s // ts),
        in_specs=[pl.BlockSpec((1, ts, QK_PAD), lambda b, i: (b, i, 0)),
                  pl.BlockSpec((1, QK_PAD), lambda b, i: (0, 0)),
                  pl.BlockSpec((1, ts, 3 * LANE), lambda b, i: (b // MLA_HEADS, i, 0)),
                  pl.BlockSpec((1, ts, QK_PAD), lambda b, i: (b, i, 0))],
        out_specs=[pl.BlockSpec((1, ts, QK_PAD), lambda b, i: (b, i, 0)),
                   pl.BlockSpec((1, QK_PAD), lambda b, i: (0, 0))],
        out_shape=[jax.ShapeDtypeStruct(x3.shape, F32), jax.ShapeDtypeStruct((1, QK_PAD), F32)],
        compiler_params=_params("arbitrary", "arbitrary"))(x3, g_pad, tabs, dout)


def _causal_probs(q, k, q0):
    scale = (QK_NOPE + QK_ROPE) ** -0.5
    sc = lax.dot_general(q, k, (((1,), (1,)), ((), ())), preferred_element_type=F32) * scale
    qi = q0 + lax.broadcasted_iota(jnp.int32, sc.shape, 0)
    ki = lax.broadcasted_iota(jnp.int32, sc.shape, 1)
    sc = jnp.where(ki <= qi, sc, -1e30)
    e = jnp.exp(sc - jnp.max(sc, axis=-1, keepdims=True))
    return e / jnp.sum(e, axis=-1, keepdims=True)


def _attn_fwd(qn, kn, v3, side):
    bh, s, _ = qn.shape
    tq = _tile(s, 256, 8)
    nq = s // tq
    ns_in, ns_out = len(side.inputs), len(side.out_shape)
    any_spec = pl.BlockSpec(memory_space=pl.ANY)

    def body(*refs):
        (q_ref, k_ref, v_ref), sins, (o_ref,), souts, (sem,) = _refs_split(refs, (3, ns_in, 1, ns_out, 1))
        step = pl.program_id(0) * nq + pl.program_id(1)
        side.begin(sins, souts, sem, step, bh * nq)
        p = _causal_probs(q_ref[0], k_ref[0], pl.program_id(1) * tq)
        o_ref[0] = jnp.dot(p.astype(BF16), v_ref[0], preferred_element_type=F32).astype(BF16)
        side.end(sins, souts, sem, step, bh * nq)

    return pl.pallas_call(
        body, name="attn_fwd", grid=(bh, nq),
        in_specs=[pl.BlockSpec((1, tq, QK_PAD), lambda b, i: (b, i, 0)),
                  pl.BlockSpec((1, s, QK_PAD), lambda b, i: (b, 0, 0)),
                  pl.BlockSpec((1, s, V_HEAD), lambda b, i: (b, 0, 0))] + [any_spec] * ns_in,
        out_specs=[pl.BlockSpec((1, tq, V_HEAD), lambda b, i: (b, i, 0))] + [any_spec] * ns_out,
        out_shape=[jax.ShapeDtypeStruct((bh, s, V_HEAD), BF16)] + list(side.out_shape),
        scratch_shapes=side.scratch(),
        compiler_params=_params("arbitrary", "arbitrary"))(qn, kn, v3, *side.inputs)


def _attn_bwd(qn, kn, v3, do3, side):
    bh, s, _ = qn.shape
    tq = _tile(s, 256, 8)
    nq = s // tq
    scale = (QK_NOPE + QK_ROPE) ** -0.5
    ns_in, ns_out = len(side.inputs), len(side.out_shape)
    any_spec = pl.BlockSpec(memory_space=pl.ANY)

    def body(*refs):
        (q_ref, k_ref, v_ref, do_ref), sins, (dq_ref, dk_ref, dv_ref), souts, (sem,) = _refs_split(
            refs, (4, ns_in, 3, ns_out, 1))
        step = pl.program_id(0) * nq + pl.program_id(1)
        side.begin(sins, souts, sem, step, bh * nq)

        @pl.when(pl.program_id(1) == 0)
        def _():
            dk_ref[...] = jnp.zeros_like(dk_ref)
            dv_ref[...] = jnp.zeros_like(dv_ref)

        q, k, v, do = q_ref[0], k_ref[0], v_ref[0], do_ref[0]
        p = _causal_probs(q, k, pl.program_id(1) * tq)
        dv_ref[0] += lax.dot_general(p.astype(BF16), do, (((0,), (0,)), ((), ())), preferred_element_type=F32)
        dp = lax.dot_general(do, v, (((1,), (1,)), ((), ())), preferred_element_type=F32)
        ds = (p * (dp - jnp.sum(p * dp, axis=-1, keepdims=True)) * scale).astype(BF16)
        dq_ref[0] = jnp.dot(ds, k, preferred_element_type=F32)
        dk_ref[0] += lax.dot_general(ds, q, (((0,), (0,)), ((), ())), preferred_element_type=F32)
        side.end(sins, souts, sem, step, bh * nq)

    return pl.pallas_call(
        body, name="attn_bwd", grid=(bh, nq),
        in_specs=[pl.BlockSpec((1, tq, QK_PAD), lambda b, i: (b, i, 0)),
                  pl.BlockSpec((1, s, QK_PAD), lambda b, i: (b, 0, 0)),
                  pl.BlockSpec((1, s, V_HEAD), lambda b, i: (b, 0, 0)),
                  pl.BlockSpec((1, tq, V_HEAD), lambda b, i: (b, i, 0))] + [any_spec] * ns_in,
        out_specs=[pl.BlockSpec((1, tq, QK_PAD), lambda b, i: (b, i, 0)),
                   pl.BlockSpec((1, s, QK_PAD), lambda b, i: (b, 0, 0)),
                   pl.BlockSpec((1, s, V_HEAD), lambda b, i: (b, 0, 0))] + [any_spec] * ns_out,
        out_shape=[jax.ShapeDtypeStruct((bh, s, QK_PAD), F32),
                   jax.ShapeDtypeStruct((bh, s, QK_PAD), F32),
                   jax.ShapeDtypeStruct((bh, s, V_HEAD), F32)] + list(side.out_shape),
        scratch_shapes=side.scratch(),
        compiler_params=_params("arbitrary", "arbitrary"))(qn, kn, v3, do3, *side.inputs)


def _gatemix_fwd(gpa, gpb, ya, yb):
    t, w = ya.shape
    tr = _tile(t, 128, 8)

    def body(ga_ref, gb_ref, ya_ref, yb_ref, o_ref):
        o_ref[...] = (jax.nn.sigmoid(ga_ref[...]) * ya_ref[...]
                      + jax.nn.sigmoid(gb_ref[...]) * yb_ref[...]).astype(BF16)

    row = pl.BlockSpec((tr, w), lambda i: (i, 0))
    return pl.pallas_call(body, name="gatemix_fwd", grid=(t // tr,), in_specs=[row] * 4, out_specs=row,
                          out_shape=jax.ShapeDtypeStruct((t, w), BF16),
                          compiler_params=_params("parallel"))(gpa, gpb, ya, yb)


def _gatemix_bwd(gpa, gpb, ya, yb, dmix):
    t, w = ya.shape
    tr = _tile(t, 128, 8)

    def body(ga_ref, gb_ref, ya_ref, yb_ref, dm_ref, dya_ref, dyb_ref, dga_ref, dgb_ref):
        dm = dm_ref[...]
        sa = jax.nn.sigmoid(ga_ref[...])
        sb = jax.nn.sigmoid(gb_ref[...])
        dya_ref[...] = (dm * sa).astype(BF16)
        dyb_ref[...] = (dm * sb).astype(BF16)
        dga_ref[...] = (dm * ya_ref[...] * sa * (1.0 - sa)).astype(BF16)
        dgb_ref[...] = (dm * yb_ref[...] * sb * (1.0 - sb)).astype(BF16)

    row = pl.BlockSpec((tr, w), lambda i: (i, 0))
    return pl.pallas_call(body, name="gatemix_bwd", grid=(t // tr,), in_specs=[row] * 5, out_specs=[row] * 4,
                          out_shape=[jax.ShapeDtypeStruct((t, w), BF16)] * 4,
                          compiler_params=_params("parallel"))(gpa, gpb, ya, yb, dmix)


def _final(x1, f2, gate, target, seq):
    t, w = x1.shape
    nex = t // seq
    tr = _row_tile(seq, w, 1024 * 1024)
    nb = seq // tr

    def body(x_ref, f_ref, gate_ref, t_ref, loss_ref, dy_ref, dff_ref, dgate_ref):
        i = pl.program_id(0)

        @pl.when(i == 0)
        def _():
            loss_ref[...] = jnp.zeros_like(loss_ref)

        @pl.when(i % nb == 0)
        def _():
            dgate_ref[...] = jnp.zeros_like(dgate_ref)

        fv = f_ref[...]
        gv = gate_ref[0]
        err = x_ref[...] + gv * fv - t_ref[...]
        sq = jnp.sum(err * err, axis=1, keepdims=True)
        loss_ref[...] += jnp.sum(sq, axis=0, keepdims=True) * (0.5 / w)
        dy = err * (1.0 / w)
        dy_ref[...] = dy
        dff_ref[...] = (dy * gv).astype(BF16)
        dgate_ref[0] += jnp.sum(dy * fv, axis=0, keepdims=True)

    row = pl.BlockSpec((tr, w), lambda i: (i, 0))
    per_ex = pl.BlockSpec((1, 1, w), lambda i: (i // nb, 0, 0))
    return pl.pallas_call(
        body, name="loss_head", grid=(t // tr,),
        in_specs=[row, row, per_ex, row],
        out_specs=[pl.BlockSpec((1, LANE), lambda i: (0, 0)), row, row, per_ex],
        out_shape=[jax.ShapeDtypeStruct((1, LANE), F32), jax.ShapeDtypeStruct((t, w), F32),
                   jax.ShapeDtypeStruct((t, w), BF16), jax.ShapeDtypeStruct((nex, 1, w), F32)],
        compiler_params=_params("arbitrary"))(x1, f2, gate, target)


def _sum_slots(parts, *, name):
    _, r, c = parts[0].shape
    tr, tc = _tile(r, 256, 16), _tile(c, 1024, LANE)

    def body(*refs):
        acc = None
        for p_ref in refs[:-1]:
            for j in range(p_ref.shape[0]):
                val = p_ref[j].astype(F32)
                acc = val if acc is None else acc + val
        refs[-1][...] = acc

    return pl.pallas_call(body, name=name, grid=(r // tr, c // tc),
                          in_specs=[pl.BlockSpec((p.shape[0], tr, tc), lambda i, j: (0, i, j)) for p in parts],
                          out_specs=pl.BlockSpec((tr, tc), lambda i, j: (i, j)),
                          out_shape=jax.ShapeDtypeStruct((r, c), F32),
                          compiler_params=_params("parallel", "parallel"))(*parts)


def _adamw(w, m, v, parts, *, name):
    r, c = w.shape
    tc = _tile(c, 1024, LANE) if c % LANE == 0 else c
    tr = _tile(r, max(8, (256 * 1024) // tc // 8 * 8), 8)
    npart = len(parts)
    b1c = 1.0 - ADAM_B1 ** ADAM_STEP
    b2c = 1.0 - ADAM_B2 ** ADAM_STEP

    def body(*refs):
        w_ref, m_ref, v_ref = refs[:3]
        g_ref, d_ref, mo_ref, vo_ref = refs[3 + npart:]
        g = refs[3][...].astype(F32)
        for p_ref in refs[4:3 + npart]:
            g = g + p_ref[...].astype(F32)
        m2 = ADAM_B1 * m_ref[...] + (1.0 - ADAM_B1) * g
        v2 = ADAM_B2 * v_ref[...] + (1.0 - ADAM_B2) * jnp.square(g)
        m_hat = m2 / b1c
        v_hat = v2 / b2c
        g_ref[...] = g
        d_ref[...] = -ADAM_LR * (m_hat / (jnp.sqrt(v_hat) + ADAM_EPS) + ADAM_WD * w_ref[...])
        mo_ref[...] = m2
        vo_ref[...] = v2

    blk = pl.BlockSpec((tr, tc), lambda i, j: (i, j))
    return pl.pallas_call(body, name=name, grid=(r // tr, c // tc),
                          in_specs=[blk] * (3 + npart), out_specs=[blk] * 4,
                          out_shape=[jax.ShapeDtypeStruct((r, c), F32)] * 4,
                          compiler_params=_params("parallel", "parallel"))(w, m, v, *parts)


def _allgather8(x, *, name):
    def body(x_ref, o_ref, ssem, rsem):
        pos = _mesh_pos()
        me = 4 * pos[0] + 2 * pos[1] + pos[2]
        o_ref[me] = x_ref[...]
        cps = []
        for k in range(1, N_DEV):
            cp = pltpu.make_async_remote_copy(src_ref=x_ref, dst_ref=o_ref.at[me], send_sem=ssem.at[k - 1],
                                              recv_sem=rsem.at[k - 1], device_id=_flip(pos, k), device_id_type=MESH)
            cp.start()
            cps.append(cp)
        for cp in cps:
            cp.wait()

    return pl.pallas_call(
        body, name=name,
        out_shape=jax.ShapeDtypeStruct((N_DEV,) + x.shape, x.dtype),
        in_specs=[pl.BlockSpec(memory_space=pltpu.VMEM)],
        out_specs=pl.BlockSpec(memory_space=pltpu.VMEM),
        scratch_shapes=[pltpu.SemaphoreType.DMA((N_DEV - 1,)), pltpu.SemaphoreType.DMA((N_DEV - 1,))],
        compiler_params=pltpu.CompilerParams(vmem_limit_bytes=VMEM_LIMIT),
    )(x)


def _sibling_swap(p, *, name):
    def body(p_ref, o_ref, ssem, rsem):
        cp = pltpu.make_async_remote_copy(src_ref=p_ref, dst_ref=o_ref, send_sem=ssem, recv_sem=rsem,
                                          device_id=_flip(_mesh_pos(), 1), device_id_type=MESH)
        cp.start()
        cp.wait()

    return pl.pallas_call(
        body, name=name,
        out_shape=jax.ShapeDtypeStruct(p.shape, p.dtype),
        in_specs=[pl.BlockSpec(memory_space=pl.ANY)],
        out_specs=pl.BlockSpec(memory_space=pl.ANY),
        scratch_shapes=[pltpu.SemaphoreType.DMA, pltpu.SemaphoreType.DMA],
    )(p)


_SMALL = ("b_ada", "g_norm1", "g_v", "w_s", "b_s", "g_q_lat", "g_kv_lat", "g_qn", "g_kn", "g_norm2")
_BIG = ("w_in", "w_uq", "w_ukv", "w_branch_a", "w_branch_b", "w_out", "w_ff1", "w_ff2")
_SHARD_AXIS = {"w_in": 1, "w_uq": 1, "w_ukv": 1, "w_branch_a": 1, "w_branch_b": 0, "w_out": 0, "w_ff1": 1, "w_ff2": 0}
_WEIGHTS = ("w_ada", "b_ada", "g_norm1", "w_in", "g_v", "w_s", "b_s", "g_q_lat", "g_kv_lat", "w_uq", "w_ukv",
            "g_qn", "g_kn", "w_branch_a", "w_branch_b", "w_out", "g_norm2", "w_ff1", "w_ff2")


def kernel(x, c, positions, w_ada, b_ada, g_norm1, w_in, g_v, w_s, b_s, g_q_lat, g_kv_lat, w_uq, w_ukv, g_qn, g_kn, w_branch_a, w_branch_b, w_out, g_norm2, w_ff1, w_ff2, loss_target, m_w_ada, m_b_ada, m_g_norm1, m_w_in, m_g_v, m_w_s, m_b_s, m_g_q_lat, m_g_kv_lat, m_w_uq, m_w_ukv, m_g_qn, m_g_kn, m_w_branch_a, m_w_branch_b, m_w_out, m_g_norm2, m_w_ff1, m_w_ff2, v_w_ada, v_b_ada, v_g_norm1, v_w_in, v_g_v, v_w_s, v_b_s, v_g_q_lat, v_g_kv_lat, v_w_uq, v_w_ukv, v_g_qn, v_g_kn, v_w_branch_a, v_w_branch_b, v_w_out, v_g_norm2, v_w_ff1, v_w_ff2):
    wts = dict(w_ada=w_ada, b_ada=b_ada, g_norm1=g_norm1, w_in=w_in, g_v=g_v, w_s=w_s, b_s=b_s, g_q_lat=g_q_lat,
               g_kv_lat=g_kv_lat, w_uq=w_uq, w_ukv=w_ukv, g_qn=g_qn, g_kn=g_kn, w_branch_a=w_branch_a,
               w_branch_b=w_branch_b, w_out=w_out, g_norm2=g_norm2, w_ff1=w_ff1, w_ff2=w_ff2)
    mom1 = dict(w_ada=m_w_ada, b_ada=m_b_ada, g_norm1=m_g_norm1, w_in=m_w_in, g_v=m_g_v, w_s=m_w_s, b_s=m_b_s,
                g_q_lat=m_g_q_lat, g_kv_lat=m_g_kv_lat, w_uq=m_w_uq, w_ukv=m_w_ukv, g_qn=m_g_qn, g_kn=m_g_kn,
                w_branch_a=m_w_branch_a, w_branch_b=m_w_branch_b, w_out=m_w_out, g_norm2=m_g_norm2,
                w_ff1=m_w_ff1, w_ff2=m_w_ff2)
    mom2 = dict(w_ada=v_w_ada, b_ada=v_b_ada, g_norm1=v_g_norm1, w_in=v_w_in, g_v=v_g_v, w_s=v_w_s, b_s=v_b_s,
                g_q_lat=v_g_q_lat, g_kv_lat=v_g_kv_lat, w_uq=v_w_uq, w_ukv=v_w_ukv, g_qn=v_g_qn, g_kn=v_g_kn,
                w_branch_a=v_w_branch_a, w_branch_b=v_w_branch_b, w_out=v_w_out, g_norm2=v_g_norm2,
                w_ff1=v_w_ff1, w_ff2=v_w_ff2)

    nb, seq, dm = x.shape
    t = nb * seq
    nh, qkh = MLA_HEADS, QK_NOPE + QK_ROPE
    bh = nb * nh
    off_q = 2 * GM_WIDTH
    off_kv = off_q + Q_LORA
    off_kpe = off_kv + KV_LORA
    off_gate = off_kpe + QK_ROPE
    in_cols = off_gate + 2 * dm
    ins = in_cols // N_CHIP
    insp = _round_up(ins, LANE)
    ada_cols = N_MOD * dm // N_CHIP

    ix, iy, ic = _mesh_pos()
    me = 4 * ix + 2 * iy + ic
    chip = 2 * ix + iy
    xf = x.reshape(t, dm)
    tgt = loss_target.reshape(t, dm)

    c_all = _allgather8(c, name="ag_cond").reshape(N_DEV * nb, dm)
    cond = _silu(c_all)
    b_sh = lax.dynamic_slice(b_ada, (0, chip * ada_cols), (1, ada_cols))
    mod_sh = _mm(cond, w_ada[0], name="ada_fwd", tn=2048) + b_sh
    mod8 = _allgather8(mod_sh, name="ag_mod")
    mod_all = jnp.concatenate([mod8[2 * s] for s in range(N_CHIP)], axis=1)
    mod = lax.dynamic_slice(mod_all, (nb * me, 0), (nb, N_MOD * dm))
    sh1, sc1, ga1, sh2, sc2, ga2 = [mod[:, j * dm:(j + 1) * dm].reshape(nb, 1, dm) for j in range(N_MOD)]

    shards = {n: wts[n][0].astype(BF16) for n in _BIG}
    shards["w_in"] = jnp.pad(shards["w_in"], ((0, 0), (0, insp - ins)))

    def gathers(*names):
        return _Side([_Gather(shards[n], _SHARD_AXIS[n]) for n in names])

    full = {}
    (full["w_in"],) = _comm_call(gathers("w_in"), name="ag_w_in")
    h1 = _norm_mod_fwd(xf, g_norm1, sh1, sc1, seq, name="norm1_fwd")
    early = ("w_branch_a", "w_uq", "w_ukv", "w_branch_b", "w_out")
    proj_p, *got = _mm(h1, full["w_in"], name="in_fwd", side=gathers(*early))
    full.update(zip(early, got))
    proj = proj_p.reshape(t, N_CHIP, insp)[:, :, :ins].reshape(t, in_cols)
    uv, q_lat, kv_lat = proj[:, :off_q], proj[:, off_q:off_kv], proj[:, off_kv:off_kpe]
    k_pe, gpa, gpb = proj[:, off_kpe:off_gate], proj[:, off_gate:off_gate + dm], proj[:, off_gate + dm:]

    b_col = b_s[0].reshape(GM_GROUPS, CHUNK, 1)
    a_out = _gmlp_fwd(uv, g_v, w_s[0], b_col)
    y_a = _mm(a_out, full["w_branch_a"], name="ba_fwd")

    zq = jnp.zeros((nb, 1, Q_LORA), F32)
    zkv = jnp.zeros((nb, 1, KV_LORA), F32)
    ql = _norm_mod_fwd(q_lat, g_q_lat, zq, zq, seq, name="qlat_norm_fwd")
    kvl = _norm_mod_fwd(kv_lat, g_kv_lat, zkv, zkv, seq, name="kvlat_norm_fwd")
    q_raw = _mm(ql, full["w_uq"], name="uq_fwd")
    kv_raw = _mm(kvl, full["w_ukv"], name="ukv_fwd")
    q3 = jnp.pad(q_raw.reshape(nb, seq, nh, qkh).transpose(0, 2, 1, 3),
                 ((0, 0), (0, 0), (0, 0), (0, QK_PAD - qkh))).reshape(bh, seq, QK_PAD)
    kv4 = kv_raw.reshape(nb, seq, nh, QK_NOPE + V_HEAD).transpose(0, 2, 1, 3)
    k3 = jnp.concatenate([kv4[..., :QK_NOPE],
                          jnp.broadcast_to(k_pe.reshape(nb, 1, seq, QK_ROPE), (nb, nh, seq, QK_ROPE)),
                          jnp.zeros((nb, nh, seq, QK_PAD - qkh), F32)], axis=-1).reshape(bh, seq, QK_PAD)
    v3 = kv4[..., QK_NOPE:].astype(BF16).reshape(bh, seq, V_HEAD)
    pos = positions.astype(F32).reshape(nb, seq, 1)
    inv_freq = 1.0 / (ROPE_THETA ** (jnp.arange(0, QK_ROPE, 2, dtype=F32) / QK_ROPE))
    invf = jnp.concatenate([inv_freq, inv_freq, jnp.zeros((LANE - QK_ROPE,), F32)]).reshape(1, LANE)
    gq_pad = jnp.pad(g_qn, ((0, 0), (0, QK_PAD - qkh)))
    gk_pad = jnp.pad(g_kn, ((0, 0), (0, QK_PAD - qkh)))
    tabs = _rope_tables(pos, invf)
    qn = _headnorm_fwd(q3, gq_pad, tabs, name="qnorm_fwd")
    kn = _headnorm_fwd(k3, gk_pad, tabs, name="knorm_fwd")
    o3, full["w_ff1"] = _attn_fwd(qn, kn, v3, gathers("w_ff1"))
    attn = o3.reshape(nb, nh, seq, V_HEAD).transpose(0, 2, 1, 3).reshape(t, nh * V_HEAD)
    y_b = _mm(attn, full["w_branch_b"], name="bb_fwd")

    mixed = _gatemix_fwd(gpa, gpb, y_a, y_b)
    o1 = _mm(mixed, full["w_out"], name="out_fwd")
    x1, h2 = _norm_mod_fwd(xf, g_norm2, sh2, sc2, seq, name="norm2_fwd", resid=(o1, ga1))
    f1, act, full["w_ff2"] = _mm(h2, full["w_ff1"], name="ff1_fwd", out_dtypes=(BF16, BF16),
                                 epilogue=_relu2_epilogue, side=gathers("w_ff2"))
    f2 = _mm(act, full["w_ff2"], name="ff2_fwd")
    loss_acc, dy, dff, dga2 = _final(x1, f2, ga2, tgt, seq)
    loss = lax.psum(loss_acc[0, 0], ("x", "y", "c"))

    def scatter(n, peers, own):
        return _Scatter(gw[n], _SHARD_AXIS[n], peers, own)

    near, diag, everyone = (1, 2), (3,), (1, 2, 3)
    gw, rs = {}, {}
    gw["w_ff2"] = _mm(act, dff, name="ff2_dw", ta=True, out_dtypes=(BF16,))
    df1, ra = _mm(dff, full["w_ff2"], name="ff2_dx", tb=True, out_dtypes=(BF16,), epilogue=_relu2_bwd_epilogue,
                  extras=(f1,), side=_Side([scatter("w_ff2", near, True)]))
    gw["w_ff1"], rb = _mm(h2, df1, name="ff1_dw", ta=True, out_dtypes=(BF16,),
                          side=_Side([scatter("w_ff2", diag, False)]))
    rs["w_ff2"] = [ra, rb]
    dh2, ra = _mm(df1, full["w_ff1"], name="ff1_dx", tb=True, side=_Side([scatter("w_ff1", near, True)]))
    dx1, dsh2, dsc2, dg_norm2, do1, dga1 = _norm_mod_bwd(x1, g_norm2, sc2, dh2, seq, name="norm2_bwd", dres=dy,
                                                         gate_o=(o1, ga1))
    gw["w_out"] = _mm(mixed, do1, name="out_dw", ta=True, out_dtypes=(BF16,))
    dmixed = _mm(do1, full["w_out"], name="out_dx", tb=True)
    dya, dyb, dgpa, dgpb = _gatemix_bwd(gpa, gpb, y_a, y_b, dmixed)

    gw["w_branch_b"] = _mm(attn, dyb, name="bb_dw", ta=True, out_dtypes=(BF16,))
    dattn = _mm(dyb, full["w_branch_b"], name="bb_dx", tb=True, out_dtypes=(BF16,))
    do3 = dattn.reshape(nb, seq, nh, V_HEAD).transpose(0, 2, 1, 3).reshape(bh, seq, V_HEAD)
    dqn, dkn, dv3, rb, r_out, r_bb = _attn_bwd(qn, kn, v3, do3, _Side([
        scatter("w_ff1", diag, False), scatter("w_out", everyone, True), scatter("w_branch_b", everyone, True)]))
    rs["w_ff1"], rs["w_out"], rs["w_branch_b"] = [ra, rb], [r_out], [r_bb]
    dq3, dg_qn = _headnorm_bwd(q3, gq_pad, tabs, dqn, name="qnorm_bwd")
    dk3, dg_kn = _headnorm_bwd(k3, gk_pad, tabs, dkn, name="knorm_bwd")
    dq_raw = dq3.reshape(nb, nh, seq, QK_PAD)[..., :qkh].transpose(0, 2, 1, 3).reshape(t, nh * qkh).astype(BF16)
    dk4 = dk3.reshape(nb, nh, seq, QK_PAD)
    dk_pe = jnp.sum(dk4[..., QK_NOPE:qkh], axis=1).reshape(t, QK_ROPE)
    dkv_raw = jnp.concatenate([dk4[..., :QK_NOPE], dv3.reshape(nb, nh, seq, V_HEAD)], axis=-1)
    dkv_raw = dkv_raw.transpose(0, 2, 1, 3).reshape(t, nh * (QK_NOPE + V_HEAD)).astype(BF16)
    gw["w_uq"] = _mm(ql, dq_raw, name="uq_dw", ta=True, out_dtypes=(BF16,))
    dql = _mm(dq_raw, full["w_uq"], name="uq_dx", tb=True)
    gw["w_ukv"] = _mm(kvl, dkv_raw, name="ukv_dw", ta=True, out_dtypes=(BF16,))
    dkvl = _mm(dkv_raw, full["w_ukv"], name="ukv_dx", tb=True)
    dq_lat, _, _, dg_q_lat = _norm_mod_bwd(q_lat, g_q_lat, zq, dql, seq, name="qlat_norm_bwd")
    dkv_lat, _, _, dg_kv_lat = _norm_mod_bwd(kv_lat, g_kv_lat, zkv, dkvl, seq, name="kvlat_norm_bwd")

    gw["w_branch_a"] = _mm(a_out, dya, name="ba_dw", ta=True, out_dtypes=(BF16,))
    da = _mm(dya, full["w_branch_a"], name="ba_dx", tb=True)
    duv, dg_v, dw_s, db_col = _gmlp_bwd(uv, g_v, w_s[0], b_col, da)

    dproj = jnp.concatenate([duv, dq_lat.astype(BF16), dkv_lat.astype(BF16), dk_pe.astype(BF16), dgpa, dgpb], axis=1)
    dproj_p = jnp.pad(dproj.reshape(t, N_CHIP, ins), ((0, 0), (0, 0), (0, insp - ins))).reshape(t, N_CHIP * insp)
    half = dm // 2
    gw["in0"], r_uq, r_ukv, r_ba = _mm(h1, dproj_p, name="in_dw0", ta=True, out_dtypes=(BF16,), m_rows=half, side=_Side([
        scatter("w_uq", everyone, True), scatter("w_ukv", everyone, True), scatter("w_branch_a", everyone, True)]))
    rs["w_uq"], rs["w_ukv"], rs["w_branch_a"] = [r_uq], [r_ukv], [r_ba]
    gw["in1"], ra0 = _mm(h1, dproj_p, name="in_dw1", ta=True, out_dtypes=(BF16,), m_rows=half, m_start=half,
                         side=_Side([_Scatter(gw["in0"], 1, near, True)]))
    dh1, rb0, ra1 = _mm(dproj_p, full["w_in"], name="in_dx", tb=True,
                        side=_Side([_Scatter(gw["in0"], 1, diag, False), _Scatter(gw["in1"], 1, near, True)]))
    (rb1,) = _comm_call(_Side([_Scatter(gw["in1"], 1, diag, False)]), name="rs_w_in_tail")
    rs["w_in"] = [jnp.concatenate([ra0, ra1], axis=1), jnp.concatenate([rb0, rb1], axis=1)]
    dx, dsh1, dsc1, dg_norm1 = _norm_mod_bwd(xf, g_norm1, sc1, dh1, seq, name="norm1_bwd", dres=dx1)
    grad_x = dx.reshape(nb, seq, dm)

    gmod = jnp.concatenate([dsh1, dsc1, dga1, dsh2, dsc2, dga2], axis=-1).reshape(nb, N_MOD * dm)
    gmod_all = _allgather8(gmod, name="ag_gmod").reshape(N_DEV * nb, N_MOD * dm)
    gmod_sh = lax.dynamic_slice(gmod_all, (0, chip * ada_cols), (N_DEV * nb, ada_cols))
    g_w_ada = _mm(cond, gmod_sh, name="ada_dw", ta=True, tn=2048)

    out_g, out_d, out_m, out_v = {}, {}, {}, {}
    out_g["w_ada"], out_d["w_ada"], out_m["w_ada"], out_v["w_ada"] = _adamw(
        w_ada[0], m_w_ada[0], v_w_ada[0], [g_w_ada], name="adamw_w_ada")
    for n in _BIG:
        part = _sum_slots(rs[n], name="sum4_" + n)
        other = _sibling_swap(part, name="swap_" + n)
        if n == "w_in":
            part, other = part[:, :ins], other[:, :ins]
        out_g[n], out_d[n], out_m[n], out_v[n] = _adamw(wts[n][0], mom1[n][0], mom2[n][0], [part, other],
                                                        name="adamw_" + n)

    small_g = dict(b_ada=jnp.sum(gmod, axis=0), g_norm1=dg_norm1, g_v=dg_v, w_s=dw_s, b_s=db_col,
                   g_q_lat=dg_q_lat, g_kv_lat=dg_kv_lat, g_qn=dg_qn[:, :qkh], g_kn=dg_kn[:, :qkh],
                   g_norm2=dg_norm2)

    def pack(d):
        flat = jnp.concatenate([d[n].reshape(-1) for n in _SMALL])
        return jnp.pad(flat, (0, _round_up(flat.shape[0], 8 * LANE) - flat.shape[0])).reshape(-1, LANE)

    g8 = _allgather8(pack(small_g), name="ag_small_grads")
    sg, sd, sm, sv = _adamw(pack({n: wts[n] for n in _SMALL}), pack({n: mom1[n] for n in _SMALL}),
                            pack({n: mom2[n] for n in _SMALL}), [g8[d] for d in range(N_DEV)], name="adamw_small")
    off = 0
    for n in _SMALL:
        size = wts[n].size
        for dst, src in ((out_g, sg), (out_d, sd), (out_m, sm), (out_v, sv)):
            dst[n] = src.reshape(-1)[off:off + size].reshape(wts[n].shape)
        off += size

    def shaped(d, n):
        return d[n].reshape(wts[n].shape)

    return (loss, grad_x, *[shaped(out_g, n) for n in _WEIGHTS], *[shaped(out_d, n) for n in _WEIGHTS],
            *[shaped(out_m, n) for n in _WEIGHTS], *[shaped(out_v, n) for n in _WEIGHTS])
```

```python
import functools
import math

import jax
import jax.numpy as jnp
from jax import lax
from jax.experimental import pallas as pl
from jax.experimental.pallas import tpu as pltpu

F32 = jnp.float32
BF16 = jnp.bfloat16

GM_WIDTH = 2048
GM_GROUPS = 8
CHUNK = 128
MLA_HEADS = 32
QK_NOPE = 128
QK_ROPE = 64
V_HEAD = 128
Q_LORA = 1024
KV_LORA = 512
ROPE_THETA = 10000.0
N_MOD = 6
EPS = 1e-6
ADAM_LR = 0.001
ADAM_B1 = 0.9
ADAM_B2 = 0.999
ADAM_EPS = 1e-08
ADAM_WD = 0.01
ADAM_STEP = 10

N_CHIP = 4
N_DEV = 8
LANE = 128
QK_PAD = 256
VMEM_LIMIT = 56 * 1024 * 1024
MESH = pl.DeviceIdType.MESH


def _round_up(n, m):
    return (n + m - 1) // m * m


def _tile(n, target, align):
    t = min(target, n) // align * align
    while t >= align:
        if n % t == 0:
            return t
        t -= align
    return n


def _params(*sem):
    return pltpu.CompilerParams(dimension_semantics=sem, vmem_limit_bytes=VMEM_LIMIT)


def _mesh_pos():
    return lax.axis_index("x"), lax.axis_index("y"), lax.axis_index("c")


def _flip(pos, k):
    ix, iy, ic = pos
    return (1 - ix if k & 4 else ix, 1 - iy if k & 2 else iy, 1 - ic if k & 1 else ic)


def _chip_of(pos):
    return 2 * pos[0] + pos[1]


def _window(ref, axis, chip, size):
    start = pl.multiple_of(chip * size, LANE if axis == 1 else 16)
    if axis == 1:
        return ref.at[:, pl.ds(start, size)]
    return ref.at[pl.ds(start, size), :]


def _remote(src, dst, sem, send, recv, peer):
    return pltpu.make_async_remote_copy(src_ref=src, dst_ref=dst, send_sem=sem.at[send], recv_sem=sem.at[recv],
                                        device_id=peer, device_id_type=MESH)


class _Gather:
    n_sem = 13

    def __init__(self, wsh, axis):
        self.axis, self.size, self.rows = axis, wsh.shape[axis], wsh.shape[0]
        self.inputs = [wsh]
        self.out_shape = [jax.ShapeDtypeStruct(
            tuple(d * N_CHIP if a == axis else d for a, d in enumerate(wsh.shape)), wsh.dtype)]

    def _half(self, ref, chip, c, local=False):
        h = self.rows // 2
        if self.axis == 1:
            rows = pl.ds(pl.multiple_of(c * h, 16), h)
            return ref.at[rows, :] if local else ref.at[rows, pl.ds(pl.multiple_of(chip * self.size, LANE), self.size)]
        base = 0 if local else chip * self.size
        return ref.at[pl.ds(pl.multiple_of(base + c * h, 16), h), :]

    def start(self, ins, outs, sem, s0):
        (w_ref,), (o_ref,) = ins, outs
        pos = _mesh_pos()
        chip, c = _chip_of(pos), pos[2]
        pltpu.make_async_copy(w_ref, _window(o_ref, self.axis, chip, self.size), sem.at[s0]).start()
        for j in range(1, N_CHIP):
            _remote(self._half(w_ref, chip, c, local=True), self._half(o_ref, chip, c), sem, s0 + j, s0 + 3 + j,
                    _flip(pos, 2 * j)).start()

    def relay(self, ins, outs, sem, s0):
        (o_ref,) = outs
        pos = _mesh_pos()
        for j in range(1, N_CHIP):
            peer = _flip(pos, 2 * j)
            blk = self._half(o_ref, _chip_of(peer), pos[2])
            _remote(blk, blk, sem, s0 + j, s0 + 3 + j, peer).wait_recv()
            _remote(blk, blk, sem, s0 + 6 + j, s0 + 9 + j, _flip(pos, 1)).start()

    def finish(self, ins, outs, sem, s0):
        (w_ref,), (o_ref,) = ins, outs
        pos = _mesh_pos()
        chip, c = _chip_of(pos), pos[2]
        sibling = _flip(pos, 1)
        for j in range(1, N_CHIP):
            peer = _flip(pos, 2 * j)
            _remote(self._half(w_ref, chip, c, local=True), self._half(o_ref, chip, c), sem, s0 + j, s0 + 3 + j,
                    peer).wait_send()
            blk = self._half(o_ref, _chip_of(peer), c)
            _remote(blk, blk, sem, s0 + 6 + j, s0 + 9 + j, sibling).wait_send()
            got = self._half(o_ref, _chip_of(peer), 1 - c)
            _remote(got, got, sem, s0 + 6 + j, s0 + 9 + j, sibling).wait_recv()
        pltpu.make_async_copy(w_ref, _window(o_ref, self.axis, chip, self.size), sem.at[s0]).wait()


class _Scatter:
    def __init__(self, g, axis, peers, own):
        self.axis, self.size, self.peers, self.own = axis, g.shape[axis] // N_CHIP, tuple(peers), int(own)
        shard = tuple(self.size if a == axis else d for a, d in enumerate(g.shape))
        self.n_sem = self.own + 2 * len(self.peers)
        self.inputs = [g]
        self.out_shape = [jax.ShapeDtypeStruct((self.own + len(self.peers),) + shard, g.dtype)]

    def _copies(self, ins, outs, sem, s0):
        (g_ref,), (o_ref,) = ins, outs
        pos = _mesh_pos()
        cps = []
        if self.own:
            cps.append(pltpu.make_async_copy(_window(g_ref, self.axis, _chip_of(pos), self.size), o_ref.at[0],
                                             sem.at[s0]))
        for n, j in enumerate(self.peers):
            peer = _flip(pos, 2 * j)
            cps.append(_remote(_window(g_ref, self.axis, _chip_of(peer), self.size), o_ref.at[self.own + n], sem,
                               s0 + self.own + 2 * n, s0 + self.own + 2 * n + 1, peer))
        return cps

    def start(self, ins, outs, sem, s0):
        for cp in self._copies(ins, outs, sem, s0):
            cp.start()

    def relay(self, ins, outs, sem, s0):
        pass

    def finish(self, ins, outs, sem, s0):
        for cp in self._copies(ins, outs, sem, s0):
            cp.wait()


class _Side:
    def __init__(self, jobs):
        self.jobs = list(jobs)
        self.inputs = [a for job in self.jobs for a in job.inputs]
        self.out_shape = [s for job in self.jobs for s in job.out_shape]

    def scratch(self):
        return [pltpu.SemaphoreType.DMA((sum(job.n_sem for job in self.jobs),))]

    def _run(self, phase, ins, outs, sem):
        i = o = s = 0
        for job in self.jobs:
            ni, no = len(job.inputs), len(job.out_shape)
            getattr(job, phase)(ins[i:i + ni], outs[o:o + no], sem, s)
            i, o, s = i + ni, o + no, s + job.n_sem

    def begin(self, ins, outs, sem, step, nsteps):
        if nsteps == 1:
            self._run("start", ins, outs, sem)
        else:
            pl.when(step == 0)(lambda: self._run("start", ins, outs, sem))

    def end(self, ins, outs, sem, step, nsteps):
        if nsteps == 1:
            self._run("relay", ins, outs, sem)
            self._run("finish", ins, outs, sem)
        else:
            pl.when(step == nsteps - max(1, nsteps // 8))(lambda: self._run("relay", ins, outs, sem))
            pl.when(step == nsteps - 1)(lambda: self._run("finish", ins, outs, sem))


def _comm_call(side, *, name):
    n_in = len(side.inputs)

    def body(*refs):
        ins, outs, sem = refs[:n_in], refs[n_in:-1], refs[-1]
        side.begin(ins, outs, sem, 0, 1)
        side.end(ins, outs, sem, 0, 1)

    any_spec = pl.BlockSpec(memory_space=pl.ANY)
    return pl.pallas_call(body, name=name, out_shape=list(side.out_shape), in_specs=[any_spec] * n_in,
                          out_specs=[any_spec] * len(side.out_shape), scratch_shapes=side.scratch())(*side.inputs)


def _refs_split(refs, counts):
    out, p = [], 0
    for n in counts:
        out.append(refs[p:p + n])
        p += n
    return out


def _mm(a, b, *, name, ta=False, tb=False, out_dtypes=(F32,), epilogue=None, extras=(),
        tm=1024, tn=1024, tk=1024, m_rows=None, m_start=0, side=None):
    m, k = (a.shape[1], a.shape[0]) if ta else a.shape
    n = b.shape[0] if tb else b.shape[1]
    assert k == (b.shape[1] if tb else b.shape[0]), (a.shape, b.shape)
    m = m if m_rows is None else m_rows
    tm = _tile(m, tm, LANE if ta else 16)
    tn = _tile(n, tn, LANE)
    tk = _tile(k, tk, LANE)
    assert m_start % tm == 0
    mo = m_start // tm
    gi, gj, nk = m // tm, n // tn, k // tk
    nsteps = gi * gj * nk
    ne, no = len(extras), len(out_dtypes)
    ns_in, ns_out = (len(side.inputs), len(side.out_shape)) if side else (0, 0)
    dims = (((0 if ta else 1,), (1 if tb else 0,)), ((), ()))

    def body(*refs):
        (a_ref, b_ref), ex, sins, outs, souts, (acc,), sems = _refs_split(
            refs, (2, ne, ns_in, no, ns_out, 1, 1 if side else 0))
        kk = pl.program_id(2)
        step = (pl.program_id(0) * gj + pl.program_id(1)) * nk + kk
        if side:
            side.begin(sins, souts, sems[0], step, nsteps)

        def prod():
            return lax.dot_general(a_ref[...].astype(BF16), b_ref[...].astype(BF16), dims,
                                   preferred_element_type=F32)

        def emit(r):
            res = epilogue(r, *[e[...] for e in ex]) if epilogue is not None else (r,)
            for o, val in zip(outs, res):
                o[...] = val.astype(o.dtype)

        if nk == 1:
            emit(prod())
        else:
            @pl.when(kk == 0)
            def _():
                acc[...] = prod()

            @pl.when((kk > 0) & (kk < nk - 1))
            def _():
                acc[...] += prod()

            @pl.when(kk == nk - 1)
            def _():
                emit(acc[...] + prod())

        if side:
            side.end(sins, souts, sems[0], step, nsteps)

    a_spec = (pl.BlockSpec((tk, tm), lambda i, j, q: (q, i + mo)) if ta
              else pl.BlockSpec((tm, tk), lambda i, j, q: (i + mo, q)))
    b_spec = pl.BlockSpec((tn, tk), lambda i, j, q: (j, q)) if tb else pl.BlockSpec((tk, tn), lambda i, j, q: (q, j))
    o_spec = pl.BlockSpec((tm, tn), lambda i, j, q: (i, j))
    any_spec = pl.BlockSpec(memory_space=pl.ANY)
    res = pl.pallas_call(
        body, name=name,
        grid=(gi, gj, nk),
        in_specs=[a_spec, b_spec] + [o_spec] * ne + [any_spec] * ns_in,
        out_specs=[o_spec] * no + [any_spec] * ns_out,
        out_shape=[jax.ShapeDtypeStruct((m, n), dt) for dt in out_dtypes] + (list(side.out_shape) if side else []),
        scratch_shapes=[pltpu.VMEM((tm, tn), F32)] + (side.scratch() if side else []),
        compiler_params=_params(*(("arbitrary",) * 3 if side else ("parallel", "parallel", "arbitrary"))),
    )(a, b, *extras, *(side.inputs if side else ()))
    return res[0] if len(res) == 1 else res


def _relu2_epilogue(r):
    return r, jnp.square(jnp.maximum(r, 0.0))


def _relu2_bwd_epilogue(r, f1):
    return (r * (2.0 * jnp.maximum(f1.astype(F32), 0.0)),)


def _row_tile(seq, width, nbytes=2 * 1024 * 1024):
    return _tile(seq, max(8, nbytes // (4 * width)), 8)


def _silu(c_all):
    def body(c_ref, o_ref):
        v = c_ref[...]
        o_ref[...] = v * jax.nn.sigmoid(v)

    return pl.pallas_call(body, name="silu", out_shape=jax.ShapeDtypeStruct(c_all.shape, F32))(c_all)


def _norm_mod_fwd(x, g, shift, scale, seq, *, name, resid=None):
    t, w = x.shape
    tr = _row_tile(seq, w, (1 if resid is not None else 2) * 1024 * 1024)
    nb = seq // tr
    has_res = resid is not None

    def body(*refs):
        if has_res:
            x_ref, o_ref, gate_ref, g_ref, sh_ref, sc_ref, x1_ref, h_ref = refs
            xv = x_ref[...] + gate_ref[0] * o_ref[...]
            x1_ref[...] = xv
        else:
            x_ref, g_ref, sh_ref, sc_ref, h_ref = refs
            xv = x_ref[...]
        r = lax.rsqrt(jnp.mean(xv * xv, axis=-1, keepdims=True) + EPS)
        nrm = xv * r * g_ref[...]
        h_ref[...] = (nrm * (1.0 + sc_ref[0]) + sh_ref[0]).astype(BF16)

    row = pl.BlockSpec((tr, w), lambda i: (i, 0))
    vec = pl.BlockSpec((1, w), lambda i: (0, 0))
    per_ex = pl.BlockSpec((1, 1, w), lambda i: (i // nb, 0, 0))
    if has_res:
        o, gate = resid
        ins, in_specs = (x, o, gate, g, shift, scale), [row, row, per_ex, vec, per_ex, per_ex]
        out_shape = [jax.ShapeDtypeStruct((t, w), F32), jax.ShapeDtypeStruct((t, w), BF16)]
        out_specs = [row, row]
    else:
        ins, in_specs = (x, g, shift, scale), [row, vec, per_ex, per_ex]
        out_shape = jax.ShapeDtypeStruct((t, w), BF16)
        out_specs = row
    return pl.pallas_call(body, name=name, grid=(t // tr,), in_specs=in_specs, out_specs=out_specs,
                          out_shape=out_shape, compiler_params=_params("parallel"))(*ins)


def _norm_mod_bwd(x, g, scale, dh, seq, *, name, dres=None, gate_o=None):
    t, w = x.shape
    nex = t // seq
    tr = _row_tile(seq, w, 1024 * 1024)
    nb = seq // tr
    has_res, has_gate = dres is not None, gate_o is not None

    def body(*refs):
        refs = list(refs)
        x_ref, g_ref, sc_ref, dh_ref = refs[:4]
        p = 4
        dres_ref = None
        if has_res:
            dres_ref = refs[p]
            p += 1
        if has_gate:
            o_ref, gate_ref = refs[p:p + 2]
            p += 2
        dx_ref, dsh_ref, dsc_ref, dg_ref = refs[p:p + 4]
        p += 4
        i = pl.program_id(0)

        @pl.when(i % nb == 0)
        def _():
            dsh_ref[...] = jnp.zeros_like(dsh_ref)
            dsc_ref[...] = jnp.zeros_like(dsc_ref)
            if has_gate:
                refs[p + 1][...] = jnp.zeros_like(refs[p + 1])

        @pl.when(i == 0)
        def _():
            dg_ref[...] = jnp.zeros_like(dg_ref)

        xv = x_ref[...]
        gv = g_ref[...]
        dhv = dh_ref[...]
        r = lax.rsqrt(jnp.mean(xv * xv, axis=-1, keepdims=True) + EPS)
        xh = xv * r
        dsh_ref[0] += jnp.sum(dhv, axis=0, keepdims=True)
        dsc_ref[0] += jnp.sum(dhv * (xh * gv), axis=0, keepdims=True)
        dn = dhv * (1.0 + sc_ref[0])
        dg_ref[...] += jnp.sum(dn * xh, axis=0, keepdims=True)
        dxh = dn * gv
        dx = r * (dxh - xh * jnp.mean(dxh * xh, axis=-1, keepdims=True))
        if has_res:
            dx = dx + dres_ref[...]
        dx_ref[...] = dx
        if has_gate:
            do_ref, dgate_ref = refs[p:p + 2]
            do_ref[...] = (dx * gate_ref[0]).astype(BF16)
            dgate_ref[0] += jnp.sum(dx * o_ref[...], axis=0, keepdims=True)

    row = pl.BlockSpec((tr, w), lambda i: (i, 0))
    vec = pl.BlockSpec((1, w), lambda i: (0, 0))
    per_ex = pl.BlockSpec((1, 1, w), lambda i: (i // nb, 0, 0))
    ins, in_specs = [x, g, scale, dh], [row, vec, per_ex, row]
    if has_res:
        ins.append(dres)
        in_specs.append(row)
    if has_gate:
        ins += list(gate_o)
        in_specs += [row, per_ex]
    ex_shape = jax.ShapeDtypeStruct((nex, 1, w), F32)
    out_shape = [jax.ShapeDtypeStruct((t, w), F32), ex_shape, ex_shape, jax.ShapeDtypeStruct((1, w), F32)]
    out_specs = [row, per_ex, per_ex, vec]
    if has_gate:
        out_shape += [jax.ShapeDtypeStruct((t, w), BF16), ex_shape]
        out_specs += [row, per_ex]
    return pl.pallas_call(body, name=name, grid=(t // tr,), in_specs=in_specs, out_specs=out_specs,
                          out_shape=out_shape, compiler_params=_params("arbitrary"))(*ins)


def _gelu_parts(xv):
    cdf = 0.5 * (1.0 + lax.erf(xv * (1.0 / math.sqrt(2.0))))
    return cdf


def _tril_mask():
    r = lax.broadcasted_iota(jnp.int32, (CHUNK, CHUNK), 0)
    c = lax.broadcasted_iota(jnp.int32, (CHUNK, CHUNK), 1)
    return c <= r


def _gmlp_fwd(uv, g_v, w_s, b_col):
    t = uv.shape[0]
    gw = GM_WIDTH // GM_GROUPS
    nck = 2
    tr = nck * CHUNK

    def body(uv_ref, gv_ref, w_ref, b_ref, o_ref):
        mask = _tril_mask()
        for ck in range(nck):
            rows = pl.ds(ck * CHUNK, CHUNK)
            xv = uv_ref[rows, :]
            z = xv * _gelu_parts(xv)
            u, v = z[:, :GM_WIDTH], z[:, GM_WIDTH:]
            r = lax.rsqrt(jnp.mean(v * v, axis=-1, keepdims=True) + EPS)
            vn = (v * r * gv_ref[...]).astype(BF16)
            for gi in range(GM_GROUPS):
                cols = slice(gi * gw, (gi + 1) * gw)
                wg = jnp.where(mask, w_ref[gi], 0.0).astype(BF16)
                mx = jnp.dot(wg, vn[:, cols], preferred_element_type=F32) + b_ref[gi]
                o_ref[rows, cols] = (u[:, cols] * mx).astype(BF16)

    return pl.pallas_call(
        body, name="gmlp_fwd", grid=(t // tr,),
        in_specs=[pl.BlockSpec((tr, 2 * GM_WIDTH), lambda i: (i, 0)),
                  pl.BlockSpec((1, GM_WIDTH), lambda i: (0, 0)),
                  pl.BlockSpec((GM_GROUPS, CHUNK, CHUNK), lambda i: (0, 0, 0)),
                  pl.BlockSpec((GM_GROUPS, CHUNK, 1), lambda i: (0, 0, 0))],
        out_specs=pl.BlockSpec((tr, GM_WIDTH), lambda i: (i, 0)),
        out_shape=jax.ShapeDtypeStruct((t, GM_WIDTH), BF16),
        compiler_params=_params("parallel"))(uv, g_v, w_s, b_col)


def _gmlp_bwd(uv, g_v, w_s, b_col, da):
    t = uv.shape[0]
    gw = GM_WIDTH // GM_GROUPS
    nck = 2
    tr = nck * CHUNK
    inv_sqrt_2pi = 1.0 / math.sqrt(2.0 * math.pi)

    def body(uv_ref, gv_ref, w_ref, b_ref, da_ref, duv_ref, dgv_ref, dw_ref, db_ref, dvn_ref):
        @pl.when(pl.program_id(0) == 0)
        def _():
            dgv_ref[...] = jnp.zeros_like(dgv_ref)
            dw_ref[...] = jnp.zeros_like(dw_ref)
            db_ref[...] = jnp.zeros_like(db_ref)

        mask = _tril_mask()
        for ck in range(nck):
            rows = pl.ds(ck * CHUNK, CHUNK)
            xv = uv_ref[rows, :]
            cdf = _gelu_parts(xv)
            z = xv * cdf
            u, v = z[:, :GM_WIDTH], z[:, GM_WIDTH:]
            r = lax.rsqrt(jnp.mean(v * v, axis=-1, keepdims=True) + EPS)
            vh = v * r
            gv = gv_ref[...]
            vn = (vh * gv).astype(BF16)
            dav = da_ref[rows, :]
            du_parts = []
            for gi in range(GM_GROUPS):
                cols = slice(gi * gw, (gi + 1) * gw)
                wg = jnp.where(mask, w_ref[gi], 0.0).astype(BF16)
                vng = vn[:, cols]
                mx = jnp.dot(wg, vng, preferred_element_type=F32) + b_ref[gi]
                du_parts.append(dav[:, cols] * mx)
                dmx = dav[:, cols] * u[:, cols]
                db_ref[gi] += jnp.sum(dmx, axis=1, keepdims=True)
                dmb = dmx.astype(BF16)
                dwg = lax.dot_general(dmb, vng, (((1,), (1,)), ((), ())), preferred_element_type=F32)
                dw_ref[gi] += jnp.where(mask, dwg, 0.0)
                dvn_ref[:, cols] = lax.dot_general(wg, dmb, (((0,), (0,)), ((), ())),
                                                   preferred_element_type=F32)
            dvn = dvn_ref[...]
            dgv_ref[...] += jnp.sum(dvn * vh, axis=0, keepdims=True)
            dvh = dvn * gv
            dv = r * (dvh - vh * jnp.mean(dvh * vh, axis=-1, keepdims=True))
            dz = jnp.concatenate(du_parts + [dv], axis=1)
            dgelu = cdf + xv * (jnp.exp(-0.5 * xv * xv) * inv_sqrt_2pi)
            duv_ref[rows, :] = (dz * dgelu).astype(BF16)

    return pl.pallas_call(
        body, name="gmlp_bwd", grid=(t // tr,),
        in_specs=[pl.BlockSpec((tr, 2 * GM_WIDTH), lambda i: (i, 0)),
                  pl.BlockSpec((1, GM_WIDTH), lambda i: (0, 0)),
                  pl.BlockSpec((GM_GROUPS, CHUNK, CHUNK), lambda i: (0, 0, 0)),
                  pl.BlockSpec((GM_GROUPS, CHUNK, 1), lambda i: (0, 0, 0)),
                  pl.BlockSpec((tr, GM_WIDTH), lambda i: (i, 0))],
        out_specs=[pl.BlockSpec((tr, 2 * GM_WIDTH), lambda i: (i, 0)),
                   pl.BlockSpec((1, GM_WIDTH), lambda i: (0, 0)),
                   pl.BlockSpec((GM_GROUPS, CHUNK, CHUNK), lambda i: (0, 0, 0)),
                   pl.BlockSpec((GM_GROUPS, CHUNK, 1), lambda i: (0, 0, 0))],
        out_shape=[jax.ShapeDtypeStruct((t, 2 * GM_WIDTH), BF16),
                   jax.ShapeDtypeStruct((1, GM_WIDTH), F32),
                   jax.ShapeDtypeStruct((GM_GROUPS, CHUNK, CHUNK), F32),
                   jax.ShapeDtypeStruct((GM_GROUPS, CHUNK, 1), F32)],
        scratch_shapes=[pltpu.VMEM((CHUNK, GM_WIDTH), F32)],
        compiler_params=_params("arbitrary"))(uv, g_v, w_s, b_col, da)


def _rope_tables(pos, invf):
    nb, s, _ = pos.shape
    ts = _tile(s, 512, 8)
    half = QK_ROPE // 2

    def body(pos_ref, invf_ref, o_ref):
        ang = pos_ref[0] * invf_ref[...]
        cs, sn = jnp.cos(ang), jnp.sin(ang)
        lane = lax.broadcasted_iota(jnp.int32, (1, LANE), 1)
        o_ref[0, :, :LANE] = jnp.where(lane < QK_ROPE, cs, 0.0)
        o_ref[0, :, LANE:2 * LANE] = jnp.where((lane >= half) & (lane < QK_ROPE), sn, 0.0)
        o_ref[0, :, 2 * LANE:] = jnp.where(lane < half, -sn, 0.0)

    return pl.pallas_call(
        body, name="rope_tables", grid=(nb, s // ts),
        in_specs=[pl.BlockSpec((1, ts, 1), lambda b, i: (b, i, 0)), pl.BlockSpec((1, LANE), lambda b, i: (0, 0))],
        out_specs=pl.BlockSpec((1, ts, 3 * LANE), lambda b, i: (b, i, 0)),
        out_shape=jax.ShapeDtypeStruct((nb, s, 3 * LANE), F32),
        compiler_params=_params("parallel", "parallel"))(pos, invf)


def _head_rstd(lo, hi):
    ss = jnp.sum(lo * lo, axis=-1, keepdims=True) + jnp.sum(hi * hi, axis=-1, keepdims=True)
    return lax.rsqrt(ss * (1.0 / (QK_NOPE + QK_ROPE)) + EPS)


def _head_specs(ts, kpe):
    if kpe is None:
        return [pl.BlockSpec((ts, QK_PAD), lambda r, h: (r, h))]
    return [pl.BlockSpec((ts, QK_NOPE), lambda r, h: (r, 2 * h)), pl.BlockSpec((ts, LANE), lambda r, h: (r, 0))]


def _head_tiles(x_refs):
    if len(x_refs) == 1:
        return x_refs[0][:, :QK_NOPE], x_refs[0][:, QK_NOPE:]
    return x_refs[0][...], x_refs[1][...]


def _qk_norm_fwd(x, kpe, g_pad, tabs, *, name):
    t = x.shape[0]
    ts = _tile(t, 1024, 8)
    half = QK_ROPE // 2
    nx = 1 if kpe is None else 2

    def body(*refs):
        x_refs, (g_ref, tab_ref, o_ref) = refs[:nx], refs[nx:]
        lo, hi = _head_tiles(x_refs)
        r = _head_rstd(lo, hi)
        hi = hi * r * g_ref[:, QK_NOPE:]
        hi = (hi * tab_ref[:, :LANE] + pltpu.roll(hi, half, 1) * tab_ref[:, LANE:2 * LANE]
              + pltpu.roll(hi, LANE - half, 1) * tab_ref[:, 2 * LANE:])
        o_ref[:, :QK_NOPE] = (lo * r * g_ref[:, :QK_NOPE]).astype(BF16)
        o_ref[:, QK_NOPE:] = hi.astype(BF16)

    return pl.pallas_call(
        body, name=name, grid=(t // ts, MLA_HEADS),
        in_specs=_head_specs(ts, kpe) + [pl.BlockSpec((1, QK_PAD), lambda r, h: (0, 0)),
                                         pl.BlockSpec((ts, 3 * LANE), lambda r, h: (r, 0))],
        out_specs=pl.BlockSpec((ts, QK_PAD), lambda r, h: (r, h)),
        out_shape=jax.ShapeDtypeStruct((t, MLA_HEADS * QK_PAD), BF16),
        compiler_params=_params("parallel", "parallel"))(*((x,) if kpe is None else (x, kpe)), g_pad, tabs)


def _qk_norm_bwd(x, kpe, g_pad, tabs, dout, dv, *, name):
    t = x.shape[0]
    ts = _tile(t, 1024, 8)
    half = QK_ROPE // 2
    is_k = kpe is not None
    nx = 2 if is_k else 1
    inv_width = 1.0 / (QK_NOPE + QK_ROPE)

    def body(*refs):
        x_refs, (g_ref, tab_ref, do_ref), rest = refs[:nx], refs[nx:nx + 3], refs[nx + 3:]
        if is_k:
            dv_ref, dx_ref, dkpe_ref, dg_ref = rest
        else:
            dx_ref, dg_ref = rest

        @pl.when((pl.program_id(0) == 0) & (pl.program_id(1) == 0))
        def _():
            dg_ref[...] = jnp.zeros_like(dg_ref)

        lo, hi = _head_tiles(x_refs)
        r = _head_rstd(lo, hi)
        lo, hi = lo * r, hi * r
        dlo, dhi = do_ref[:, :QK_NOPE], do_ref[:, QK_NOPE:]
        dhi = (dhi * tab_ref[:, :LANE] + pltpu.roll(dhi * tab_ref[:, LANE:2 * LANE], LANE - half, 1)
               + pltpu.roll(dhi * tab_ref[:, 2 * LANE:], half, 1))
        dg_ref[:, :QK_NOPE] += jnp.sum(dlo * lo, axis=0, keepdims=True)
        dg_ref[:, QK_NOPE:] += jnp.sum(dhi * hi, axis=0, keepdims=True)
        dlo, dhi = dlo * g_ref[:, :QK_NOPE], dhi * g_ref[:, QK_NOPE:]
        mean = (jnp.sum(dlo * lo, axis=-1, keepdims=True) + jnp.sum(dhi * hi, axis=-1, keepdims=True)) * inv_width
        dx_ref[:, :QK_NOPE] = (r * (dlo - lo * mean)).astype(BF16)
        dxhi = r * (dhi - hi * mean)
        if is_k:
            dx_ref[:, QK_NOPE:] = dv_ref[...].astype(BF16)

            @pl.when(pl.program_id(1) == 0)
            def _():
                dkpe_ref[...] = jnp.zeros_like(dkpe_ref)

            dkpe_ref[...] += dxhi
        else:
            dx_ref[:, QK_NOPE:] = dxhi.astype(BF16)

    head = pl.BlockSpec((ts, QK_PAD), lambda r, h: (r, h))
    vec = pl.BlockSpec((1, QK_PAD), lambda r, h: (0, 0))
    in_specs = _head_specs(ts, kpe) + [vec, pl.BlockSpec((ts, 3 * LANE), lambda r, h: (r, 0)), head]
    ins = [x] + ([kpe] if is_k else []) + [g_pad, tabs, dout]
    out_specs = [head]
    out_shape = [jax.ShapeDtypeStruct((t, MLA_HEADS * QK_PAD), BF16)]
    if is_k:
        ins.append(dv)
        in_specs.append(pl.BlockSpec((ts, V_HEAD), lambda r, h: (r, h)))
        out_specs.append(pl.BlockSpec((ts, LANE), lambda r, h: (r, 0)))
        out_shape.append(jax.ShapeDtypeStruct((t, LANE), F32))
    out_specs.append(vec)
    out_shape.append(jax.ShapeDtypeStruct((1, QK_PAD), F32))
    return pl.pallas_call(body, name=name, grid=(t // ts, MLA_HEADS), in_specs=in_specs, out_specs=out_specs,
                          out_shape=out_shape, compiler_params=_params("arbitrary", "arbitrary"))(*ins)


def _causal_probs(q, k, q0):
    scale = (QK_NOPE + QK_ROPE) ** -0.5
    sc = lax.dot_general(q, k, (((1,), (1,)), ((), ())), preferred_element_type=F32) * scale
    qi = q0 + lax.broadcasted_iota(jnp.int32, sc.shape, 0)
    ki = lax.broadcasted_iota(jnp.int32, sc.shape, 1)
    sc = jnp.where(ki <= qi, sc, -1e30)
    e = jnp.exp(sc - jnp.max(sc, axis=-1, keepdims=True))
    return e / jnp.sum(e, axis=-1, keepdims=True)


def _attn_specs(s, tq):
    nq = s // tq
    q_spec = pl.BlockSpec((tq, QK_PAD), lambda b, h, i: (b * nq + i, h))
    o_spec = pl.BlockSpec((tq, V_HEAD), lambda b, h, i: (b * nq + i, h))
    k_spec = pl.BlockSpec((s, QK_PAD), lambda b, h, i: (b, h))
    v_spec = pl.BlockSpec((s, V_HEAD), lambda b, h, i: (b, 2 * h + 1))
    dv_spec = pl.BlockSpec((s, V_HEAD), lambda b, h, i: (b, h))
    return q_spec, o_spec, k_spec, v_spec, dv_spec


def _attn_fwd(qn, kn, kv_raw, nb, side):
    t = qn.shape[0]
    s = t // nb
    tq = _tile(s, 256, 8)
    nq = s // tq
    nsteps = nb * MLA_HEADS * nq
    ns_in, ns_out = len(side.inputs), len(side.out_shape)
    any_spec = pl.BlockSpec(memory_space=pl.ANY)
    q_spec, o_spec, k_spec, v_spec, _ = _attn_specs(s, tq)

    def body(*refs):
        (q_ref, k_ref, v_ref), sins, (o_ref,), souts, (sem,) = _refs_split(refs, (3, ns_in, 1, ns_out, 1))
        step = (pl.program_id(0) * MLA_HEADS + pl.program_id(1)) * nq + pl.program_id(2)
        side.begin(sins, souts, sem, step, nsteps)
        p = _causal_probs(q_ref[...], k_ref[...], pl.program_id(2) * tq)
        o_ref[...] = jnp.dot(p.astype(BF16), v_ref[...].astype(BF16), preferred_element_type=F32).astype(BF16)
        side.end(sins, souts, sem, step, nsteps)

    return pl.pallas_call(
        body, name="attn_fwd", grid=(nb, MLA_HEADS, nq),
        in_specs=[q_spec, k_spec, v_spec] + [any_spec] * ns_in,
        out_specs=[o_spec] + [any_spec] * ns_out,
        out_shape=[jax.ShapeDtypeStruct((t, MLA_HEADS * V_HEAD), BF16)] + list(side.out_shape),
        scratch_shapes=side.scratch(),
        compiler_params=_params("arbitrary", "arbitrary", "arbitrary"))(qn, kn, kv_raw, *side.inputs)


def _attn_bwd(qn, kn, kv_raw, dattn, nb, side):
    t = qn.shape[0]
    s = t // nb
    tq = _tile(s, 256, 8)
    nq = s // tq
    nsteps = nb * MLA_HEADS * nq
    scale = (QK_NOPE + QK_ROPE) ** -0.5
    ns_in, ns_out = len(side.inputs), len(side.out_shape)
    any_spec = pl.BlockSpec(memory_space=pl.ANY)
    q_spec, o_spec, k_spec, v_spec, dv_spec = _attn_specs(s, tq)

    def body(*refs):
        (q_ref, k_ref, v_ref, do_ref), sins, (dq_ref, dk_ref, dv_ref), souts, (sem,) = _refs_split(
            refs, (4, ns_in, 3, ns_out, 1))
        step = (pl.program_id(0) * MLA_HEADS + pl.program_id(1)) * nq + pl.program_id(2)
        side.begin(sins, souts, sem, step, nsteps)

        @pl.when(pl.program_id(2) == 0)
        def _():
            dk_ref[...] = jnp.zeros_like(dk_ref)
            dv_ref[...] = jnp.zeros_like(dv_ref)

        q, k, v, do = q_ref[...], k_ref[...], v_ref[...].astype(BF16), do_ref[...]
        p = _causal_probs(q, k, pl.program_id(2) * tq)
        dv_ref[...] += lax.dot_general(p.astype(BF16), do, (((0,), (0,)), ((), ())), preferred_element_type=F32)
        dp = lax.dot_general(do, v, (((1,), (1,)), ((), ())), preferred_element_type=F32)
        ds = (p * (dp - jnp.sum(p * dp, axis=-1, keepdims=True)) * scale).astype(BF16)
        dq_ref[...] = jnp.dot(ds, k, preferred_element_type=F32)
        dk_ref[...] += lax.dot_general(ds, q, (((0,), (0,)), ((), ())), preferred_element_type=F32)
        side.end(sins, souts, sem, step, nsteps)

    return pl.pallas_call(
        body, name="attn_bwd", grid=(nb, MLA_HEADS, nq),
        in_specs=[q_spec, k_spec, v_spec, o_spec] + [any_spec] * ns_in,
        out_specs=[q_spec, k_spec, dv_spec] + [any_spec] * ns_out,
        out_shape=[jax.ShapeDtypeStruct((t, MLA_HEADS * QK_PAD), F32),
                   jax.ShapeDtypeStruct((t, MLA_HEADS * QK_PAD), F32),
                   jax.ShapeDtypeStruct((t, MLA_HEADS * V_HEAD), F32)] + list(side.out_shape),
        scratch_shapes=side.scratch(),
        compiler_params=_params("arbitrary", "arbitrary", "arbitrary"))(qn, kn, kv_raw, dattn, *side.inputs)


def _gatemix_fwd(gpa, gpb, ya, yb):
    t, w = ya.shape
    tr = _tile(t, 128, 8)

    def body(ga_ref, gb_ref, ya_ref, yb_ref, o_ref):
        o_ref[...] = (jax.nn.sigmoid(ga_ref[...]) * ya_ref[...]
                      + jax.nn.sigmoid(gb_ref[...]) * yb_ref[...]).astype(BF16)

    row = pl.BlockSpec((tr, w), lambda i: (i, 0))
    return pl.pallas_call(body, name="gatemix_fwd", grid=(t // tr,), in_specs=[row] * 4, out_specs=row,
                          out_shape=jax.ShapeDtypeStruct((t, w), BF16),
                          compiler_params=_params("parallel"))(gpa, gpb, ya, yb)


def _gatemix_bwd(gpa, gpb, ya, yb, dmix):
    t, w = ya.shape
    tr = _tile(t, 128, 8)

    def body(ga_ref, gb_ref, ya_ref, yb_ref, dm_ref, dya_ref, dyb_ref, dga_ref, dgb_ref):
        dm = dm_ref[...]
        sa = jax.nn.sigmoid(ga_ref[...])
        sb = jax.nn.sigmoid(gb_ref[...])
        dya_ref[...] = (dm * sa).astype(BF16)
        dyb_ref[...] = (dm * sb).astype(BF16)
        dga_ref[...] = (dm * ya_ref[...] * sa * (1.0 - sa)).astype(BF16)
        dgb_ref[...] = (dm * yb_ref[...] * sb * (1.0 - sb)).astype(BF16)

    row = pl.BlockSpec((tr, w), lambda i: (i, 0))
    return pl.pallas_call(body, name="gatemix_bwd", grid=(t // tr,), in_specs=[row] * 5, out_specs=[row] * 4,
                          out_shape=[jax.ShapeDtypeStruct((t, w), BF16)] * 4,
                          compiler_params=_params("parallel"))(gpa, gpb, ya, yb, dmix)


def _final(x1, f2, gate, target, seq):
    t, w = x1.shape
    nex = t // seq
    tr = _row_tile(seq, w, 1024 * 1024)
    nb = seq // tr

    def body(x_ref, f_ref, gate_ref, t_ref, loss_ref, dy_ref, dff_ref, dgate_ref):
        i = pl.program_id(0)

        @pl.when(i == 0)
        def _():
            loss_ref[...] = jnp.zeros_like(loss_ref)

        @pl.when(i % nb == 0)
        def _():
            dgate_ref[...] = jnp.zeros_like(dgate_ref)

        fv = f_ref[...]
        gv = gate_ref[0]
        err = x_ref[...] + gv * fv - t_ref[...]
        sq = jnp.sum(err * err, axis=1, keepdims=True)
        loss_ref[...] += jnp.sum(sq, axis=0, keepdims=True) * (0.5 / w)
        dy = err * (1.0 / w)
        dy_ref[...] = dy
        dff_ref[...] = (dy * gv).astype(BF16)
        dgate_ref[0] += jnp.sum(dy * fv, axis=0, keepdims=True)

    row = pl.BlockSpec((tr, w), lambda i: (i, 0))
    per_ex = pl.BlockSpec((1, 1, w), lambda i: (i // nb, 0, 0))
    return pl.pallas_call(
        body, name="loss_head", grid=(t // tr,),
        in_specs=[row, row, per_ex, row],
        out_specs=[pl.BlockSpec((1, LANE), lambda i: (0, 0)), row, row, per_ex],
        out_shape=[jax.ShapeDtypeStruct((1, LANE), F32), jax.ShapeDtypeStruct((t, w), F32),
                   jax.ShapeDtypeStruct((t, w), BF16), jax.ShapeDtypeStruct((nex, 1, w), F32)],
        compiler_params=_params("arbitrary"))(x1, f2, gate, target)


def _sum_slots(parts, *, name):
    _, r, c = parts[0].shape
    tr, tc = _tile(r, 256, 16), _tile(c, 1024, LANE)

    def body(*refs):
        acc = None
        for p_ref in refs[:-1]:
            for j in range(p_ref.shape[0]):
                val = p_ref[j].astype(F32)
                acc = val if acc is None else acc + val
        refs[-1][...] = acc

    return pl.pallas_call(body, name=name, grid=(r // tr, c // tc),
                          in_specs=[pl.BlockSpec((p.shape[0], tr, tc), lambda i, j: (0, i, j)) for p in parts],
                          out_specs=pl.BlockSpec((tr, tc), lambda i, j: (i, j)),
                          out_shape=jax.ShapeDtypeStruct((r, c), F32),
                          compiler_params=_params("parallel", "parallel"))(*parts)


def _adamw(w, m, v, parts, *, name):
    _, r, c = w.shape
    parts = [p.reshape(w.shape) for p in parts]
    tc = _tile(c, 1024, LANE) if c % LANE == 0 else c
    tr = _tile(r, max(8, (256 * 1024) // tc // 8 * 8), 8)
    npart = len(parts)
    b1c = 1.0 - ADAM_B1 ** ADAM_STEP
    b2c = 1.0 - ADAM_B2 ** ADAM_STEP

    def body(*refs):
        w_ref, m_ref, v_ref = refs[:3]
        g_ref, d_ref, mo_ref, vo_ref = refs[3 + npart:]
        g = refs[3][...].astype(F32)
        for p_ref in refs[4:3 + npart]:
            g = g + p_ref[...].astype(F32)
        m2 = ADAM_B1 * m_ref[...] + (1.0 - ADAM_B1) * g
        v2 = ADAM_B2 * v_ref[...] + (1.0 - ADAM_B2) * jnp.square(g)
        m_hat = m2 / b1c
        v_hat = v2 / b2c
        g_ref[...] = g
        d_ref[...] = -ADAM_LR * (m_hat / (jnp.sqrt(v_hat) + ADAM_EPS) + ADAM_WD * w_ref[...])
        mo_ref[...] = m2
        vo_ref[...] = v2

    blk = pl.BlockSpec((1, tr, tc), lambda i, j: (0, i, j))
    return pl.pallas_call(body, name=name, grid=(r // tr, c // tc),
                          in_specs=[blk] * (3 + npart), out_specs=[blk] * 4,
                          out_shape=[jax.ShapeDtypeStruct((1, r, c), F32)] * 4,
                          compiler_params=_params("parallel", "parallel"))(w, m, v, *parts)


def _allgather8(x, *, name):
    def body(x_ref, o_ref, ssem, rsem):
        pos = _mesh_pos()
        me = 4 * pos[0] + 2 * pos[1] + pos[2]
        o_ref[me] = x_ref[...]
        cps = []
        for k in range(1, N_DEV):
            cp = pltpu.make_async_remote_copy(src_ref=x_ref, dst_ref=o_ref.at[me], send_sem=ssem.at[k - 1],
                                              recv_sem=rsem.at[k - 1], device_id=_flip(pos, k), device_id_type=MESH)
            cp.start()
            cps.append(cp)
        for cp in cps:
            cp.wait()

    return pl.pallas_call(
        body, name=name,
        out_shape=jax.ShapeDtypeStruct((N_DEV,) + x.shape, x.dtype),
        in_specs=[pl.BlockSpec(memory_space=pltpu.VMEM)],
        out_specs=pl.BlockSpec(memory_space=pltpu.VMEM),
        scratch_shapes=[pltpu.SemaphoreType.DMA((N_DEV - 1,)), pltpu.SemaphoreType.DMA((N_DEV - 1,))],
        compiler_params=pltpu.CompilerParams(vmem_limit_bytes=VMEM_LIMIT),
    )(x)


def _sibling_swap(p, *, name):
    def body(p_ref, o_ref, ssem, rsem):
        cp = pltpu.make_async_remote_copy(src_ref=p_ref, dst_ref=o_ref, send_sem=ssem, recv_sem=rsem,
                                          device_id=_flip(_mesh_pos(), 1), device_id_type=MESH)
        cp.start()
        cp.wait()

    return pl.pallas_call(
        body, name=name,
        out_shape=jax.ShapeDtypeStruct(p.shape, p.dtype),
        in_specs=[pl.BlockSpec(memory_space=pl.ANY)],
        out_specs=pl.BlockSpec(memory_space=pl.ANY),
        scratch_shapes=[pltpu.SemaphoreType.DMA, pltpu.SemaphoreType.DMA],
    )(p)


_SMALL = ("b_ada", "g_norm1", "g_v", "w_s", "b_s", "g_q_lat", "g_kv_lat", "g_qn", "g_kn", "g_norm2")
_BIG = ("w_in", "w_uq", "w_ukv", "w_branch_a", "w_branch_b", "w_out", "w_ff1", "w_ff2")
_SHARD_AXIS = {"w_in": 1, "w_uq": 1, "w_ukv": 1, "w_branch_a": 1, "w_branch_b": 0, "w_out": 0, "w_ff1": 1, "w_ff2": 0}
_WEIGHTS = ("w_ada", "b_ada", "g_norm1", "w_in", "g_v", "w_s", "b_s", "g_q_lat", "g_kv_lat", "w_uq", "w_ukv",
            "g_qn", "g_kn", "w_branch_a", "w_branch_b", "w_out", "g_norm2", "w_ff1", "w_ff2")


def kernel(x, c, positions, w_ada, b_ada, g_norm1, w_in, g_v, w_s, b_s, g_q_lat, g_kv_lat, w_uq, w_ukv, g_qn, g_kn, w_branch_a, w_branch_b, w_out, g_norm2, w_ff1, w_ff2, loss_target, m_w_ada, m_b_ada, m_g_norm1, m_w_in, m_g_v, m_w_s, m_b_s, m_g_q_lat, m_g_kv_lat, m_w_uq, m_w_ukv, m_g_qn, m_g_kn, m_w_branch_a, m_w_branch_b, m_w_out, m_g_norm2, m_w_ff1, m_w_ff2, v_w_ada, v_b_ada, v_g_norm1, v_w_in, v_g_v, v_w_s, v_b_s, v_g_q_lat, v_g_kv_lat, v_w_uq, v_w_ukv, v_g_qn, v_g_kn, v_w_branch_a, v_w_branch_b, v_w_out, v_g_norm2, v_w_ff1, v_w_ff2):
    wts = dict(w_ada=w_ada, b_ada=b_ada, g_norm1=g_norm1, w_in=w_in, g_v=g_v, w_s=w_s, b_s=b_s, g_q_lat=g_q_lat,
               g_kv_lat=g_kv_lat, w_uq=w_uq, w_ukv=w_ukv, g_qn=g_qn, g_kn=g_kn, w_branch_a=w_branch_a,
               w_branch_b=w_branch_b, w_out=w_out, g_norm2=g_norm2, w_ff1=w_ff1, w_ff2=w_ff2)
    mom1 = dict(w_ada=m_w_ada, b_ada=m_b_ada, g_norm1=m_g_norm1, w_in=m_w_in, g_v=m_g_v, w_s=m_w_s, b_s=m_b_s,
                g_q_lat=m_g_q_lat, g_kv_lat=m_g_kv_lat, w_uq=m_w_uq, w_ukv=m_w_ukv, g_qn=m_g_qn, g_kn=m_g_kn,
                w_branch_a=m_w_branch_a, w_branch_b=m_w_branch_b, w_out=m_w_out, g_norm2=m_g_norm2,
                w_ff1=m_w_ff1, w_ff2=m_w_ff2)
    mom2 = dict(w_ada=v_w_ada, b_ada=v_b_ada, g_norm1=v_g_norm1, w_in=v_w_in, g_v=v_g_v, w_s=v_w_s, b_s=v_b_s,
                g_q_lat=v_g_q_lat, g_kv_lat=v_g_kv_lat, w_uq=v_w_uq, w_ukv=v_w_ukv, g_qn=v_g_qn, g_kn=v_g_kn,
                w_branch_a=v_w_branch_a, w_branch_b=v_w_branch_b, w_out=v_w_out, g_norm2=v_g_norm2,
                w_ff1=v_w_ff1, w_ff2=v_w_ff2)

    nb, seq, dm = x.shape
    t = nb * seq
    nh, qkh = MLA_HEADS, QK_NOPE + QK_ROPE
    bh = nb * nh
    off_q = 2 * GM_WIDTH
    off_kv = off_q + Q_LORA
    off_kpe = off_kv + KV_LORA
    off_gate = off_kpe + QK_ROPE
    in_cols = off_gate + 2 * dm
    ins = in_cols // N_CHIP
    insp = _round_up(ins, LANE)
    ada_cols = N_MOD * dm // N_CHIP

    ix, iy, ic = _mesh_pos()
    me = 4 * ix + 2 * iy + ic
    chip = 2 * ix + iy
    xf = x.reshape(t, dm)
    tgt = loss_target.reshape(t, dm)

    c_all = _allgather8(c, name="ag_cond").reshape(N_DEV * nb, dm)
    cond = _silu(c_all)
    b_sh = lax.dynamic_slice(b_ada, (0, chip * ada_cols), (1, ada_cols))
    mod_sh = _mm(cond, w_ada[0], name="ada_fwd", tn=2048, tk=512) + b_sh
    mod8 = _allgather8(mod_sh, name="ag_mod")
    mod_all = jnp.concatenate([mod8[2 * s] for s in range(N_CHIP)], axis=1)
    mod = lax.dynamic_slice(mod_all, (nb * me, 0), (nb, N_MOD * dm))
    sh1, sc1, ga1, sh2, sc2, ga2 = [mod[:, j * dm:(j + 1) * dm].reshape(nb, 1, dm) for j in range(N_MOD)]

    shards = {n: wts[n][0].astype(BF16) for n in _BIG}
    shards["w_in"] = jnp.pad(shards["w_in"], ((0, 0), (0, insp - ins)))

    def gathers(*names):
        return _Side([_Gather(shards[n], _SHARD_AXIS[n]) for n in names])

    full = {}
    (full["w_in"],) = _comm_call(gathers("w_in"), name="ag_w_in")
    h1 = _norm_mod_fwd(xf, g_norm1, sh1, sc1, seq, name="norm1_fwd")
    early = ("w_branch_a", "w_uq", "w_ukv", "w_branch_b", "w_out")
    proj_p, *got = _mm(h1, full["w_in"], name="in_fwd", side=gathers(*early))
    full.update(zip(early, got))

    def shard_pieces(lo, hi):
        out = []
        for s in range(N_CHIP):
            a, b = max(lo, s * ins), min(hi, (s + 1) * ins)
            if a < b:
                out.append((s, a - s * ins, b - a))
        return out

    def seg(lo, hi):
        parts = [proj_p[:, s * insp + a:s * insp + a + w] for s, a, w in shard_pieces(lo, hi)]
        return parts[0] if len(parts) == 1 else jnp.concatenate(parts, axis=1)

    bounds = (0, off_q, off_kv, off_kpe, off_gate, off_gate + dm, in_cols)
    uv, q_lat, kv_lat, k_pe, gpa, gpb = [seg(lo, hi) for lo, hi in zip(bounds[:-1], bounds[1:])]

    b_col = b_s[0].reshape(GM_GROUPS, CHUNK, 1)
    a_out = _gmlp_fwd(uv, g_v, w_s[0], b_col)
    y_a = _mm(a_out, full["w_branch_a"], name="ba_fwd")

    zq = jnp.zeros((nb, 1, Q_LORA), F32)
    zkv = jnp.zeros((nb, 1, KV_LORA), F32)
    ql = _norm_mod_fwd(q_lat, g_q_lat, zq, zq, seq, name="qlat_norm_fwd")
    kvl = _norm_mod_fwd(kv_lat, g_kv_lat, zkv, zkv, seq, name="kvlat_norm_fwd")
    w_uq_p = jnp.pad(full["w_uq"].reshape(Q_LORA, nh, qkh), ((0, 0), (0, 0), (0, QK_PAD - qkh)))
    w_uq_p = w_uq_p.reshape(Q_LORA, nh * QK_PAD)
    q_raw = _mm(ql, w_uq_p, name="uq_fwd")
    kv_raw = _mm(kvl, full["w_ukv"], name="ukv_fwd")
    kpe_p = jnp.pad(k_pe, ((0, 0), (0, LANE - QK_ROPE)))
    pos = positions.astype(F32).reshape(nb, seq, 1)
    inv_freq = 1.0 / (ROPE_THETA ** (jnp.arange(0, QK_ROPE, 2, dtype=F32) / QK_ROPE))
    invf = jnp.concatenate([inv_freq, inv_freq, jnp.zeros((LANE - QK_ROPE,), F32)]).reshape(1, LANE)
    gq_pad = jnp.pad(g_qn, ((0, 0), (0, QK_PAD - qkh)))
    gk_pad = jnp.pad(g_kn, ((0, 0), (0, QK_PAD - qkh)))
    tabs = _rope_tables(pos, invf).reshape(t, 3 * LANE)
    qn = _qk_norm_fwd(q_raw, None, gq_pad, tabs, name="qnorm_fwd")
    kn = _qk_norm_fwd(kv_raw, kpe_p, gk_pad, tabs, name="knorm_fwd")
    attn, full["w_ff1"] = _attn_fwd(qn, kn, kv_raw, nb, gathers("w_ff1"))
    y_b = _mm(attn, full["w_branch_b"], name="bb_fwd")

    mixed = _gatemix_fwd(gpa, gpb, y_a, y_b)
    o1 = _mm(mixed, full["w_out"], name="out_fwd")
    x1, h2 = _norm_mod_fwd(xf, g_norm2, sh2, sc2, seq, name="norm2_fwd", resid=(o1, ga1))
    f1, act, full["w_ff2"] = _mm(h2, full["w_ff1"], name="ff1_fwd", out_dtypes=(BF16, BF16),
                                 epilogue=_relu2_epilogue, side=gathers("w_ff2"))
    f2 = _mm(act, full["w_ff2"], name="ff2_fwd")
    loss_acc, dy, dff, dga2 = _final(x1, f2, ga2, tgt, seq)
    loss = lax.psum(loss_acc[0, 0], ("x", "y", "c"))

    def scatter(n, peers, own):
        return _Scatter(gw[n], _SHARD_AXIS[n], peers, own)

    near, diag, everyone = (1, 2), (3,), (1, 2, 3)
    gw, rs = {}, {}
    gw["w_ff2"] = _mm(act, dff, name="ff2_dw", ta=True, out_dtypes=(BF16,))
    df1, ra = _mm(dff, full["w_ff2"], name="ff2_dx", tb=True, out_dtypes=(BF16,), epilogue=_relu2_bwd_epilogue,
                  extras=(f1,), side=_Side([scatter("w_ff2", near, True)]))
    gw["w_ff1"], rb = _mm(h2, df1, name="ff1_dw", ta=True, out_dtypes=(BF16,),
                          side=_Side([scatter("w_ff2", diag, False)]))
    rs["w_ff2"] = [ra, rb]
    dh2, ra = _mm(df1, full["w_ff1"], name="ff1_dx", tb=True, side=_Side([scatter("w_ff1", near, True)]))
    dx1, dsh2, dsc2, dg_norm2, do1, dga1 = _norm_mod_bwd(x1, g_norm2, sc2, dh2, seq, name="norm2_bwd", dres=dy,
                                                         gate_o=(o1, ga1))
    gw["w_out"] = _mm(mixed, do1, name="out_dw", ta=True, out_dtypes=(BF16,))
    dmixed = _mm(do1, full["w_out"], name="out_dx", tb=True)
    dya, dyb, dgpa, dgpb = _gatemix_bwd(gpa, gpb, y_a, y_b, dmixed)

    gw["w_branch_b"] = _mm(attn, dyb, name="bb_dw", ta=True, out_dtypes=(BF16,))
    dattn = _mm(dyb, full["w_branch_b"], name="bb_dx", tb=True, out_dtypes=(BF16,))
    dqn, dkn, dv, rb, r_out, r_bb = _attn_bwd(qn, kn, kv_raw, dattn, nb, _Side([
        scatter("w_ff1", diag, False), scatter("w_out", everyone, True), scatter("w_branch_b", everyone, True)]))
    rs["w_ff1"], rs["w_out"], rs["w_branch_b"] = [ra, rb], [r_out], [r_bb]
    dq_raw, dg_qn = _qk_norm_bwd(q_raw, None, gq_pad, tabs, dqn, None, name="qnorm_bwd")
    dkv_raw, dkpe_p, dg_kn = _qk_norm_bwd(kv_raw, kpe_p, gk_pad, tabs, dkn, dv, name="knorm_bwd")
    dk_pe = dkpe_p[:, :QK_ROPE]
    gw_uq_p = _mm(ql, dq_raw, name="uq_dw", ta=True, out_dtypes=(BF16,))
    gw["w_uq"] = gw_uq_p.reshape(Q_LORA, nh, QK_PAD)[:, :, :qkh].reshape(Q_LORA, nh * qkh)
    dql = _mm(dq_raw, w_uq_p, name="uq_dx", tb=True)
    gw["w_ukv"] = _mm(kvl, dkv_raw, name="ukv_dw", ta=True, out_dtypes=(BF16,))
    dkvl = _mm(dkv_raw, full["w_ukv"], name="ukv_dx", tb=True)
    dq_lat, _, _, dg_q_lat = _norm_mod_bwd(q_lat, g_q_lat, zq, dql, seq, name="qlat_norm_bwd")
    dkv_lat, _, _, dg_kv_lat = _norm_mod_bwd(kv_lat, g_kv_lat, zkv, dkvl, seq, name="kvlat_norm_bwd")

    gw["w_branch_a"] = _mm(a_out, dya, name="ba_dw", ta=True, out_dtypes=(BF16,))
    da = _mm(dya, full["w_branch_a"], name="ba_dx", tb=True)
    duv, dg_v, dw_s, db_col = _gmlp_bwd(uv, g_v, w_s[0], b_col, da)

    dsegs = (duv, dq_lat.astype(BF16), dkv_lat.astype(BF16), dk_pe.astype(BF16), dgpa, dgpb)
    by_shard = [[] for _ in range(N_CHIP)]
    for dseg, lo, hi in zip(dsegs, bounds[:-1], bounds[1:]):
        for s, a, w in shard_pieces(lo, hi):
            by_shard[s].append(dseg[:, s * ins + a - lo:s * ins + a - lo + w])
    dproj_p = jnp.concatenate([p for s in range(N_CHIP) for p in by_shard[s] + [jnp.zeros((t, insp - ins), BF16)]],
                              axis=1)
    half = dm // 2
    gw["in0"], r_uq, r_ukv, r_ba = _mm(h1, dproj_p, name="in_dw0", ta=True, out_dtypes=(BF16,), m_rows=half, side=_Side([
        scatter("w_uq", everyone, True), scatter("w_ukv", everyone, True), scatter("w_branch_a", everyone, True)]))
    rs["w_uq"], rs["w_ukv"], rs["w_branch_a"] = [r_uq], [r_ukv], [r_ba]
    gw["in1"], ra0 = _mm(h1, dproj_p, name="in_dw1", ta=True, out_dtypes=(BF16,), m_rows=half, m_start=half,
                         side=_Side([_Scatter(gw["in0"], 1, near, True)]))
    dh1, rb0, ra1 = _mm(dproj_p, full["w_in"], name="in_dx", tb=True,
                        side=_Side([_Scatter(gw["in0"], 1, diag, False), _Scatter(gw["in1"], 1, near, True)]))
    (rb1,) = _comm_call(_Side([_Scatter(gw["in1"], 1, diag, False)]), name="rs_w_in_tail")
    rs["w_in"] = [jnp.concatenate([ra0, ra1], axis=1), jnp.concatenate([rb0, rb1], axis=1)]
    dx, dsh1, dsc1, dg_norm1 = _norm_mod_bwd(xf, g_norm1, sc1, dh1, seq, name="norm1_bwd", dres=dx1)
    grad_x = dx.reshape(nb, seq, dm)

    gmod = jnp.concatenate([dsh1, dsc1, dga1, dsh2, dsc2, dga2], axis=-1).reshape(nb, N_MOD * dm)
    gmod_all = _allgather8(gmod, name="ag_gmod").reshape(N_DEV * nb, N_MOD * dm)
    gmod_sh = lax.dynamic_slice(gmod_all, (0, chip * ada_cols), (N_DEV * nb, ada_cols))
    g_w_ada = _mm(cond, gmod_sh, name="ada_dw", ta=True, tn=2048)

    out_g, out_d, out_m, out_v = {}, {}, {}, {}
    out_g["w_ada"], out_d["w_ada"], out_m["w_ada"], out_v["w_ada"] = _adamw(
        w_ada, m_w_ada, v_w_ada, [g_w_ada], name="adamw_w_ada")
    for n in _BIG:
        part = _sum_slots(rs[n], name="sum4_" + n)
        other = _sibling_swap(part, name="swap_" + n)
        if n == "w_in":
            part, other = part[:, :ins], other[:, :ins]
        out_g[n], out_d[n], out_m[n], out_v[n] = _adamw(wts[n], mom1[n], mom2[n], [part, other], name="adamw_" + n)

    small_g = dict(b_ada=jnp.sum(gmod, axis=0), g_norm1=dg_norm1, g_v=dg_v, w_s=dw_s, b_s=db_col,
                   g_q_lat=dg_q_lat, g_kv_lat=dg_kv_lat, g_qn=dg_qn[:, :qkh], g_kn=dg_kn[:, :qkh],
                   g_norm2=dg_norm2)

    def pack(d):
        flat = jnp.concatenate([d[n].reshape(-1) for n in _SMALL])
        return jnp.pad(flat, (0, _round_up(flat.shape[0], 8 * LANE) - flat.shape[0])).reshape(-1, LANE)

    g8 = _allgather8(pack(small_g), name="ag_small_grads")
    sg, sd, sm, sv = _adamw(pack({n: wts[n] for n in _SMALL})[None], pack({n: mom1[n] for n in _SMALL})[None],
                            pack({n: mom2[n] for n in _SMALL})[None], [g8[d] for d in range(N_DEV)],
                            name="adamw_small")
    off = 0
    for n in _SMALL:
        size = wts[n].size
        for dst, src in ((out_g, sg), (out_d, sd), (out_m, sm), (out_v, sv)):
            dst[n] = src.reshape(-1)[off:off + size].reshape(wts[n].shape)
        off += size

    def shaped(d, n):
        return d[n].reshape(wts[n].shape)

    return (loss, grad_x, *[shaped(out_g, n) for n in _WEIGHTS], *[shaped(out_d, n) for n in _WEIGHTS],
            *[shaped(out_m, n) for n in _WEIGHTS], *[shaped(out_v, n) for n in _WEIGHTS])
```

```python
import functools
import math

import jax
import jax.numpy as jnp
from jax import lax
from jax.experimental import pallas as pl
from jax.experimental.pallas import tpu as pltpu

F32 = jnp.float32
BF16 = jnp.bfloat16

GM_WIDTH = 2048
GM_GROUPS = 8
CHUNK = 128
MLA_HEADS = 32
QK_NOPE = 128
QK_ROPE = 64
V_HEAD = 128
Q_LORA = 1024
KV_LORA = 512
ROPE_THETA = 10000.0
N_MOD = 6
EPS = 1e-6
ADAM_LR = 0.001
ADAM_B1 = 0.9
ADAM_B2 = 0.999
ADAM_EPS = 1e-08
ADAM_WD = 0.01
ADAM_STEP = 10

N_CHIP = 4
N_DEV = 8
LANE = 128
QK_PAD = 256
VMEM_LIMIT = 56 * 1024 * 1024
MESH = pl.DeviceIdType.MESH


def _round_up(n, m):
    return (n + m - 1) // m * m


def _tile(n, target, align):
    t = min(target, n) // align * align
    while t >= align:
        if n % t == 0:
            return t
        t -= align
    return n


def _params(*sem):
    return pltpu.CompilerParams(dimension_semantics=sem, vmem_limit_bytes=VMEM_LIMIT)


def _mesh_pos():
    return lax.axis_index("x"), lax.axis_index("y"), lax.axis_index("c")


def _flip(pos, k):
    ix, iy, ic = pos
    return (1 - ix if k & 4 else ix, 1 - iy if k & 2 else iy, 1 - ic if k & 1 else ic)


def _chip_of(pos):
    return 2 * pos[0] + pos[1]


def _window(ref, axis, chip, size):
    start = pl.multiple_of(chip * size, LANE if axis == 1 else 16)
    if axis == 1:
        return ref.at[:, pl.ds(start, size)]
    return ref.at[pl.ds(start, size), :]


def _remote(src, dst, sem, send, recv, peer):
    return pltpu.make_async_remote_copy(src_ref=src, dst_ref=dst, send_sem=sem.at[send], recv_sem=sem.at[recv],
                                        device_id=peer, device_id_type=MESH)


class _Gather:
    n_sem = 13

    def __init__(self, wsh, axis):
        self.axis, self.size, self.rows = axis, wsh.shape[axis], wsh.shape[0]
        self.inputs = [wsh]
        self.out_shape = [jax.ShapeDtypeStruct(
            tuple(d * N_CHIP if a == axis else d for a, d in enumerate(wsh.shape)), wsh.dtype)]

    def _half(self, ref, chip, c, local=False):
        h = self.rows // 2
        if self.axis == 1:
            rows = pl.ds(pl.multiple_of(c * h, 16), h)
            return ref.at[rows, :] if local else ref.at[rows, pl.ds(pl.multiple_of(chip * self.size, LANE), self.size)]
        base = 0 if local else chip * self.size
        return ref.at[pl.ds(pl.multiple_of(base + c * h, 16), h), :]

    def start(self, ins, outs, sem, s0):
        (w_ref,), (o_ref,) = ins, outs
        pos = _mesh_pos()
        chip, c = _chip_of(pos), pos[2]
        pltpu.make_async_copy(w_ref, _window(o_ref, self.axis, chip, self.size), sem.at[s0]).start()
        for j in range(1, N_CHIP):
            _remote(self._half(w_ref, chip, c, local=True), self._half(o_ref, chip, c), sem, s0 + j, s0 + 3 + j,
                    _flip(pos, 2 * j)).start()

    def relay(self, ins, outs, sem, s0):
        (o_ref,) = outs
        pos = _mesh_pos()
        for j in range(1, N_CHIP):
            peer = _flip(pos, 2 * j)
            blk = self._half(o_ref, _chip_of(peer), pos[2])
            _remote(blk, blk, sem, s0 + j, s0 + 3 + j, peer).wait_recv()
            _remote(blk, blk, sem, s0 + 6 + j, s0 + 9 + j, _flip(pos, 1)).start()

    def finish(self, ins, outs, sem, s0):
        (w_ref,), (o_ref,) = ins, outs
        pos = _mesh_pos()
        chip, c = _chip_of(pos), pos[2]
        sibling = _flip(pos, 1)
        for j in range(1, N_CHIP):
            peer = _flip(pos, 2 * j)
            _remote(self._half(w_ref, chip, c, local=True), self._half(o_ref, chip, c), sem, s0 + j, s0 + 3 + j,
                    peer).wait_send()
            blk = self._half(o_ref, _chip_of(peer), c)
            _remote(blk, blk, sem, s0 + 6 + j, s0 + 9 + j, sibling).wait_send()
            got = self._half(o_ref, _chip_of(peer), 1 - c)
            _remote(got, got, sem, s0 + 6 + j, s0 + 9 + j, sibling).wait_recv()
        pltpu.make_async_copy(w_ref, _window(o_ref, self.axis, chip, self.size), sem.at[s0]).wait()


class _Scatter:
    def __init__(self, g, axis, peers, own):
        self.axis, self.size, self.peers, self.own = axis, g.shape[axis] // N_CHIP, tuple(peers), int(own)
        shard = tuple(self.size if a == axis else d for a, d in enumerate(g.shape))
        self.n_sem = self.own + 2 * len(self.peers)
        self.inputs = [g]
        self.out_shape = [jax.ShapeDtypeStruct((self.own + len(self.peers),) + shard, g.dtype)]

    def _copies(self, ins, outs, sem, s0):
        (g_ref,), (o_ref,) = ins, outs
        pos = _mesh_pos()
        cps = []
        if self.own:
            cps.append(pltpu.make_async_copy(_window(g_ref, self.axis, _chip_of(pos), self.size), o_ref.at[0],
                                             sem.at[s0]))
        for n, j in enumerate(self.peers):
            peer = _flip(pos, 2 * j)
            cps.append(_remote(_window(g_ref, self.axis, _chip_of(peer), self.size), o_ref.at[self.own + n], sem,
                               s0 + self.own + 2 * n, s0 + self.own + 2 * n + 1, peer))
        return cps

    def start(self, ins, outs, sem, s0):
        for cp in self._copies(ins, outs, sem, s0):
            cp.start()

    def relay(self, ins, outs, sem, s0):
        pass

    def finish(self, ins, outs, sem, s0):
        for cp in self._copies(ins, outs, sem, s0):
            cp.wait()


class _Side:
    def __init__(self, jobs):
        self.jobs = list(jobs)
        self.inputs = [a for job in self.jobs for a in job.inputs]
        self.out_shape = [s for job in self.jobs for s in job.out_shape]

    def scratch(self):
        return [pltpu.SemaphoreType.DMA((sum(job.n_sem for job in self.jobs),))]

    def _run(self, phase, ins, outs, sem):
        i = o = s = 0
        for job in self.jobs:
            ni, no = len(job.inputs), len(job.out_shape)
            getattr(job, phase)(ins[i:i + ni], outs[o:o + no], sem, s)
            i, o, s = i + ni, o + no, s + job.n_sem

    def begin(self, ins, outs, sem, step, nsteps):
        if nsteps == 1:
            self._run("start", ins, outs, sem)
        else:
            pl.when(step == 0)(lambda: self._run("start", ins, outs, sem))

    def end(self, ins, outs, sem, step, nsteps):
        if nsteps == 1:
            self._run("relay", ins, outs, sem)
            self._run("finish", ins, outs, sem)
        else:
            pl.when(step == nsteps - max(1, nsteps // 8))(lambda: self._run("relay", ins, outs, sem))
            pl.when(step == nsteps - 1)(lambda: self._run("finish", ins, outs, sem))


def _comm_call(side, *, name):
    n_in = len(side.inputs)

    def body(*refs):
        ins, outs, sem = refs[:n_in], refs[n_in:-1], refs[-1]
        side.begin(ins, outs, sem, 0, 1)
        side.end(ins, outs, sem, 0, 1)

    any_spec = pl.BlockSpec(memory_space=pl.ANY)
    return pl.pallas_call(body, name=name, out_shape=list(side.out_shape), in_specs=[any_spec] * n_in,
                          out_specs=[any_spec] * len(side.out_shape), scratch_shapes=side.scratch())(*side.inputs)


_HBM = pl.BlockSpec(memory_space=pltpu.HBM)
_SEM = pl.BlockSpec(memory_space=pltpu.SEMAPHORE)
_DATAFLOW = pltpu.SideEffectType.DATAFLOW_SIDE_EFFECTING


def _scatter_start(g, axis, *, name):
    job = _Scatter(g, axis, (1, 2, 3), True)
    land = job.out_shape[0]

    def body(g_ref, land_ref, sem, g_thru, land_thru, token):
        job.start([g_ref], [land_ref], sem, 0)
        token[...] = jnp.zeros_like(token)

    sem, g_thru, land_thru, token = pl.pallas_call(
        body, name=name,
        out_shape=(pltpu.SemaphoreType.DMA((job.n_sem,)), pltpu.HBM(g.shape, g.dtype),
                   pltpu.HBM(land.shape, land.dtype), jax.ShapeDtypeStruct((8, LANE), F32)),
        in_specs=(_HBM, _HBM), out_specs=(_SEM, _HBM, _HBM, pl.BlockSpec(memory_space=pltpu.VMEM)),
        input_output_aliases={0: 1, 1: 2},
        compiler_params=pltpu.CompilerParams(has_side_effects=_DATAFLOW),
    )(pltpu.with_memory_space_constraint(g, pltpu.HBM),
      pltpu.with_memory_space_constraint(lax.empty(land.shape, land.dtype), pltpu.HBM))
    return (job, sem, g_thru, land_thru), token


def _scatter_wait(started, after, *, name):
    job, sem, g_thru, land_thru = started

    def body(g_ref, land_ref, sem_ref, after_ref, g_dead, got_ref):
        job.finish([g_ref], [land_ref], sem_ref, 0)

    return pl.pallas_call(
        body, name=name,
        out_shape=(pltpu.HBM(g_thru.shape, g_thru.dtype), pltpu.HBM(land_thru.shape, land_thru.dtype)),
        in_specs=(_HBM, _HBM, _SEM, pl.BlockSpec(memory_space=pl.ANY)), out_specs=(_HBM, _HBM),
        input_output_aliases={0: 0, 1: 1},
        compiler_params=pltpu.CompilerParams(has_side_effects=_DATAFLOW),
    )(g_thru, land_thru, sem, after)[1]


def _refs_split(refs, counts):
    out, p = [], 0
    for n in counts:
        out.append(refs[p:p + n])
        p += n
    return out


def _mm(a, b, *, name, ta=False, tb=False, out_dtypes=(F32,), epilogue=None, extras=(),
        tm=1024, tn=1024, tk=1024, m_rows=None, m_start=0, side=None, after=()):
    after = tuple(after)
    m, k = (a.shape[1], a.shape[0]) if ta else a.shape
    n = b.shape[0] if tb else b.shape[1]
    assert k == (b.shape[1] if tb else b.shape[0]), (a.shape, b.shape)
    m = m if m_rows is None else m_rows
    tm = _tile(m, tm, LANE if ta else 16)
    tn = _tile(n, tn, LANE)
    tk = _tile(k, tk, LANE)
    assert m_start % tm == 0
    mo = m_start // tm
    gi, gj, nk = m // tm, n // tn, k // tk
    nsteps = gi * gj * nk
    ne, no = len(extras), len(out_dtypes)
    ns_in, ns_out = (len(side.inputs), len(side.out_shape)) if side else (0, 0)
    dims = (((0 if ta else 1,), (1 if tb else 0,)), ((), ()))

    def body(*refs):
        (a_ref, b_ref), ex, sins, _, outs, souts, (acc,), sems = _refs_split(
            refs, (2, ne, ns_in, len(after), no, ns_out, 1, 1 if side else 0))
        kk = pl.program_id(2)
        step = (pl.program_id(0) * gj + pl.program_id(1)) * nk + kk
        if side:
            side.begin(sins, souts, sems[0], step, nsteps)

        def prod():
            return lax.dot_general(a_ref[...].astype(BF16), b_ref[...].astype(BF16), dims,
                                   preferred_element_type=F32)

        def emit(r):
            res = epilogue(r, *[e[...] for e in ex]) if epilogue is not None else (r,)
            for o, val in zip(outs, res):
                o[...] = val.astype(o.dtype)

        if nk == 1:
            emit(prod())
        else:
            @pl.when(kk == 0)
            def _():
                acc[...] = prod()

            @pl.when((kk > 0) & (kk < nk - 1))
            def _():
                acc[...] += prod()

            @pl.when(kk == nk - 1)
            def _():
                emit(acc[...] + prod())

        if side:
            side.end(sins, souts, sems[0], step, nsteps)

    a_spec = (pl.BlockSpec((tk, tm), lambda i, j, q: (q, i + mo)) if ta
              else pl.BlockSpec((tm, tk), lambda i, j, q: (i + mo, q)))
    b_spec = pl.BlockSpec((tn, tk), lambda i, j, q: (j, q)) if tb else pl.BlockSpec((tk, tn), lambda i, j, q: (q, j))
    o_spec = pl.BlockSpec((tm, tn), lambda i, j, q: (i, j))
    any_spec = pl.BlockSpec(memory_space=pl.ANY)
    res = pl.pallas_call(
        body, name=name,
        grid=(gi, gj, nk),
        in_specs=[a_spec, b_spec] + [o_spec] * ne + [any_spec] * (ns_in + len(after)),
        out_specs=[o_spec] * no + [any_spec] * ns_out,
        out_shape=[jax.ShapeDtypeStruct((m, n), dt) for dt in out_dtypes] + (list(side.out_shape) if side else []),
        scratch_shapes=[pltpu.VMEM((tm, tn), F32)] + (side.scratch() if side else []),
        compiler_params=_params(*(("arbitrary",) * 3 if side else ("parallel", "parallel", "arbitrary"))),
    )(a, b, *extras, *(side.inputs if side else ()), *after)
    return res[0] if len(res) == 1 else res


def _relu2_epilogue(r):
    return r, jnp.square(jnp.maximum(r, 0.0))


def _relu2_bwd_epilogue(r, f1):
    return (r * (2.0 * jnp.maximum(f1.astype(F32), 0.0)),)


def _row_tile(seq, width, nbytes=2 * 1024 * 1024):
    return _tile(seq, max(8, nbytes // (4 * width)), 8)


def _silu(c_all):
    def body(c_ref, o_ref):
        v = c_ref[...]
        o_ref[...] = v * jax.nn.sigmoid(v)

    return pl.pallas_call(body, name="silu", out_shape=jax.ShapeDtypeStruct(c_all.shape, F32))(c_all)


def _norm_mod_fwd(x, g, shift, scale, seq, *, name, resid=None):
    t, w = x.shape
    tr = _row_tile(seq, w, (1 if resid is not None else 2) * 1024 * 1024)
    nb = seq // tr
    has_res = resid is not None

    def body(*refs):
        if has_res:
            x_ref, o_ref, gate_ref, g_ref, sh_ref, sc_ref, x1_ref, h_ref = refs
            xv = x_ref[...] + gate_ref[0] * o_ref[...]
            x1_ref[...] = xv
        else:
            x_ref, g_ref, sh_ref, sc_ref, h_ref = refs
            xv = x_ref[...]
        r = lax.rsqrt(jnp.mean(xv * xv, axis=-1, keepdims=True) + EPS)
        nrm = xv * r * g_ref[...]
        h_ref[...] = (nrm * (1.0 + sc_ref[0]) + sh_ref[0]).astype(BF16)

    row = pl.BlockSpec((tr, w), lambda i: (i, 0))
    vec = pl.BlockSpec((1, w), lambda i: (0, 0))
    per_ex = pl.BlockSpec((1, 1, w), lambda i: (i // nb, 0, 0))
    if has_res:
        o, gate = resid
        ins, in_specs = (x, o, gate, g, shift, scale), [row, row, per_ex, vec, per_ex, per_ex]
        out_shape = [jax.ShapeDtypeStruct((t, w), F32), jax.ShapeDtypeStruct((t, w), BF16)]
        out_specs = [row, row]
    else:
        ins, in_specs = (x, g, shift, scale), [row, vec, per_ex, per_ex]
        out_shape = jax.ShapeDtypeStruct((t, w), BF16)
        out_specs = row
    return pl.pallas_call(body, name=name, grid=(t // tr,), in_specs=in_specs, out_specs=out_specs,
                          out_shape=out_shape, compiler_params=_params("parallel"))(*ins)


def _norm_mod_bwd(x, g, scale, dh, seq, *, name, dres=None, gate_o=None):
    t, w = x.shape
    nex = t // seq
    tr = _row_tile(seq, w, 1024 * 1024)
    nb = seq // tr
    has_res, has_gate = dres is not None, gate_o is not None

    def body(*refs):
        refs = list(refs)
        x_ref, g_ref, sc_ref, dh_ref = refs[:4]
        p = 4
        dres_ref = None
        if has_res:
            dres_ref = refs[p]
            p += 1
        if has_gate:
            o_ref, gate_ref = refs[p:p + 2]
            p += 2
        dx_ref, dsh_ref, dsc_ref, dg_ref = refs[p:p + 4]
        p += 4
        i = pl.program_id(0)

        @pl.when(i % nb == 0)
        def _():
            dsh_ref[...] = jnp.zeros_like(dsh_ref)
            dsc_ref[...] = jnp.zeros_like(dsc_ref)
            if has_gate:
                refs[p + 1][...] = jnp.zeros_like(refs[p + 1])

        @pl.when(i == 0)
        def _():
            dg_ref[...] = jnp.zeros_like(dg_ref)

        xv = x_ref[...]
        gv = g_ref[...]
        dhv = dh_ref[...]
        r = lax.rsqrt(jnp.mean(xv * xv, axis=-1, keepdims=True) + EPS)
        xh = xv * r
        dsh_ref[0] += jnp.sum(dhv, axis=0, keepdims=True)
        dsc_ref[0] += jnp.sum(dhv * (xh * gv), axis=0, keepdims=True)
        dn = dhv * (1.0 + sc_ref[0])
        dg_ref[...] += jnp.sum(dn * xh, axis=0, keepdims=True)
        dxh = dn * gv
        dx = r * (dxh - xh * jnp.mean(dxh * xh, axis=-1, keepdims=True))
        if has_res:
            dx = dx + dres_ref[...]
        dx_ref[...] = dx
        if has_gate:
            do_ref, dgate_ref = refs[p:p + 2]
            do_ref[...] = (dx * gate_ref[0]).astype(BF16)
            dgate_ref[0] += jnp.sum(dx * o_ref[...], axis=0, keepdims=True)

    row = pl.BlockSpec((tr, w), lambda i: (i, 0))
    vec = pl.BlockSpec((1, w), lambda i: (0, 0))
    per_ex = pl.BlockSpec((1, 1, w), lambda i: (i // nb, 0, 0))
    ins, in_specs = [x, g, scale, dh], [row, vec, per_ex, row]
    if has_res:
        ins.append(dres)
        in_specs.append(row)
    if has_gate:
        ins += list(gate_o)
        in_specs += [row, per_ex]
    ex_shape = jax.ShapeDtypeStruct((nex, 1, w), F32)
    out_shape = [jax.ShapeDtypeStruct((t, w), F32), ex_shape, ex_shape, jax.ShapeDtypeStruct((1, w), F32)]
    out_specs = [row, per_ex, per_ex, vec]
    if has_gate:
        out_shape += [jax.ShapeDtypeStruct((t, w), BF16), ex_shape]
        out_specs += [row, per_ex]
    return pl.pallas_call(body, name=name, grid=(t // tr,), in_specs=in_specs, out_specs=out_specs,
                          out_shape=out_shape, compiler_params=_params("arbitrary"))(*ins)


def _gelu_parts(xv):
    cdf = 0.5 * (1.0 + lax.erf(xv * (1.0 / math.sqrt(2.0))))
    return cdf


def _tril_mask():
    r = lax.broadcasted_iota(jnp.int32, (CHUNK, CHUNK), 0)
    c = lax.broadcasted_iota(jnp.int32, (CHUNK, CHUNK), 1)
    return c <= r


def _gmlp_fwd(uv, g_v, w_s, b_col):
    t = uv.shape[0]
    gw = GM_WIDTH // GM_GROUPS
    nck = 2
    tr = nck * CHUNK

    def body(uv_ref, gv_ref, w_ref, b_ref, o_ref):
        mask = _tril_mask()
        for ck in range(nck):
            rows = pl.ds(ck * CHUNK, CHUNK)
            xv = uv_ref[rows, :]
            z = xv * _gelu_parts(xv)
            u, v = z[:, :GM_WIDTH], z[:, GM_WIDTH:]
            r = lax.rsqrt(jnp.mean(v * v, axis=-1, keepdims=True) + EPS)
            vn = (v * r * gv_ref[...]).astype(BF16)
            for gi in range(GM_GROUPS):
                cols = slice(gi * gw, (gi + 1) * gw)
                wg = jnp.where(mask, w_ref[gi], 0.0).astype(BF16)
                mx = jnp.dot(wg, vn[:, cols], preferred_element_type=F32) + b_ref[gi]
                o_ref[rows, cols] = (u[:, cols] * mx).astype(BF16)

    return pl.pallas_call(
        body, name="gmlp_fwd", grid=(t // tr,),
        in_specs=[pl.BlockSpec((tr, 2 * GM_WIDTH), lambda i: (i, 0)),
                  pl.BlockSpec((1, GM_WIDTH), lambda i: (0, 0)),
                  pl.BlockSpec((GM_GROUPS, CHUNK, CHUNK), lambda i: (0, 0, 0)),
                  pl.BlockSpec((GM_GROUPS, CHUNK, 1), lambda i: (0, 0, 0))],
        out_specs=pl.BlockSpec((tr, GM_WIDTH), lambda i: (i, 0)),
        out_shape=jax.ShapeDtypeStruct((t, GM_WIDTH), BF16),
        compiler_params=_params("parallel"))(uv, g_v, w_s, b_col)


def _gmlp_bwd(uv, g_v, w_s, b_col, da):
    t = uv.shape[0]
    gw = GM_WIDTH // GM_GROUPS
    nck = 2
    tr = nck * CHUNK
    inv_sqrt_2pi = 1.0 / math.sqrt(2.0 * math.pi)

    def body(uv_ref, gv_ref, w_ref, b_ref, da_ref, duv_ref, dgv_ref, dw_ref, db_ref, dvn_ref):
        @pl.when(pl.program_id(0) == 0)
        def _():
            dgv_ref[...] = jnp.zeros_like(dgv_ref)
            dw_ref[...] = jnp.zeros_like(dw_ref)
            db_ref[...] = jnp.zeros_like(db_ref)

        mask = _tril_mask()
        for ck in range(nck):
            rows = pl.ds(ck * CHUNK, CHUNK)
            xv = uv_ref[rows, :]
            cdf = _gelu_parts(xv)
            z = xv * cdf
            u, v = z[:, :GM_WIDTH], z[:, GM_WIDTH:]
            r = lax.rsqrt(jnp.mean(v * v, axis=-1, keepdims=True) + EPS)
            vh = v * r
            gv = gv_ref[...]
            vn = (vh * gv).astype(BF16)
            dav = da_ref[rows, :]
            du_parts = []
            for gi in range(GM_GROUPS):
                cols = slice(gi * gw, (gi + 1) * gw)
                wg = jnp.where(mask, w_ref[gi], 0.0).astype(BF16)
                vng = vn[:, cols]
                mx = jnp.dot(wg, vng, preferred_element_type=F32) + b_ref[gi]
                du_parts.append(dav[:, cols] * mx)
                dmx = dav[:, cols] * u[:, cols]
                db_ref[gi] += jnp.sum(dmx, axis=1, keepdims=True)
                dmb = dmx.astype(BF16)
                dwg = lax.dot_general(dmb, vng, (((1,), (1,)), ((), ())), preferred_element_type=F32)
                dw_ref[gi] += jnp.where(mask, dwg, 0.0)
                dvn_ref[:, cols] = lax.dot_general(wg, dmb, (((0,), (0,)), ((), ())),
                                                   preferred_element_type=F32)
            dvn = dvn_ref[...]
            dgv_ref[...] += jnp.sum(dvn * vh, axis=0, keepdims=True)
            dvh = dvn * gv
            dv = r * (dvh - vh * jnp.mean(dvh * vh, axis=-1, keepdims=True))
            dz = jnp.concatenate(du_parts + [dv], axis=1)
            dgelu = cdf + xv * (jnp.exp(-0.5 * xv * xv) * inv_sqrt_2pi)
            duv_ref[rows, :] = (dz * dgelu).astype(BF16)

    return pl.pallas_call(
        body, name="gmlp_bwd", grid=(t // tr,),
        in_specs=[pl.BlockSpec((tr, 2 * GM_WIDTH), lambda i: (i, 0)),
                  pl.BlockSpec((1, GM_WIDTH), lambda i: (0, 0)),
                  pl.BlockSpec((GM_GROUPS, CHUNK, CHUNK), lambda i: (0, 0, 0)),
                  pl.BlockSpec((GM_GROUPS, CHUNK, 1), lambda i: (0, 0, 0)),
                  pl.BlockSpec((tr, GM_WIDTH), lambda i: (i, 0))],
        out_specs=[pl.BlockSpec((tr, 2 * GM_WIDTH), lambda i: (i, 0)),
                   pl.BlockSpec((1, GM_WIDTH), lambda i: (0, 0)),
                   pl.BlockSpec((GM_GROUPS, CHUNK, CHUNK), lambda i: (0, 0, 0)),
                   pl.BlockSpec((GM_GROUPS, CHUNK, 1), lambda i: (0, 0, 0))],
        out_shape=[jax.ShapeDtypeStruct((t, 2 * GM_WIDTH), BF16),
                   jax.ShapeDtypeStruct((1, GM_WIDTH), F32),
                   jax.ShapeDtypeStruct((GM_GROUPS, CHUNK, CHUNK), F32),
                   jax.ShapeDtypeStruct((GM_GROUPS, CHUNK, 1), F32)],
        scratch_shapes=[pltpu.VMEM((CHUNK, GM_WIDTH), F32)],
        compiler_params=_params("arbitrary"))(uv, g_v, w_s, b_col, da)


def _rope_tables(pos, invf):
    nb, s, _ = pos.shape
    ts = _tile(s, 512, 8)
    half = QK_ROPE // 2

    def body(pos_ref, invf_ref, o_ref):
        ang = pos_ref[0] * invf_ref[...]
        cs, sn = jnp.cos(ang), jnp.sin(ang)
        lane = lax.broadcasted_iota(jnp.int32, (1, LANE), 1)
        o_ref[0, :, :LANE] = jnp.where(lane < QK_ROPE, cs, 0.0)
        o_ref[0, :, LANE:2 * LANE] = jnp.where((lane >= half) & (lane < QK_ROPE), sn, 0.0)
        o_ref[0, :, 2 * LANE:] = jnp.where(lane < half, -sn, 0.0)

    return pl.pallas_call(
        body, name="rope_tables", grid=(nb, s // ts),
        in_specs=[pl.BlockSpec((1, ts, 1), lambda b, i: (b, i, 0)), pl.BlockSpec((1, LANE), lambda b, i: (0, 0))],
        out_specs=pl.BlockSpec((1, ts, 3 * LANE), lambda b, i: (b, i, 0)),
        out_shape=jax.ShapeDtypeStruct((nb, s, 3 * LANE), F32),
        compiler_params=_params("parallel", "parallel"))(pos, invf)


def _head_rstd(lo, hi):
    ss = jnp.sum(lo * lo, axis=-1, keepdims=True) + jnp.sum(hi * hi, axis=-1, keepdims=True)
    return lax.rsqrt(ss * (1.0 / (QK_NOPE + QK_ROPE)) + EPS)


def _head_specs(ts, kpe):
    if kpe is None:
        return [pl.BlockSpec((ts, QK_PAD), lambda r, h: (r, h))]
    return [pl.BlockSpec((ts, QK_NOPE), lambda r, h: (r, 2 * h)), pl.BlockSpec((ts, LANE), lambda r, h: (r, 0))]


def _head_tiles(x_refs):
    if len(x_refs) == 1:
        return x_refs[0][:, :QK_NOPE], x_refs[0][:, QK_NOPE:]
    return x_refs[0][...], x_refs[1][...]


def _qk_norm_fwd(x, kpe, g_pad, tabs, *, name):
    t = x.shape[0]
    ts = _tile(t, 1024, 8)
    half = QK_ROPE // 2
    nx = 1 if kpe is None else 2

    def body(*refs):
        x_refs, (g_ref, tab_ref, o_ref) = refs[:nx], refs[nx:]
        lo, hi = _head_tiles(x_refs)
        r = _head_rstd(lo, hi)
        hi = hi * r * g_ref[:, QK_NOPE:]
        hi = (hi * tab_ref[:, :LANE] + pltpu.roll(hi, half, 1) * tab_ref[:, LANE:2 * LANE]
              + pltpu.roll(hi, LANE - half, 1) * tab_ref[:, 2 * LANE:])
        o_ref[:, :QK_NOPE] = (lo * r * g_ref[:, :QK_NOPE]).astype(BF16)
        o_ref[:, QK_NOPE:] = hi.astype(BF16)

    return pl.pallas_call(
        body, name=name, grid=(t // ts, MLA_HEADS),
        in_specs=_head_specs(ts, kpe) + [pl.BlockSpec((1, QK_PAD), lambda r, h: (0, 0)),
                                         pl.BlockSpec((ts, 3 * LANE), lambda r, h: (r, 0))],
        out_specs=pl.BlockSpec((ts, QK_PAD), lambda r, h: (r, h)),
        out_shape=jax.ShapeDtypeStruct((t, MLA_HEADS * QK_PAD), BF16),
        compiler_params=_params("parallel", "parallel"))(*((x,) if kpe is None else (x, kpe)), g_pad, tabs)


def _qk_norm_bwd(x, kpe, g_pad, tabs, dout, dv, *, name):
    t = x.shape[0]
    ts = _tile(t, 1024, 8)
    half = QK_ROPE // 2
    is_k = kpe is not None
    nx = 2 if is_k else 1
    inv_width = 1.0 / (QK_NOPE + QK_ROPE)

    def body(*refs):
        x_refs, (g_ref, tab_ref, do_ref), rest = refs[:nx], refs[nx:nx + 3], refs[nx + 3:]
        if is_k:
            dv_ref, dx_ref, dkpe_ref, dg_ref = rest
        else:
            dx_ref, dg_ref = rest

        @pl.when((pl.program_id(0) == 0) & (pl.program_id(1) == 0))
        def _():
            dg_ref[...] = jnp.zeros_like(dg_ref)

        lo, hi = _head_tiles(x_refs)
        r = _head_rstd(lo, hi)
        lo, hi = lo * r, hi * r
        dlo, dhi = do_ref[:, :QK_NOPE], do_ref[:, QK_NOPE:]
        dhi = (dhi * tab_ref[:, :LANE] + pltpu.roll(dhi * tab_ref[:, LANE:2 * LANE], LANE - half, 1)
               + pltpu.roll(dhi * tab_ref[:, 2 * LANE:], half, 1))
        dg_ref[:, :QK_NOPE] += jnp.sum(dlo * lo, axis=0, keepdims=True)
        dg_ref[:, QK_NOPE:] += jnp.sum(dhi * hi, axis=0, keepdims=True)
        dlo, dhi = dlo * g_ref[:, :QK_NOPE], dhi * g_ref[:, QK_NOPE:]
        mean = (jnp.sum(dlo * lo, axis=-1, keepdims=True) + jnp.sum(dhi * hi, axis=-1, keepdims=True)) * inv_width
        dx_ref[:, :QK_NOPE] = (r * (dlo - lo * mean)).astype(BF16)
        dxhi = r * (dhi - hi * mean)
        if is_k:
            dx_ref[:, QK_NOPE:] = dv_ref[...].astype(BF16)

            @pl.when(pl.program_id(1) == 0)
            def _():
                dkpe_ref[...] = jnp.zeros_like(dkpe_ref)

            dkpe_ref[...] += dxhi
        else:
            dx_ref[:, QK_NOPE:] = dxhi.astype(BF16)

    head = pl.BlockSpec((ts, QK_PAD), lambda r, h: (r, h))
    vec = pl.BlockSpec((1, QK_PAD), lambda r, h: (0, 0))
    in_specs = _head_specs(ts, kpe) + [vec, pl.BlockSpec((ts, 3 * LANE), lambda r, h: (r, 0)), head]
    ins = [x] + ([kpe] if is_k else []) + [g_pad, tabs, dout]
    out_specs = [head]
    out_shape = [jax.ShapeDtypeStruct((t, MLA_HEADS * QK_PAD), BF16)]
    if is_k:
        ins.append(dv)
        in_specs.append(pl.BlockSpec((ts, V_HEAD), lambda r, h: (r, h)))
        out_specs.append(pl.BlockSpec((ts, LANE), lambda r, h: (r, 0)))
        out_shape.append(jax.ShapeDtypeStruct((t, LANE), F32))
    out_specs.append(vec)
    out_shape.append(jax.ShapeDtypeStruct((1, QK_PAD), F32))
    return pl.pallas_call(body, name=name, grid=(t // ts, MLA_HEADS), in_specs=in_specs, out_specs=out_specs,
                          out_shape=out_shape, compiler_params=_params("arbitrary", "arbitrary"))(*ins)


def _causal_probs(q, k, q0):
    scale = (QK_NOPE + QK_ROPE) ** -0.5
    sc = lax.dot_general(q, k, (((1,), (1,)), ((), ())), preferred_element_type=F32) * scale
    qi = q0 + lax.broadcasted_iota(jnp.int32, sc.shape, 0)
    ki = lax.broadcasted_iota(jnp.int32, sc.shape, 1)
    sc = jnp.where(ki <= qi, sc, -1e30)
    e = jnp.exp(sc - jnp.max(sc, axis=-1, keepdims=True))
    return e / jnp.sum(e, axis=-1, keepdims=True)


def _attn_specs(s, tq):
    nq = s // tq
    q_spec = pl.BlockSpec((tq, QK_PAD), lambda b, h, i: (b * nq + i, h))
    o_spec = pl.BlockSpec((tq, V_HEAD), lambda b, h, i: (b * nq + i, h))
    k_spec = pl.BlockSpec((s, QK_PAD), lambda b, h, i: (b, h))
    v_spec = pl.BlockSpec((s, V_HEAD), lambda b, h, i: (b, 2 * h + 1))
    dv_spec = pl.BlockSpec((s, V_HEAD), lambda b, h, i: (b, h))
    return q_spec, o_spec, k_spec, v_spec, dv_spec


def _attn_fwd(qn, kn, kv_raw, nb, side):
    t = qn.shape[0]
    s = t // nb
    tq = _tile(s, 256, 8)
    nq = s // tq
    nsteps = nb * MLA_HEADS * nq
    ns_in, ns_out = len(side.inputs), len(side.out_shape)
    any_spec = pl.BlockSpec(memory_space=pl.ANY)
    q_spec, o_spec, k_spec, v_spec, _ = _attn_specs(s, tq)

    def body(*refs):
        (q_ref, k_ref, v_ref), sins, (o_ref,), souts, (sem,) = _refs_split(refs, (3, ns_in, 1, ns_out, 1))
        step = (pl.program_id(0) * MLA_HEADS + pl.program_id(1)) * nq + pl.program_id(2)
        side.begin(sins, souts, sem, step, nsteps)

        def block(g):
            keys = (g + 1) * tq
            p = _causal_probs(q_ref[...], k_ref[:keys, :], g * tq)
            o_ref[...] = jnp.dot(p.astype(BF16), v_ref[:keys, :].astype(BF16),
                                 preferred_element_type=F32).astype(BF16)

        for g in range(nq):
            pl.when(pl.program_id(2) == g)(functools.partial(block, g))
        side.end(sins, souts, sem, step, nsteps)

    return pl.pallas_call(
        body, name="attn_fwd", grid=(nb, MLA_HEADS, nq),
        in_specs=[q_spec, k_spec, v_spec] + [any_spec] * ns_in,
        out_specs=[o_spec] + [any_spec] * ns_out,
        out_shape=[jax.ShapeDtypeStruct((t, MLA_HEADS * V_HEAD), BF16)] + list(side.out_shape),
        scratch_shapes=side.scratch(),
        compiler_params=_params("arbitrary", "arbitrary", "arbitrary"))(qn, kn, kv_raw, *side.inputs)


def _attn_bwd(qn, kn, kv_raw, dattn, nb):
    t = qn.shape[0]
    s = t // nb
    tq = _tile(s, 256, 8)
    nq = s // tq
    scale = (QK_NOPE + QK_ROPE) ** -0.5
    q_spec, o_spec, k_spec, v_spec, dv_spec = _attn_specs(s, tq)

    def body(q_ref, k_ref, v_ref, do_ref, dq_ref, dk_ref, dv_ref):
        @pl.when(pl.program_id(2) == 0)
        def _():
            dk_ref[...] = jnp.zeros_like(dk_ref)
            dv_ref[...] = jnp.zeros_like(dv_ref)

        def block(g):
            keys = (g + 1) * tq
            q, k, v, do = q_ref[...], k_ref[:keys, :], v_ref[:keys, :].astype(BF16), do_ref[...]
            p = _causal_probs(q, k, g * tq)
            dv_ref[:keys, :] += lax.dot_general(p.astype(BF16), do, (((0,), (0,)), ((), ())),
                                                preferred_element_type=F32)
            dp = lax.dot_general(do, v, (((1,), (1,)), ((), ())), preferred_element_type=F32)
            ds = (p * (dp - jnp.sum(p * dp, axis=-1, keepdims=True)) * scale).astype(BF16)
            dq_ref[...] = jnp.dot(ds, k, preferred_element_type=F32)
            dk_ref[:keys, :] += lax.dot_general(ds, q, (((0,), (0,)), ((), ())), preferred_element_type=F32)

        for g in range(nq):
            pl.when(pl.program_id(2) == g)(functools.partial(block, g))

    return pl.pallas_call(
        body, name="attn_bwd", grid=(nb, MLA_HEADS, nq),
        in_specs=[q_spec, k_spec, v_spec, o_spec],
        out_specs=[q_spec, k_spec, dv_spec],
        out_shape=[jax.ShapeDtypeStruct((t, MLA_HEADS * QK_PAD), F32),
                   jax.ShapeDtypeStruct((t, MLA_HEADS * QK_PAD), F32),
                   jax.ShapeDtypeStruct((t, MLA_HEADS * V_HEAD), F32)],
        compiler_params=_params("parallel", "parallel", "arbitrary"))(qn, kn, kv_raw, dattn)


def _gatemix_fwd(gpa, gpb, ya, yb):
    t, w = ya.shape
    tr = _tile(t, 128, 8)

    def body(ga_ref, gb_ref, ya_ref, yb_ref, o_ref):
        o_ref[...] = (jax.nn.sigmoid(ga_ref[...]) * ya_ref[...]
                      + jax.nn.sigmoid(gb_ref[...]) * yb_ref[...]).astype(BF16)

    row = pl.BlockSpec((tr, w), lambda i: (i, 0))
    return pl.pallas_call(body, name="gatemix_fwd", grid=(t // tr,), in_specs=[row] * 4, out_specs=row,
                          out_shape=jax.ShapeDtypeStruct((t, w), BF16),
                          compiler_params=_params("parallel"))(gpa, gpb, ya, yb)


def _gatemix_bwd(gpa, gpb, ya, yb, dmix):
    t, w = ya.shape
    tr = _tile(t, 128, 8)

    def body(ga_ref, gb_ref, ya_ref, yb_ref, dm_ref, dya_ref, dyb_ref, dga_ref, dgb_ref):
        dm = dm_ref[...]
        sa = jax.nn.sigmoid(ga_ref[...])
        sb = jax.nn.sigmoid(gb_ref[...])
        dya_ref[...] = (dm * sa).astype(BF16)
        dyb_ref[...] = (dm * sb).astype(BF16)
        dga_ref[...] = (dm * ya_ref[...] * sa * (1.0 - sa)).astype(BF16)
        dgb_ref[...] = (dm * yb_ref[...] * sb * (1.0 - sb)).astype(BF16)

    row = pl.BlockSpec((tr, w), lambda i: (i, 0))
    return pl.pallas_call(body, name="gatemix_bwd", grid=(t // tr,), in_specs=[row] * 5, out_specs=[row] * 4,
                          out_shape=[jax.ShapeDtypeStruct((t, w), BF16)] * 4,
                          compiler_params=_params("parallel"))(gpa, gpb, ya, yb, dmix)


def _final(x1, f2, gate, target, seq):
    t, w = x1.shape
    nex = t // seq
    tr = _row_tile(seq, w, 1024 * 1024)
    nb = seq // tr

    def body(x_ref, f_ref, gate_ref, t_ref, loss_ref, dy_ref, dff_ref, dgate_ref):
        i = pl.program_id(0)

        @pl.when(i == 0)
        def _():
            loss_ref[...] = jnp.zeros_like(loss_ref)

        @pl.when(i % nb == 0)
        def _():
            dgate_ref[...] = jnp.zeros_like(dgate_ref)

        fv = f_ref[...]
        gv = gate_ref[0]
        err = x_ref[...] + gv * fv - t_ref[...]
        sq = jnp.sum(err * err, axis=1, keepdims=True)
        loss_ref[...] += jnp.sum(sq, axis=0, keepdims=True) * (0.5 / w)
        dy = err * (1.0 / w)
        dy_ref[...] = dy
        dff_ref[...] = (dy * gv).astype(BF16)
        dgate_ref[0] += jnp.sum(dy * fv, axis=0, keepdims=True)

    row = pl.BlockSpec((tr, w), lambda i: (i, 0))
    per_ex = pl.BlockSpec((1, 1, w), lambda i: (i // nb, 0, 0))
    return pl.pallas_call(
        body, name="loss_head", grid=(t // tr,),
        in_specs=[row, row, per_ex, row],
        out_specs=[pl.BlockSpec((1, LANE), lambda i: (0, 0)), row, row, per_ex],
        out_shape=[jax.ShapeDtypeStruct((1, LANE), F32), jax.ShapeDtypeStruct((t, w), F32),
                   jax.ShapeDtypeStruct((t, w), BF16), jax.ShapeDtypeStruct((nex, 1, w), F32)],
        compiler_params=_params("arbitrary"))(x1, f2, gate, target)


def _sum_slots(parts, *, name):
    _, r, c = parts[0].shape
    tr, tc = _tile(r, 256, 16), _tile(c, 1024, LANE)

    def body(*refs):
        acc = None
        for p_ref in refs[:-1]:
            for j in range(p_ref.shape[0]):
                val = p_ref[j].astype(F32)
                acc = val if acc is None else acc + val
        refs[-1][...] = acc

    return pl.pallas_call(body, name=name, grid=(r // tr, c // tc),
                          in_specs=[pl.BlockSpec((p.shape[0], tr, tc), lambda i, j: (0, i, j)) for p in parts],
                          out_specs=pl.BlockSpec((tr, tc), lambda i, j: (i, j)),
                          out_shape=jax.ShapeDtypeStruct((r, c), F32),
                          compiler_params=_params("parallel", "parallel"))(*parts)


def _adamw(w, m, v, parts, *, name):
    _, r, c = w.shape
    parts = [p.reshape(w.shape) for p in parts]
    tc = _tile(c, 1024, LANE) if c % LANE == 0 else c
    tr = _tile(r, max(8, (256 * 1024) // tc // 8 * 8), 8)
    npart = len(parts)
    b1c = 1.0 - ADAM_B1 ** ADAM_STEP
    b2c = 1.0 - ADAM_B2 ** ADAM_STEP

    def body(*refs):
        w_ref, m_ref, v_ref = refs[:3]
        g_ref, d_ref, mo_ref, vo_ref = refs[3 + npart:]
        g = refs[3][...].astype(F32)
        for p_ref in refs[4:3 + npart]:
            g = g + p_ref[...].astype(F32)
        m2 = ADAM_B1 * m_ref[...] + (1.0 - ADAM_B1) * g
        v2 = ADAM_B2 * v_ref[...] + (1.0 - ADAM_B2) * jnp.square(g)
        m_hat = m2 / b1c
        v_hat = v2 / b2c
        g_ref[...] = g
        d_ref[...] = -ADAM_LR * (m_hat / (jnp.sqrt(v_hat) + ADAM_EPS) + ADAM_WD * w_ref[...])
        mo_ref[...] = m2
        vo_ref[...] = v2

    blk = pl.BlockSpec((1, tr, tc), lambda i, j: (0, i, j))
    return pl.pallas_call(body, name=name, grid=(r // tr, c // tc),
                          in_specs=[blk] * (3 + npart), out_specs=[blk] * 4,
                          out_shape=[jax.ShapeDtypeStruct((1, r, c), F32)] * 4,
                          compiler_params=_params("parallel", "parallel"))(w, m, v, *parts)


def _allgather8(x, *, name):
    def body(x_ref, o_ref, ssem, rsem):
        pos = _mesh_pos()
        me = 4 * pos[0] + 2 * pos[1] + pos[2]
        o_ref[me] = x_ref[...]
        cps = []
        for k in range(1, N_DEV):
            cp = pltpu.make_async_remote_copy(src_ref=x_ref, dst_ref=o_ref.at[me], send_sem=ssem.at[k - 1],
                                              recv_sem=rsem.at[k - 1], device_id=_flip(pos, k), device_id_type=MESH)
            cp.start()
            cps.append(cp)
        for cp in cps:
            cp.wait()

    return pl.pallas_call(
        body, name=name,
        out_shape=jax.ShapeDtypeStruct((N_DEV,) + x.shape, x.dtype),
        in_specs=[pl.BlockSpec(memory_space=pltpu.VMEM)],
        out_specs=pl.BlockSpec(memory_space=pltpu.VMEM),
        scratch_shapes=[pltpu.SemaphoreType.DMA((N_DEV - 1,)), pltpu.SemaphoreType.DMA((N_DEV - 1,))],
        compiler_params=pltpu.CompilerParams(vmem_limit_bytes=VMEM_LIMIT),
    )(x)


def _sibling_swap(p, *, name):
    def body(p_ref, o_ref, ssem, rsem):
        cp = pltpu.make_async_remote_copy(src_ref=p_ref, dst_ref=o_ref, send_sem=ssem, recv_sem=rsem,
                                          device_id=_flip(_mesh_pos(), 1), device_id_type=MESH)
        cp.start()
        cp.wait()

    return pl.pallas_call(
        body, name=name,
        out_shape=jax.ShapeDtypeStruct(p.shape, p.dtype),
        in_specs=[pl.BlockSpec(memory_space=pl.ANY)],
        out_specs=pl.BlockSpec(memory_space=pl.ANY),
        scratch_shapes=[pltpu.SemaphoreType.DMA, pltpu.SemaphoreType.DMA],
    )(p)


_SMALL = ("b_ada", "g_norm1", "g_v", "w_s", "b_s", "g_q_lat", "g_kv_lat", "g_qn", "g_kn", "g_norm2")
_BIG = ("w_in", "w_uq", "w_ukv", "w_branch_a", "w_branch_b", "w_out", "w_ff1", "w_ff2")
_SHARD_AXIS = {"w_in": 1, "w_uq": 1, "w_ukv": 1, "w_branch_a": 1, "w_branch_b": 0, "w_out": 0, "w_ff1": 1, "w_ff2": 0}
_WEIGHTS = ("w_ada", "b_ada", "g_norm1", "w_in", "g_v", "w_s", "b_s", "g_q_lat", "g_kv_lat", "w_uq", "w_ukv",
            "g_qn", "g_kn", "w_branch_a", "w_branch_b", "w_out", "g_norm2", "w_ff1", "w_ff2")


def kernel(x, c, positions, w_ada, b_ada, g_norm1, w_in, g_v, w_s, b_s, g_q_lat, g_kv_lat, w_uq, w_ukv, g_qn, g_kn, w_branch_a, w_branch_b, w_out, g_norm2, w_ff1, w_ff2, loss_target, m_w_ada, m_b_ada, m_g_norm1, m_w_in, m_g_v, m_w_s, m_b_s, m_g_q_lat, m_g_kv_lat, m_w_uq, m_w_ukv, m_g_qn, m_g_kn, m_w_branch_a, m_w_branch_b, m_w_out, m_g_norm2, m_w_ff1, m_w_ff2, v_w_ada, v_b_ada, v_g_norm1, v_w_in, v_g_v, v_w_s, v_b_s, v_g_q_lat, v_g_kv_lat, v_w_uq, v_w_ukv, v_g_qn, v_g_kn, v_w_branch_a, v_w_branch_b, v_w_out, v_g_norm2, v_w_ff1, v_w_ff2):
    wts = dict(w_ada=w_ada, b_ada=b_ada, g_norm1=g_norm1, w_in=w_in, g_v=g_v, w_s=w_s, b_s=b_s, g_q_lat=g_q_lat,
               g_kv_lat=g_kv_lat, w_uq=w_uq, w_ukv=w_ukv, g_qn=g_qn, g_kn=g_kn, w_branch_a=w_branch_a,
               w_branch_b=w_branch_b, w_out=w_out, g_norm2=g_norm2, w_ff1=w_ff1, w_ff2=w_ff2)
    mom1 = dict(w_ada=m_w_ada, b_ada=m_b_ada, g_norm1=m_g_norm1, w_in=m_w_in, g_v=m_g_v, w_s=m_w_s, b_s=m_b_s,
                g_q_lat=m_g_q_lat, g_kv_lat=m_g_kv_lat, w_uq=m_w_uq, w_ukv=m_w_ukv, g_qn=m_g_qn, g_kn=m_g_kn,
                w_branch_a=m_w_branch_a, w_branch_b=m_w_branch_b, w_out=m_w_out, g_norm2=m_g_norm2,
                w_ff1=m_w_ff1, w_ff2=m_w_ff2)
    mom2 = dict(w_ada=v_w_ada, b_ada=v_b_ada, g_norm1=v_g_norm1, w_in=v_w_in, g_v=v_g_v, w_s=v_w_s, b_s=v_b_s,
                g_q_lat=v_g_q_lat, g_kv_lat=v_g_kv_lat, w_uq=v_w_uq, w_ukv=v_w_ukv, g_qn=v_g_qn, g_kn=v_g_kn,
                w_branch_a=v_w_branch_a, w_branch_b=v_w_branch_b, w_out=v_w_out, g_norm2=v_g_norm2,
                w_ff1=v_w_ff1, w_ff2=v_w_ff2)

    nb, seq, dm = x.shape
    t = nb * seq
    nh, qkh = MLA_HEADS, QK_NOPE + QK_ROPE
    bh = nb * nh
    off_q = 2 * GM_WIDTH
    off_kv = off_q + Q_LORA
    off_kpe = off_kv + KV_LORA
    off_gate = off_kpe + QK_ROPE
    in_cols = off_gate + 2 * dm
    ins = in_cols // N_CHIP
    insp = _round_up(ins, LANE)
    ada_cols = N_MOD * dm // N_CHIP

    ix, iy, ic = _mesh_pos()
    me = 4 * ix + 2 * iy + ic
    chip = 2 * ix + iy
    xf = x.reshape(t, dm)
    tgt = loss_target.reshape(t, dm)

    c_all = _allgather8(c, name="ag_cond").reshape(N_DEV * nb, dm)
    cond = _silu(c_all)
    b_sh = lax.dynamic_slice(b_ada, (0, chip * ada_cols), (1, ada_cols))
    mod_sh = _mm(cond, w_ada[0], name="ada_fwd", tn=2048, tk=512) + b_sh
    mod8 = _allgather8(mod_sh, name="ag_mod")
    mod_all = jnp.concatenate([mod8[2 * s] for s in range(N_CHIP)], axis=1)
    mod = lax.dynamic_slice(mod_all, (nb * me, 0), (nb, N_MOD * dm))
    sh1, sc1, ga1, sh2, sc2, ga2 = [mod[:, j * dm:(j + 1) * dm].reshape(nb, 1, dm) for j in range(N_MOD)]

    shards = {n: wts[n][0].astype(BF16) for n in _BIG}
    shards["w_in"] = jnp.pad(shards["w_in"], ((0, 0), (0, insp - ins)))

    def gathers(*names):
        return _Side([_Gather(shards[n], _SHARD_AXIS[n]) for n in names])

    full = {}
    (full["w_in"],) = _comm_call(gathers("w_in"), name="ag_w_in")
    h1 = _norm_mod_fwd(xf, g_norm1, sh1, sc1, seq, name="norm1_fwd")
    early = ("w_branch_a", "w_uq", "w_ukv", "w_branch_b", "w_out")
    proj_p, *got = _mm(h1, full["w_in"], name="in_fwd", side=gathers(*early))
    full.update(zip(early, got))

    def shard_pieces(lo, hi):
        out = []
        for s in range(N_CHIP):
            a, b = max(lo, s * ins), min(hi, (s + 1) * ins)
            if a < b:
                out.append((s, a - s * ins, b - a))
        return out

    def seg(lo, hi):
        parts = [proj_p[:, s * insp + a:s * insp + a + w] for s, a, w in shard_pieces(lo, hi)]
        return parts[0] if len(parts) == 1 else jnp.concatenate(parts, axis=1)

    bounds = (0, off_q, off_kv, off_kpe, off_gate, off_gate + dm, in_cols)
    uv, q_lat, kv_lat, k_pe, gpa, gpb = [seg(lo, hi) for lo, hi in zip(bounds[:-1], bounds[1:])]

    b_col = b_s[0].reshape(GM_GROUPS, CHUNK, 1)
    a_out = _gmlp_fwd(uv, g_v, w_s[0], b_col)
    y_a = _mm(a_out, full["w_branch_a"], name="ba_fwd")

    zq = jnp.zeros((nb, 1, Q_LORA), F32)
    zkv = jnp.zeros((nb, 1, KV_LORA), F32)
    ql = _norm_mod_fwd(q_lat, g_q_lat, zq, zq, seq, name="qlat_norm_fwd")
    kvl = _norm_mod_fwd(kv_lat, g_kv_lat, zkv, zkv, seq, name="kvlat_norm_fwd")
    w_uq_p = jnp.pad(full["w_uq"].reshape(Q_LORA, nh, qkh), ((0, 0), (0, 0), (0, QK_PAD - qkh)))
    w_uq_p = w_uq_p.reshape(Q_LORA, nh * QK_PAD)
    q_raw = _mm(ql, w_uq_p, name="uq_fwd")
    kv_raw = _mm(kvl, full["w_ukv"], name="ukv_fwd")
    kpe_p = jnp.pad(k_pe, ((0, 0), (0, LANE - QK_ROPE)))
    pos = positions.astype(F32).reshape(nb, seq, 1)
    inv_freq = 1.0 / (ROPE_THETA ** (jnp.arange(0, QK_ROPE, 2, dtype=F32) / QK_ROPE))
    invf = jnp.concatenate([inv_freq, inv_freq, jnp.zeros((LANE - QK_ROPE,), F32)]).reshape(1, LANE)
    gq_pad = jnp.pad(g_qn, ((0, 0), (0, QK_PAD - qkh)))
    gk_pad = jnp.pad(g_kn, ((0, 0), (0, QK_PAD - qkh)))
    tabs = _rope_tables(pos, invf).reshape(t, 3 * LANE)
    qn = _qk_norm_fwd(q_raw, None, gq_pad, tabs, name="qnorm_fwd")
    kn = _qk_norm_fwd(kv_raw, kpe_p, gk_pad, tabs, name="knorm_fwd")
    attn, full["w_ff1"] = _attn_fwd(qn, kn, kv_raw, nb, gathers("w_ff1"))
    y_b = _mm(attn, full["w_branch_b"], name="bb_fwd")

    mixed = _gatemix_fwd(gpa, gpb, y_a, y_b)
    o1 = _mm(mixed, full["w_out"], name="out_fwd")
    x1, h2 = _norm_mod_fwd(xf, g_norm2, sh2, sc2, seq, name="norm2_fwd", resid=(o1, ga1))
    f1, act, full["w_ff2"] = _mm(h2, full["w_ff1"], name="ff1_fwd", out_dtypes=(BF16, BF16),
                                 epilogue=_relu2_epilogue, side=gathers("w_ff2"))
    f2 = _mm(act, full["w_ff2"], name="ff2_fwd")
    loss_acc, dy, dff, dga2 = _final(x1, f2, ga2, tgt, seq)
    loss = lax.psum(loss_acc[0, 0], ("x", "y", "c"))

    sent = {}

    def scatter_start(n, g):
        sent[n], token = _scatter_start(g, _SHARD_AXIS[n], name="rs_start_" + n)
        return token

    tok = scatter_start("w_ff2", _mm(act, dff, name="ff2_dw", ta=True, out_dtypes=(BF16,)))
    df1 = _mm(dff, full["w_ff2"], name="ff2_dx", tb=True, out_dtypes=(BF16,), epilogue=_relu2_bwd_epilogue,
              extras=(f1,), after=(tok,))
    tok = scatter_start("w_ff1", _mm(h2, df1, name="ff1_dw", ta=True, out_dtypes=(BF16,)))
    dh2 = _mm(df1, full["w_ff1"], name="ff1_dx", tb=True, after=(tok,))
    dx1, dsh2, dsc2, dg_norm2, do1, dga1 = _norm_mod_bwd(x1, g_norm2, sc2, dh2, seq, name="norm2_bwd", dres=dy,
                                                         gate_o=(o1, ga1))
    tok = scatter_start("w_out", _mm(mixed, do1, name="out_dw", ta=True, out_dtypes=(BF16,)))
    dmixed = _mm(do1, full["w_out"], name="out_dx", tb=True, after=(tok,))
    dya, dyb, dgpa, dgpb = _gatemix_bwd(gpa, gpb, y_a, y_b, dmixed)

    tok = scatter_start("w_branch_b", _mm(attn, dyb, name="bb_dw", ta=True, out_dtypes=(BF16,)))
    dattn = _mm(dyb, full["w_branch_b"], name="bb_dx", tb=True, out_dtypes=(BF16,), after=(tok,))
    dqn, dkn, dv = _attn_bwd(qn, kn, kv_raw, dattn, nb)
    dq_raw, dg_qn = _qk_norm_bwd(q_raw, None, gq_pad, tabs, dqn, None, name="qnorm_bwd")
    dkv_raw, dkpe_p, dg_kn = _qk_norm_bwd(kv_raw, kpe_p, gk_pad, tabs, dkn, dv, name="knorm_bwd")
    dk_pe = dkpe_p[:, :QK_ROPE]
    gw_uq_p = _mm(ql, dq_raw, name="uq_dw", ta=True, out_dtypes=(BF16,))
    tok = scatter_start("w_uq", gw_uq_p.reshape(Q_LORA, nh, QK_PAD)[:, :, :qkh].reshape(Q_LORA, nh * qkh))
    dql = _mm(dq_raw, w_uq_p, name="uq_dx", tb=True, after=(tok,))
    tok = scatter_start("w_ukv", _mm(kvl, dkv_raw, name="ukv_dw", ta=True, out_dtypes=(BF16,)))
    dkvl = _mm(dkv_raw, full["w_ukv"], name="ukv_dx", tb=True, after=(tok,))
    dq_lat, _, _, dg_q_lat = _norm_mod_bwd(q_lat, g_q_lat, zq, dql, seq, name="qlat_norm_bwd")
    dkv_lat, _, _, dg_kv_lat = _norm_mod_bwd(kv_lat, g_kv_lat, zkv, dkvl, seq, name="kvlat_norm_bwd")

    tok = scatter_start("w_branch_a", _mm(a_out, dya, name="ba_dw", ta=True, out_dtypes=(BF16,)))
    da = _mm(dya, full["w_branch_a"], name="ba_dx", tb=True, after=(tok,))
    duv, dg_v, dw_s, db_col = _gmlp_bwd(uv, g_v, w_s[0], b_col, da)

    dsegs = (duv, dq_lat.astype(BF16), dkv_lat.astype(BF16), dk_pe.astype(BF16), dgpa, dgpb)
    by_shard = [[] for _ in range(N_CHIP)]
    for dseg, lo, hi in zip(dsegs, bounds[:-1], bounds[1:]):
        for s, a, w in shard_pieces(lo, hi):
            by_shard[s].append(dseg[:, s * ins + a - lo:s * ins + a - lo + w])
    dproj_p = jnp.concatenate([p for s in range(N_CHIP) for p in by_shard[s] + [jnp.zeros((t, insp - ins), BF16)]],
                              axis=1)
    tok = scatter_start("w_in", _mm(h1, dproj_p, name="in_dw", ta=True, out_dtypes=(BF16,)))
    dh1 = _mm(dproj_p, full["w_in"], name="in_dx", tb=True, after=(tok,))
    dx, dsh1, dsc1, dg_norm1 = _norm_mod_bwd(xf, g_norm1, sc1, dh1, seq, name="norm1_bwd", dres=dx1)
    grad_x = dx.reshape(nb, seq, dm)

    gmod = jnp.concatenate([dsh1, dsc1, dga1, dsh2, dsc2, dga2], axis=-1).reshape(nb, N_MOD * dm)
    gmod_all = _allgather8(gmod, name="ag_gmod").reshape(N_DEV * nb, N_MOD * dm)
    gmod_sh = lax.dynamic_slice(gmod_all, (0, chip * ada_cols), (N_DEV * nb, ada_cols))
    g_w_ada = _mm(cond, gmod_sh, name="ada_dw", ta=True, tn=2048)

    out_g, out_d, out_m, out_v = {}, {}, {}, {}
    out_g["w_ada"], out_d["w_ada"], out_m["w_ada"], out_v["w_ada"] = _adamw(
        w_ada, m_w_ada, v_w_ada, [g_w_ada], name="adamw_w_ada")
    late = out_g["w_ada"]
    for n in ("w_ff2", "w_ff1", "w_out", "w_branch_b", "w_uq", "w_ukv", "w_branch_a", "w_in"):
        part = _sum_slots([_scatter_wait(sent[n], late, name="rs_wait_" + n)], name="sum4_" + n)
        other = _sibling_swap(part, name="swap_" + n)
        if n == "w_in":
            part, other = part[:, :ins], other[:, :ins]
        out_g[n], out_d[n], out_m[n], out_v[n] = _adamw(wts[n], mom1[n], mom2[n], [part, other], name="adamw_" + n)
        late = out_g[n]

    small_g = dict(b_ada=jnp.sum(gmod, axis=0), g_norm1=dg_norm1, g_v=dg_v, w_s=dw_s, b_s=db_col,
                   g_q_lat=dg_q_lat, g_kv_lat=dg_kv_lat, g_qn=dg_qn[:, :qkh], g_kn=dg_kn[:, :qkh],
                   g_norm2=dg_norm2)

    def pack(d):
        flat = jnp.concatenate([d[n].reshape(-1) for n in _SMALL])
        return jnp.pad(flat, (0, _round_up(flat.shape[0], 8 * LANE) - flat.shape[0])).reshape(-1, LANE)

    g8 = _allgather8(pack(small_g), name="ag_small_grads")
    sg, sd, sm, sv = _adamw(pack({n: wts[n] for n in _SMALL})[None], pack({n: mom1[n] for n in _SMALL})[None],
                            pack({n: mom2[n] for n in _SMALL})[None], [g8[d] for d in range(N_DEV)],
                            name="adamw_small")
    off = 0
    for n in _SMALL:
        size = wts[n].size
        for dst, src in ((out_g, sg), (out_d, sd), (out_m, sm), (out_v, sv)):
            dst[n] = src.reshape(-1)[off:off + size].reshape(wts[n].shape)
        off += size

    def shaped(d, n):
        return d[n].reshape(wts[n].shape)

    return (loss, grad_x, *[shaped(out_g, n) for n in _WEIGHTS], *[shaped(out_d, n) for n in _WEIGHTS],
            *[shaped(out_m, n) for n in _WEIGHTS], *[shaped(out_v, n) for n in _WEIGHTS])
```

```python
import functools
import math

import jax
import jax.numpy as jnp
from jax import lax
from jax.experimental import pallas as pl
from jax.experimental.pallas import tpu as pltpu

F32 = jnp.float32
BF16 = jnp.bfloat16

GM_WIDTH = 2048
GM_GROUPS = 8
CHUNK = 128
MLA_HEADS = 32
QK_NOPE = 128
QK_ROPE = 64
V_HEAD = 128
Q_LORA = 1024
KV_LORA = 512
ROPE_THETA = 10000.0
N_MOD = 6
EPS = 1e-6
ADAM_LR = 0.001
ADAM_B1 = 0.9
ADAM_B2 = 0.999
ADAM_EPS = 1e-08
ADAM_WD = 0.01
ADAM_STEP = 10

N_CHIP = 4
N_DEV = 8
LANE = 128
QK_PAD = 256
VMEM_LIMIT = 56 * 1024 * 1024
MESH = pl.DeviceIdType.MESH


def _round_up(n, m):
    return (n + m - 1) // m * m


def _tile(n, target, align):
    t = min(target, n) // align * align
    while t >= align:
        if n % t == 0:
            return t
        t -= align
    return n


def _params(*sem):
    return pltpu.CompilerParams(dimension_semantics=sem, vmem_limit_bytes=VMEM_LIMIT)


def _mesh_pos():
    return lax.axis_index("x"), lax.axis_index("y"), lax.axis_index("c")


def _flip(pos, k):
    ix, iy, ic = pos
    return (1 - ix if k & 4 else ix, 1 - iy if k & 2 else iy, 1 - ic if k & 1 else ic)


def _chip_of(pos):
    return 2 * pos[0] + pos[1]


def _window(ref, axis, chip, size):
    start = pl.multiple_of(chip * size, LANE if axis == 1 else 16)
    if axis == 1:
        return ref.at[:, pl.ds(start, size)]
    return ref.at[pl.ds(start, size), :]


def _remote(src, dst, sem, send, recv, peer):
    return pltpu.make_async_remote_copy(src_ref=src, dst_ref=dst, send_sem=sem.at[send], recv_sem=sem.at[recv],
                                        device_id=peer, device_id_type=MESH)


class _Gather:
    n_sem = 13

    def __init__(self, wsh, axis):
        self.axis, self.size, self.rows = axis, wsh.shape[axis], wsh.shape[0]
        self.inputs = [wsh]
        self.out_shape = [jax.ShapeDtypeStruct(
            tuple(d * N_CHIP if a == axis else d for a, d in enumerate(wsh.shape)), wsh.dtype)]

    def _half(self, ref, chip, c, local=False):
        h = self.rows // 2
        if self.axis == 1:
            rows = pl.ds(pl.multiple_of(c * h, 16), h)
            return ref.at[rows, :] if local else ref.at[rows, pl.ds(pl.multiple_of(chip * self.size, LANE), self.size)]
        base = 0 if local else chip * self.size
        return ref.at[pl.ds(pl.multiple_of(base + c * h, 16), h), :]

    def start(self, ins, outs, sem, s0):
        (w_ref,), (o_ref,) = ins, outs
        pos = _mesh_pos()
        chip, c = _chip_of(pos), pos[2]
        pltpu.make_async_copy(w_ref, _window(o_ref, self.axis, chip, self.size), sem.at[s0]).start()
        for j in range(1, N_CHIP):
            _remote(self._half(w_ref, chip, c, local=True), self._half(o_ref, chip, c), sem, s0 + j, s0 + 3 + j,
                    _flip(pos, 2 * j)).start()

    def _landed(self, outs, j):
        pos = _mesh_pos()
        peer = _flip(pos, 2 * j)
        return self._half(outs[0], _chip_of(peer), pos[2]), peer

    def arrive(self, ins, outs, sem, s0):
        for j in range(1, N_CHIP):
            blk, peer = self._landed(outs, j)
            _remote(blk, blk, sem, s0 + j, s0 + 3 + j, peer).wait_recv()

    def forward(self, ins, outs, sem, s0):
        for j in range(1, N_CHIP):
            blk, _ = self._landed(outs, j)
            _remote(blk, blk, sem, s0 + 6 + j, s0 + 9 + j, _flip(_mesh_pos(), 1)).start()

    def relay(self, ins, outs, sem, s0):
        for j in range(1, N_CHIP):
            blk, peer = self._landed(outs, j)
            _remote(blk, blk, sem, s0 + j, s0 + 3 + j, peer).wait_recv()
            _remote(blk, blk, sem, s0 + 6 + j, s0 + 9 + j, _flip(_mesh_pos(), 1)).start()

    def finish(self, ins, outs, sem, s0):
        (w_ref,), (o_ref,) = ins, outs
        pos = _mesh_pos()
        chip, c = _chip_of(pos), pos[2]
        sibling = _flip(pos, 1)
        for j in range(1, N_CHIP):
            peer = _flip(pos, 2 * j)
            _remote(self._half(w_ref, chip, c, local=True), self._half(o_ref, chip, c), sem, s0 + j, s0 + 3 + j,
                    peer).wait_send()
            blk = self._half(o_ref, _chip_of(peer), c)
            _remote(blk, blk, sem, s0 + 6 + j, s0 + 9 + j, sibling).wait_send()
            got = self._half(o_ref, _chip_of(peer), 1 - c)
            _remote(got, got, sem, s0 + 6 + j, s0 + 9 + j, sibling).wait_recv()
        pltpu.make_async_copy(w_ref, _window(o_ref, self.axis, chip, self.size), sem.at[s0]).wait()


class _Scatter:
    def __init__(self, g, axis, peers, own):
        self.axis, self.size, self.peers, self.own = axis, g.shape[axis] // N_CHIP, tuple(peers), int(own)
        shard = tuple(self.size if a == axis else d for a, d in enumerate(g.shape))
        self.n_sem = self.own + 2 * len(self.peers)
        self.inputs = [g]
        self.out_shape = [jax.ShapeDtypeStruct((self.own + len(self.peers),) + shard, g.dtype)]

    def _copies(self, ins, outs, sem, s0):
        (g_ref,), (o_ref,) = ins, outs
        pos = _mesh_pos()
        cps = []
        if self.own:
            cps.append(pltpu.make_async_copy(_window(g_ref, self.axis, _chip_of(pos), self.size), o_ref.at[0],
                                             sem.at[s0]))
        for n, j in enumerate(self.peers):
            peer = _flip(pos, 2 * j)
            cps.append(_remote(_window(g_ref, self.axis, _chip_of(peer), self.size), o_ref.at[self.own + n], sem,
                               s0 + self.own + 2 * n, s0 + self.own + 2 * n + 1, peer))
        return cps

    def start(self, ins, outs, sem, s0):
        for cp in self._copies(ins, outs, sem, s0):
            cp.start()

    def relay(self, ins, outs, sem, s0):
        pass

    def finish(self, ins, outs, sem, s0):
        for cp in self._copies(ins, outs, sem, s0):
            cp.wait()


class _Side:
    def __init__(self, jobs):
        self.jobs = list(jobs)
        self.inputs = [a for job in self.jobs for a in job.inputs]
        self.out_shape = [s for job in self.jobs for s in job.out_shape]

    def scratch(self):
        return [pltpu.SemaphoreType.DMA((sum(job.n_sem for job in self.jobs),))]

    def _run(self, phase, ins, outs, sem):
        i = o = s = 0
        for job in self.jobs:
            ni, no = len(job.inputs), len(job.out_shape)
            getattr(job, phase)(ins[i:i + ni], outs[o:o + no], sem, s)
            i, o, s = i + ni, o + no, s + job.n_sem

    def begin(self, ins, outs, sem, step, nsteps):
        if nsteps == 1:
            self._run("start", ins, outs, sem)
        else:
            pl.when(step == 0)(lambda: self._run("start", ins, outs, sem))

    def end(self, ins, outs, sem, step, nsteps):
        if nsteps == 1:
            self._run("relay", ins, outs, sem)
            self._run("finish", ins, outs, sem)
        else:
            pl.when(step == nsteps - max(1, nsteps // 8))(lambda: self._run("relay", ins, outs, sem))
            pl.when(step == nsteps - 1)(lambda: self._run("finish", ins, outs, sem))


def _comm_call(side, *, name):
    n_in = len(side.inputs)

    def body(*refs):
        ins, outs, sem = refs[:n_in], refs[n_in:-1], refs[-1]
        side.begin(ins, outs, sem, 0, 1)
        side.end(ins, outs, sem, 0, 1)

    any_spec = pl.BlockSpec(memory_space=pl.ANY)
    return pl.pallas_call(body, name=name, out_shape=list(side.out_shape), in_specs=[any_spec] * n_in,
                          out_specs=[any_spec] * len(side.out_shape), scratch_shapes=side.scratch())(*side.inputs)


_HBM = pl.BlockSpec(memory_space=pltpu.HBM)
_SEM = pl.BlockSpec(memory_space=pltpu.SEMAPHORE)
_DATAFLOW = pltpu.SideEffectType.DATAFLOW_SIDE_EFFECTING


def _split_start(side, *, name, after=()):
    after = tuple(after)
    ni, no = len(side.inputs), len(side.out_shape)
    bufs = list(side.inputs) + [lax.empty(s.shape, s.dtype) for s in side.out_shape]

    def body(*refs):
        side._run("start", refs[:ni], refs[ni:ni + no], refs[ni + no + len(after)])
        refs[-1][...] = jnp.zeros_like(refs[-1])

    res = pl.pallas_call(
        body, name=name,
        out_shape=[side.scratch()[0]] + [pltpu.HBM(b.shape, b.dtype) for b in bufs]
        + [jax.ShapeDtypeStruct((8, LANE), F32)],
        in_specs=[_HBM] * (ni + no) + [pl.BlockSpec(memory_space=pl.ANY)] * len(after),
        out_specs=[_SEM] + [_HBM] * (ni + no) + [pl.BlockSpec(memory_space=pltpu.VMEM)],
        input_output_aliases={i: 1 + i for i in range(ni + no)},
        compiler_params=pltpu.CompilerParams(has_side_effects=_DATAFLOW),
    )(*[pltpu.with_memory_space_constraint(b, pltpu.HBM) for b in bufs], *after)
    return (side, res[0], list(res[1:-1])), res[-1]


def _split_step(state, phase, *, name, after=()):
    side, sem, bufs = state
    after = tuple(after)
    ni, n = len(side.inputs), len(bufs)

    def body(*refs):
        side._run(phase, refs[:ni], refs[ni:n], refs[n])

    res = pl.pallas_call(
        body, name=name,
        out_shape=[pltpu.HBM(b.shape, b.dtype) for b in bufs],
        in_specs=[_HBM] * n + [_SEM] + [pl.BlockSpec(memory_space=pl.ANY)] * len(after),
        out_specs=[_HBM] * n,
        input_output_aliases={i: i for i in range(n)},
        compiler_params=pltpu.CompilerParams(has_side_effects=_DATAFLOW),
    )(*bufs, sem, *after)
    return (side, sem, list(res))


def _split_results(state):
    side, _, bufs = state
    return bufs[len(side.inputs):]


def _refs_split(refs, counts):
    out, p = [], 0
    for n in counts:
        out.append(refs[p:p + n])
        p += n
    return out


def _mm(a, b, *, name, ta=False, tb=False, out_dtypes=(F32,), epilogue=None, extras=(),
        tm=1024, tn=1024, tk=1024, m_rows=None, m_start=0, side=None, after=()):
    after = tuple(after)
    m, k = (a.shape[1], a.shape[0]) if ta else a.shape
    n = b.shape[0] if tb else b.shape[1]
    assert k == (b.shape[1] if tb else b.shape[0]), (a.shape, b.shape)
    m = m if m_rows is None else m_rows
    tm = _tile(m, tm, LANE if ta else 16)
    tn = _tile(n, tn, LANE)
    tk = _tile(k, tk, LANE)
    assert m_start % tm == 0
    mo = m_start // tm
    gi, gj, nk = m // tm, n // tn, k // tk
    nsteps = gi * gj * nk
    ne, no = len(extras), len(out_dtypes)
    ns_in, ns_out = (len(side.inputs), len(side.out_shape)) if side else (0, 0)
    dims = (((0 if ta else 1,), (1 if tb else 0,)), ((), ()))

    def body(*refs):
        (a_ref, b_ref), ex, sins, _, outs, souts, (acc,), sems = _refs_split(
            refs, (2, ne, ns_in, len(after), no, ns_out, 1, 1 if side else 0))
        kk = pl.program_id(2)
        step = (pl.program_id(0) * gj + pl.program_id(1)) * nk + kk
        if side:
            side.begin(sins, souts, sems[0], step, nsteps)

        def prod():
            return lax.dot_general(a_ref[...].astype(BF16), b_ref[...].astype(BF16), dims,
                                   preferred_element_type=F32)

        def emit(r):
            res = epilogue(r, *[e[...] for e in ex]) if epilogue is not None else (r,)
            for o, val in zip(outs, res):
                o[...] = val.astype(o.dtype)

        if nk == 1:
            emit(prod())
        else:
            @pl.when(kk == 0)
            def _():
                acc[...] = prod()

            @pl.when((kk > 0) & (kk < nk - 1))
            def _():
                acc[...] += prod()

            @pl.when(kk == nk - 1)
            def _():
                emit(acc[...] + prod())

        if side:
            side.end(sins, souts, sems[0], step, nsteps)

    a_spec = (pl.BlockSpec((tk, tm), lambda i, j, q: (q, i + mo)) if ta
              else pl.BlockSpec((tm, tk), lambda i, j, q: (i + mo, q)))
    b_spec = pl.BlockSpec((tn, tk), lambda i, j, q: (j, q)) if tb else pl.BlockSpec((tk, tn), lambda i, j, q: (q, j))
    o_spec = pl.BlockSpec((tm, tn), lambda i, j, q: (i, j))
    any_spec = pl.BlockSpec(memory_space=pl.ANY)
    res = pl.pallas_call(
        body, name=name,
        grid=(gi, gj, nk),
        in_specs=[a_spec, b_spec] + [o_spec] * ne + [any_spec] * (ns_in + len(after)),
        out_specs=[o_spec] * no + [any_spec] * ns_out,
        out_shape=[jax.ShapeDtypeStruct((m, n), dt) for dt in out_dtypes] + (list(side.out_shape) if side else []),
        scratch_shapes=[pltpu.VMEM((tm, tn), F32)] + (side.scratch() if side else []),
        compiler_params=_params(*(("arbitrary",) * 3 if side else ("parallel", "parallel", "arbitrary"))),
    )(a, b, *extras, *(side.inputs if side else ()), *after)
    return res[0] if len(res) == 1 else res


def _relu2_epilogue(r):
    return r, jnp.square(jnp.maximum(r, 0.0))


def _relu2_bwd_epilogue(r, f1):
    return (r * (2.0 * jnp.maximum(f1.astype(F32), 0.0)),)


def _row_tile(seq, width, nbytes=2 * 1024 * 1024):
    return _tile(seq, max(8, nbytes // (4 * width)), 8)


def _silu(c_all):
    def body(c_ref, o_ref):
        v = c_ref[...]
        o_ref[...] = v * jax.nn.sigmoid(v)

    return pl.pallas_call(body, name="silu", out_shape=jax.ShapeDtypeStruct(c_all.shape, F32))(c_all)


def _norm_mod_fwd(x, g, shift, scale, seq, *, name, resid=None):
    t, w = x.shape
    tr = _row_tile(seq, w, (1 if resid is not None else 2) * 1024 * 1024)
    nb = seq // tr
    has_res = resid is not None

    def body(*refs):
        if has_res:
            x_ref, o_ref, gate_ref, g_ref, sh_ref, sc_ref, x1_ref, h_ref = refs
            xv = x_ref[...] + gate_ref[0] * o_ref[...]
            x1_ref[...] = xv
        else:
            x_ref, g_ref, sh_ref, sc_ref, h_ref = refs
            xv = x_ref[...]
        r = lax.rsqrt(jnp.mean(xv * xv, axis=-1, keepdims=True) + EPS)
        nrm = xv * r * g_ref[...]
        h_ref[...] = (nrm * (1.0 + sc_ref[0]) + sh_ref[0]).astype(BF16)

    row = pl.BlockSpec((tr, w), lambda i: (i, 0))
    vec = pl.BlockSpec((1, w), lambda i: (0, 0))
    per_ex = pl.BlockSpec((1, 1, w), lambda i: (i // nb, 0, 0))
    if has_res:
        o, gate = resid
        ins, in_specs = (x, o, gate, g, shift, scale), [row, row, per_ex, vec, per_ex, per_ex]
        out_shape = [jax.ShapeDtypeStruct((t, w), F32), jax.ShapeDtypeStruct((t, w), BF16)]
        out_specs = [row, row]
    else:
        ins, in_specs = (x, g, shift, scale), [row, vec, per_ex, per_ex]
        out_shape = jax.ShapeDtypeStruct((t, w), BF16)
        out_specs = row
    return pl.pallas_call(body, name=name, grid=(t // tr,), in_specs=in_specs, out_specs=out_specs,
                          out_shape=out_shape, compiler_params=_params("parallel"))(*ins)


def _norm_mod_bwd(x, g, scale, dh, seq, *, name, dres=None, gate_o=None):
    t, w = x.shape
    nex = t // seq
    tr = _row_tile(seq, w, 1024 * 1024)
    nb = seq // tr
    has_res, has_gate = dres is not None, gate_o is not None

    def body(*refs):
        refs = list(refs)
        x_ref, g_ref, sc_ref, dh_ref = refs[:4]
        p = 4
        dres_ref = None
        if has_res:
            dres_ref = refs[p]
            p += 1
        if has_gate:
            o_ref, gate_ref = refs[p:p + 2]
            p += 2
        dx_ref, dsh_ref, dsc_ref, dg_ref = refs[p:p + 4]
        p += 4
        i = pl.program_id(0)

        @pl.when(i % nb == 0)
        def _():
            dsh_ref[...] = jnp.zeros_like(dsh_ref)
            dsc_ref[...] = jnp.zeros_like(dsc_ref)
            if has_gate:
                refs[p + 1][...] = jnp.zeros_like(refs[p + 1])

        @pl.when(i == 0)
        def _():
            dg_ref[...] = jnp.zeros_like(dg_ref)

        xv = x_ref[...]
        gv = g_ref[...]
        dhv = dh_ref[...]
        r = lax.rsqrt(jnp.mean(xv * xv, axis=-1, keepdims=True) + EPS)
        xh = xv * r
        dsh_ref[0] += jnp.sum(dhv, axis=0, keepdims=True)
        dsc_ref[0] += jnp.sum(dhv * (xh * gv), axis=0, keepdims=True)
        dn = dhv * (1.0 + sc_ref[0])
        dg_ref[...] += jnp.sum(dn * xh, axis=0, keepdims=True)
        dxh = dn * gv
        dx = r * (dxh - xh * jnp.mean(dxh * xh, axis=-1, keepdims=True))
        if has_res:
            dx = dx + dres_ref[...]
        dx_ref[...] = dx
        if has_gate:
            do_ref, dgate_ref = refs[p:p + 2]
            do_ref[...] = (dx * gate_ref[0]).astype(BF16)
            dgate_ref[0] += jnp.sum(dx * o_ref[...], axis=0, keepdims=True)

    row = pl.BlockSpec((tr, w), lambda i: (i, 0))
    vec = pl.BlockSpec((1, w), lambda i: (0, 0))
    per_ex = pl.BlockSpec((1, 1, w), lambda i: (i // nb, 0, 0))
    ins, in_specs = [x, g, scale, dh], [row, vec, per_ex, row]
    if has_res:
        ins.append(dres)
        in_specs.append(row)
    if has_gate:
        ins += list(gate_o)
        in_specs += [row, per_ex]
    ex_shape = jax.ShapeDtypeStruct((nex, 1, w), F32)
    out_shape = [jax.ShapeDtypeStruct((t, w), F32), ex_shape, ex_shape, jax.ShapeDtypeStruct((1, w), F32)]
    out_specs = [row, per_ex, per_ex, vec]
    if has_gate:
        out_shape += [jax.ShapeDtypeStruct((t, w), BF16), ex_shape]
        out_specs += [row, per_ex]
    return pl.pallas_call(body, name=name, grid=(t // tr,), in_specs=in_specs, out_specs=out_specs,
                          out_shape=out_shape, compiler_params=_params("arbitrary"))(*ins)


def _gelu_parts(xv):
    cdf = 0.5 * (1.0 + lax.erf(xv * (1.0 / math.sqrt(2.0))))
    return cdf


def _tril_mask():
    r = lax.broadcasted_iota(jnp.int32, (CHUNK, CHUNK), 0)
    c = lax.broadcasted_iota(jnp.int32, (CHUNK, CHUNK), 1)
    return c <= r


def _gmlp_fwd(uv, g_v, w_s, b_col):
    t = uv.shape[0]
    gw = GM_WIDTH // GM_GROUPS
    nck = 2
    tr = nck * CHUNK

    def body(uv_ref, gv_ref, w_ref, b_ref, o_ref):
        mask = _tril_mask()
        for ck in range(nck):
            rows = pl.ds(ck * CHUNK, CHUNK)
            xv = uv_ref[rows, :]
            z = xv * _gelu_parts(xv)
            u, v = z[:, :GM_WIDTH], z[:, GM_WIDTH:]
            r = lax.rsqrt(jnp.mean(v * v, axis=-1, keepdims=True) + EPS)
            vn = (v * r * gv_ref[...]).astype(BF16)
            for gi in range(GM_GROUPS):
                cols = slice(gi * gw, (gi + 1) * gw)
                wg = jnp.where(mask, w_ref[gi], 0.0).astype(BF16)
                mx = jnp.dot(wg, vn[:, cols], preferred_element_type=F32) + b_ref[gi]
                o_ref[rows, cols] = (u[:, cols] * mx).astype(BF16)

    return pl.pallas_call(
        body, name="gmlp_fwd", grid=(t // tr,),
        in_specs=[pl.BlockSpec((tr, 2 * GM_WIDTH), lambda i: (i, 0)),
                  pl.BlockSpec((1, GM_WIDTH), lambda i: (0, 0)),
                  pl.BlockSpec((GM_GROUPS, CHUNK, CHUNK), lambda i: (0, 0, 0)),
                  pl.BlockSpec((GM_GROUPS, CHUNK, 1), lambda i: (0, 0, 0))],
        out_specs=pl.BlockSpec((tr, GM_WIDTH), lambda i: (i, 0)),
        out_shape=jax.ShapeDtypeStruct((t, GM_WIDTH), BF16),
        compiler_params=_params("parallel"))(uv, g_v, w_s, b_col)


def _gmlp_bwd(uv, g_v, w_s, b_col, da):
    t = uv.shape[0]
    gw = GM_WIDTH // GM_GROUPS
    nck = 2
    tr = nck * CHUNK
    inv_sqrt_2pi = 1.0 / math.sqrt(2.0 * math.pi)

    def body(uv_ref, gv_ref, w_ref, b_ref, da_ref, duv_ref, dgv_ref, dw_ref, db_ref, dvn_ref):
        @pl.when(pl.program_id(0) == 0)
        def _():
            dgv_ref[...] = jnp.zeros_like(dgv_ref)
            dw_ref[...] = jnp.zeros_like(dw_ref)
            db_ref[...] = jnp.zeros_like(db_ref)

        mask = _tril_mask()
        for ck in range(nck):
            rows = pl.ds(ck * CHUNK, CHUNK)
            xv = uv_ref[rows, :]
            cdf = _gelu_parts(xv)
            z = xv * cdf
            u, v = z[:, :GM_WIDTH], z[:, GM_WIDTH:]
            r = lax.rsqrt(jnp.mean(v * v, axis=-1, keepdims=True) + EPS)
            vh = v * r
            gv = gv_ref[...]
            vn = (vh * gv).astype(BF16)
            dav = da_ref[rows, :]
            du_parts = []
            for gi in range(GM_GROUPS):
                cols = slice(gi * gw, (gi + 1) * gw)
                wg = jnp.where(mask, w_ref[gi], 0.0).astype(BF16)
                vng = vn[:, cols]
                mx = jnp.dot(wg, vng, preferred_element_type=F32) + b_ref[gi]
                du_parts.append(dav[:, cols] * mx)
                dmx = dav[:, cols] * u[:, cols]
                db_ref[gi] += jnp.sum(dmx, axis=1, keepdims=True)
                dmb = dmx.astype(BF16)
                dwg = lax.dot_general(dmb, vng, (((1,), (1,)), ((), ())), preferred_element_type=F32)
                dw_ref[gi] += jnp.where(mask, dwg, 0.0)
                dvn_ref[:, cols] = lax.dot_general(wg, dmb, (((0,), (0,)), ((), ())),
                                                   preferred_element_type=F32)
            dvn = dvn_ref[...]
            dgv_ref[...] += jnp.sum(dvn * vh, axis=0, keepdims=True)
            dvh = dvn * gv
            dv = r * (dvh - vh * jnp.mean(dvh * vh, axis=-1, keepdims=True))
            dz = jnp.concatenate(du_parts + [dv], axis=1)
            dgelu = cdf + xv * (jnp.exp(-0.5 * xv * xv) * inv_sqrt_2pi)
            duv_ref[rows, :] = (dz * dgelu).astype(BF16)

    return pl.pallas_call(
        body, name="gmlp_bwd", grid=(t // tr,),
        in_specs=[pl.BlockSpec((tr, 2 * GM_WIDTH), lambda i: (i, 0)),
                  pl.BlockSpec((1, GM_WIDTH), lambda i: (0, 0)),
                  pl.BlockSpec((GM_GROUPS, CHUNK, CHUNK), lambda i: (0, 0, 0)),
                  pl.BlockSpec((GM_GROUPS, CHUNK, 1), lambda i: (0, 0, 0)),
                  pl.BlockSpec((tr, GM_WIDTH), lambda i: (i, 0))],
        out_specs=[pl.BlockSpec((tr, 2 * GM_WIDTH), lambda i: (i, 0)),
                   pl.BlockSpec((1, GM_WIDTH), lambda i: (0, 0)),
                   pl.BlockSpec((GM_GROUPS, CHUNK, CHUNK), lambda i: (0, 0, 0)),
                   pl.BlockSpec((GM_GROUPS, CHUNK, 1), lambda i: (0, 0, 0))],
        out_shape=[jax.ShapeDtypeStruct((t, 2 * GM_WIDTH), BF16),
                   jax.ShapeDtypeStruct((1, GM_WIDTH), F32),
                   jax.ShapeDtypeStruct((GM_GROUPS, CHUNK, CHUNK), F32),
                   jax.ShapeDtypeStruct((GM_GROUPS, CHUNK, 1), F32)],
        scratch_shapes=[pltpu.VMEM((CHUNK, GM_WIDTH), F32)],
        compiler_params=_params("arbitrary"))(uv, g_v, w_s, b_col, da)


def _rope_tables(pos, invf):
    nb, s, _ = pos.shape
    ts = _tile(s, 512, 8)
    half = QK_ROPE // 2

    def body(pos_ref, invf_ref, o_ref):
        ang = pos_ref[0] * invf_ref[...]
        cs, sn = jnp.cos(ang), jnp.sin(ang)
        lane = lax.broadcasted_iota(jnp.int32, (1, LANE), 1)
        o_ref[0, :, :LANE] = jnp.where(lane < QK_ROPE, cs, 0.0)
        o_ref[0, :, LANE:2 * LANE] = jnp.where((lane >= half) & (lane < QK_ROPE), sn, 0.0)
        o_ref[0, :, 2 * LANE:] = jnp.where(lane < half, -sn, 0.0)

    return pl.pallas_call(
        body, name="rope_tables", grid=(nb, s // ts),
        in_specs=[pl.BlockSpec((1, ts, 1), lambda b, i: (b, i, 0)), pl.BlockSpec((1, LANE), lambda b, i: (0, 0))],
        out_specs=pl.BlockSpec((1, ts, 3 * LANE), lambda b, i: (b, i, 0)),
        out_shape=jax.ShapeDtypeStruct((nb, s, 3 * LANE), F32),
        compiler_params=_params("parallel", "parallel"))(pos, invf)


def _head_rstd(lo, hi):
    ss = jnp.sum(lo * lo, axis=-1, keepdims=True) + jnp.sum(hi * hi, axis=-1, keepdims=True)
    return lax.rsqrt(ss * (1.0 / (QK_NOPE + QK_ROPE)) + EPS)


def _head_specs(ts, kpe):
    if kpe is None:
        return [pl.BlockSpec((ts, QK_PAD), lambda r, h: (r, h))]
    return [pl.BlockSpec((ts, QK_NOPE), lambda r, h: (r, 2 * h)), pl.BlockSpec((ts, LANE), lambda r, h: (r, 0))]


def _head_tiles(x_refs):
    if len(x_refs) == 1:
        return x_refs[0][:, :QK_NOPE], x_refs[0][:, QK_NOPE:]
    return x_refs[0][...], x_refs[1][...]


def _qk_norm_fwd(x, kpe, g_pad, tabs, *, name):
    t = x.shape[0]
    ts = _tile(t, 1024, 8)
    half = QK_ROPE // 2
    nx = 1 if kpe is None else 2

    def body(*refs):
        x_refs, (g_ref, tab_ref, o_ref) = refs[:nx], refs[nx:]
        lo, hi = _head_tiles(x_refs)
        r = _head_rstd(lo, hi)
        hi = hi * r * g_ref[:, QK_NOPE:]
        hi = (hi * tab_ref[:, :LANE] + pltpu.roll(hi, half, 1) * tab_ref[:, LANE:2 * LANE]
              + pltpu.roll(hi, LANE - half, 1) * tab_ref[:, 2 * LANE:])
        o_ref[:, :QK_NOPE] = (lo * r * g_ref[:, :QK_NOPE]).astype(BF16)
        o_ref[:, QK_NOPE:] = hi.astype(BF16)

    return pl.pallas_call(
        body, name=name, grid=(t // ts, MLA_HEADS),
        in_specs=_head_specs(ts, kpe) + [pl.BlockSpec((1, QK_PAD), lambda r, h: (0, 0)),
                                         pl.BlockSpec((ts, 3 * LANE), lambda r, h: (r, 0))],
        out_specs=pl.BlockSpec((ts, QK_PAD), lambda r, h: (r, h)),
        out_shape=jax.ShapeDtypeStruct((t, MLA_HEADS * QK_PAD), BF16),
        compiler_params=_params("parallel", "parallel"))(*((x,) if kpe is None else (x, kpe)), g_pad, tabs)


def _qk_norm_bwd(x, kpe, g_pad, tabs, dout, dv, *, name):
    t = x.shape[0]
    ts = _tile(t, 1024, 8)
    half = QK_ROPE // 2
    is_k = kpe is not None
    nx = 2 if is_k else 1
    inv_width = 1.0 / (QK_NOPE + QK_ROPE)

    def body(*refs):
        x_refs, (g_ref, tab_ref, do_ref), rest = refs[:nx], refs[nx:nx + 3], refs[nx + 3:]
        if is_k:
            dv_ref, dx_ref, dkpe_ref, dg_ref = rest
        else:
            dx_ref, dg_ref = rest

        @pl.when((pl.program_id(0) == 0) & (pl.program_id(1) == 0))
        def _():
            dg_ref[...] = jnp.zeros_like(dg_ref)

        lo, hi = _head_tiles(x_refs)
        r = _head_rstd(lo, hi)
        lo, hi = lo * r, hi * r
        dlo, dhi = do_ref[:, :QK_NOPE], do_ref[:, QK_NOPE:]
        dhi = (dhi * tab_ref[:, :LANE] + pltpu.roll(dhi * tab_ref[:, LANE:2 * LANE], LANE - half, 1)
               + pltpu.roll(dhi * tab_ref[:, 2 * LANE:], half, 1))
        dg_ref[:, :QK_NOPE] += jnp.sum(dlo * lo, axis=0, keepdims=True)
        dg_ref[:, QK_NOPE:] += jnp.sum(dhi * hi, axis=0, keepdims=True)
        dlo, dhi = dlo * g_ref[:, :QK_NOPE], dhi * g_ref[:, QK_NOPE:]
        mean = (jnp.sum(dlo * lo, axis=-1, keepdims=True) + jnp.sum(dhi * hi, axis=-1, keepdims=True)) * inv_width
        dx_ref[:, :QK_NOPE] = (r * (dlo - lo * mean)).astype(BF16)
        dxhi = r * (dhi - hi * mean)
        if is_k:
            dx_ref[:, QK_NOPE:] = dv_ref[...].astype(BF16)

            @pl.when(pl.program_id(1) == 0)
            def _():
                dkpe_ref[...] = jnp.zeros_like(dkpe_ref)

            dkpe_ref[...] += dxhi
        else:
            dx_ref[:, QK_NOPE:] = dxhi.astype(BF16)

    head = pl.BlockSpec((ts, QK_PAD), lambda r, h: (r, h))
    vec = pl.BlockSpec((1, QK_PAD), lambda r, h: (0, 0))
    in_specs = _head_specs(ts, kpe) + [vec, pl.BlockSpec((ts, 3 * LANE), lambda r, h: (r, 0)), head]
    ins = [x] + ([kpe] if is_k else []) + [g_pad, tabs, dout]
    out_specs = [head]
    out_shape = [jax.ShapeDtypeStruct((t, MLA_HEADS * QK_PAD), BF16)]
    if is_k:
        ins.append(dv)
        in_specs.append(pl.BlockSpec((ts, V_HEAD), lambda r, h: (r, h)))
        out_specs.append(pl.BlockSpec((ts, LANE), lambda r, h: (r, 0)))
        out_shape.append(jax.ShapeDtypeStruct((t, LANE), F32))
    out_specs.append(vec)
    out_shape.append(jax.ShapeDtypeStruct((1, QK_PAD), F32))
    return pl.pallas_call(body, name=name, grid=(t // ts, MLA_HEADS), in_specs=in_specs, out_specs=out_specs,
                          out_shape=out_shape, compiler_params=_params("arbitrary", "arbitrary"))(*ins)


def _causal_probs(q, k, q0):
    scale = (QK_NOPE + QK_ROPE) ** -0.5
    sc = lax.dot_general(q, k, (((1,), (1,)), ((), ())), preferred_element_type=F32) * scale
    qi = q0 + lax.broadcasted_iota(jnp.int32, sc.shape, 0)
    ki = lax.broadcasted_iota(jnp.int32, sc.shape, 1)
    sc = jnp.where(ki <= qi, sc, -1e30)
    e = jnp.exp(sc - jnp.max(sc, axis=-1, keepdims=True))
    return e / jnp.sum(e, axis=-1, keepdims=True)


def _attn_specs(s, tq):
    nq = s // tq
    q_spec = pl.BlockSpec((tq, QK_PAD), lambda b, h, i: (b * nq + i, h))
    o_spec = pl.BlockSpec((tq, V_HEAD), lambda b, h, i: (b * nq + i, h))
    k_spec = pl.BlockSpec((s, QK_PAD), lambda b, h, i: (b, h))
    v_spec = pl.BlockSpec((s, V_HEAD), lambda b, h, i: (b, 2 * h + 1))
    dv_spec = pl.BlockSpec((s, V_HEAD), lambda b, h, i: (b, h))
    return q_spec, o_spec, k_spec, v_spec, dv_spec


def _attn_fwd(qn, kn, kv_raw, nb):
    t = qn.shape[0]
    s = t // nb
    tq = _tile(s, 256, 8)
    nq = s // tq
    q_spec, o_spec, k_spec, v_spec, _ = _attn_specs(s, tq)

    def body(q_ref, k_ref, v_ref, o_ref):
        def block(g):
            keys = (g + 1) * tq
            p = _causal_probs(q_ref[...], k_ref[:keys, :], g * tq)
            o_ref[...] = jnp.dot(p.astype(BF16), v_ref[:keys, :].astype(BF16),
                                 preferred_element_type=F32).astype(BF16)

        for g in range(nq):
            pl.when(pl.program_id(2) == g)(functools.partial(block, g))

    return pl.pallas_call(
        body, name="attn_fwd", grid=(nb, MLA_HEADS, nq),
        in_specs=[q_spec, k_spec, v_spec], out_specs=o_spec,
        out_shape=jax.ShapeDtypeStruct((t, MLA_HEADS * V_HEAD), BF16),
        compiler_params=_params("parallel", "parallel", "parallel"))(qn, kn, kv_raw)


def _attn_bwd(qn, kn, kv_raw, dattn, nb):
    t = qn.shape[0]
    s = t // nb
    tq = _tile(s, 256, 8)
    nq = s // tq
    scale = (QK_NOPE + QK_ROPE) ** -0.5
    q_spec, o_spec, k_spec, v_spec, dv_spec = _attn_specs(s, tq)

    def body(q_ref, k_ref, v_ref, do_ref, dq_ref, dk_ref, dv_ref):
        @pl.when(pl.program_id(2) == 0)
        def _():
            dk_ref[...] = jnp.zeros_like(dk_ref)
            dv_ref[...] = jnp.zeros_like(dv_ref)

        def block(g):
            keys = (g + 1) * tq
            q, k, v, do = q_ref[...], k_ref[:keys, :], v_ref[:keys, :].astype(BF16), do_ref[...]
            p = _causal_probs(q, k, g * tq)
            dv_ref[:keys, :] += lax.dot_general(p.astype(BF16), do, (((0,), (0,)), ((), ())),
                                                preferred_element_type=F32)
            dp = lax.dot_general(do, v, (((1,), (1,)), ((), ())), preferred_element_type=F32)
            ds = (p * (dp - jnp.sum(p * dp, axis=-1, keepdims=True)) * scale).astype(BF16)
            dq_ref[...] = jnp.dot(ds, k, preferred_element_type=F32)
            dk_ref[:keys, :] += lax.dot_general(ds, q, (((0,), (0,)), ((), ())), preferred_element_type=F32)

        for g in range(nq):
            pl.when(pl.program_id(2) == g)(functools.partial(block, g))

    return pl.pallas_call(
        body, name="attn_bwd", grid=(nb, MLA_HEADS, nq),
        in_specs=[q_spec, k_spec, v_spec, o_spec],
        out_specs=[q_spec, k_spec, dv_spec],
        out_shape=[jax.ShapeDtypeStruct((t, MLA_HEADS * QK_PAD), F32),
                   jax.ShapeDtypeStruct((t, MLA_HEADS * QK_PAD), F32),
                   jax.ShapeDtypeStruct((t, MLA_HEADS * V_HEAD), F32)],
        compiler_params=_params("parallel", "parallel", "arbitrary"))(qn, kn, kv_raw, dattn)


def _gatemix_fwd(gpa, gpb, ya, yb):
    t, w = ya.shape
    tr = _tile(t, 128, 8)

    def body(ga_ref, gb_ref, ya_ref, yb_ref, o_ref):
        o_ref[...] = (jax.nn.sigmoid(ga_ref[...]) * ya_ref[...]
                      + jax.nn.sigmoid(gb_ref[...]) * yb_ref[...]).astype(BF16)

    row = pl.BlockSpec((tr, w), lambda i: (i, 0))
    return pl.pallas_call(body, name="gatemix_fwd", grid=(t // tr,), in_specs=[row] * 4, out_specs=row,
                          out_shape=jax.ShapeDtypeStruct((t, w), BF16),
                          compiler_params=_params("parallel"))(gpa, gpb, ya, yb)


def _gatemix_bwd(gpa, gpb, ya, yb, dmix):
    t, w = ya.shape
    tr = _tile(t, 128, 8)

    def body(ga_ref, gb_ref, ya_ref, yb_ref, dm_ref, dya_ref, dyb_ref, dga_ref, dgb_ref):
        dm = dm_ref[...]
        sa = jax.nn.sigmoid(ga_ref[...])
        sb = jax.nn.sigmoid(gb_ref[...])
        dya_ref[...] = (dm * sa).astype(BF16)
        dyb_ref[...] = (dm * sb).astype(BF16)
        dga_ref[...] = (dm * ya_ref[...] * sa * (1.0 - sa)).astype(BF16)
        dgb_ref[...] = (dm * yb_ref[...] * sb * (1.0 - sb)).astype(BF16)

    row = pl.BlockSpec((tr, w), lambda i: (i, 0))
    return pl.pallas_call(body, name="gatemix_bwd", grid=(t // tr,), in_specs=[row] * 5, out_specs=[row] * 4,
                          out_shape=[jax.ShapeDtypeStruct((t, w), BF16)] * 4,
                          compiler_params=_params("parallel"))(gpa, gpb, ya, yb, dmix)


def _final(x1, f2, gate, target, seq):
    t, w = x1.shape
    nex = t // seq
    tr = _row_tile(seq, w, 1024 * 1024)
    nb = seq // tr

    def body(x_ref, f_ref, gate_ref, t_ref, loss_ref, dy_ref, dff_ref, dgate_ref):
        i = pl.program_id(0)

        @pl.when(i == 0)
        def _():
            loss_ref[...] = jnp.zeros_like(loss_ref)

        @pl.when(i % nb == 0)
        def _():
            dgate_ref[...] = jnp.zeros_like(dgate_ref)

        fv = f_ref[...]
        gv = gate_ref[0]
        err = x_ref[...] + gv * fv - t_ref[...]
        sq = jnp.sum(err * err, axis=1, keepdims=True)
        loss_ref[...] += jnp.sum(sq, axis=0, keepdims=True) * (0.5 / w)
        dy = err * (1.0 / w)
        dy_ref[...] = dy
        dff_ref[...] = (dy * gv).astype(BF16)
        dgate_ref[0] += jnp.sum(dy * fv, axis=0, keepdims=True)

    row = pl.BlockSpec((tr, w), lambda i: (i, 0))
    per_ex = pl.BlockSpec((1, 1, w), lambda i: (i // nb, 0, 0))
    return pl.pallas_call(
        body, name="loss_head", grid=(t // tr,),
        in_specs=[row, row, per_ex, row],
        out_specs=[pl.BlockSpec((1, LANE), lambda i: (0, 0)), row, row, per_ex],
        out_shape=[jax.ShapeDtypeStruct((1, LANE), F32), jax.ShapeDtypeStruct((t, w), F32),
                   jax.ShapeDtypeStruct((t, w), BF16), jax.ShapeDtypeStruct((nex, 1, w), F32)],
        compiler_params=_params("arbitrary"))(x1, f2, gate, target)


def _sum_slots(parts, *, name):
    _, r, c = parts[0].shape
    tr, tc = _tile(r, 256, 16), _tile(c, 1024, LANE)

    def body(*refs):
        acc = None
        for p_ref in refs[:-1]:
            for j in range(p_ref.shape[0]):
                val = p_ref[j].astype(F32)
                acc = val if acc is None else acc + val
        refs[-1][...] = acc

    return pl.pallas_call(body, name=name, grid=(r // tr, c // tc),
                          in_specs=[pl.BlockSpec((p.shape[0], tr, tc), lambda i, j: (0, i, j)) for p in parts],
                          out_specs=pl.BlockSpec((tr, tc), lambda i, j: (i, j)),
                          out_shape=jax.ShapeDtypeStruct((r, c), F32),
                          compiler_params=_params("parallel", "parallel"))(*parts)


def _adamw(w, m, v, parts, *, name):
    _, r, c = w.shape
    parts = [p.reshape(w.shape) for p in parts]
    tc = _tile(c, 1024, LANE) if c % LANE == 0 else c
    tr = _tile(r, max(8, (256 * 1024) // tc // 8 * 8), 8)
    npart = len(parts)
    b1c = 1.0 - ADAM_B1 ** ADAM_STEP
    b2c = 1.0 - ADAM_B2 ** ADAM_STEP

    def body(*refs):
        w_ref, m_ref, v_ref = refs[:3]
        g_ref, d_ref, mo_ref, vo_ref = refs[3 + npart:]
        g = refs[3][...].astype(F32)
        for p_ref in refs[4:3 + npart]:
            g = g + p_ref[...].astype(F32)
        m2 = ADAM_B1 * m_ref[...] + (1.0 - ADAM_B1) * g
        v2 = ADAM_B2 * v_ref[...] + (1.0 - ADAM_B2) * jnp.square(g)
        m_hat = m2 / b1c
        v_hat = v2 / b2c
        g_ref[...] = g
        d_ref[...] = -ADAM_LR * (m_hat / (jnp.sqrt(v_hat) + ADAM_EPS) + ADAM_WD * w_ref[...])
        mo_ref[...] = m2
        vo_ref[...] = v2

    blk = pl.BlockSpec((1, tr, tc), lambda i, j: (0, i, j))
    return pl.pallas_call(body, name=name, grid=(r // tr, c // tc),
                          in_specs=[blk] * (3 + npart), out_specs=[blk] * 4,
                          out_shape=[jax.ShapeDtypeStruct((1, r, c), F32)] * 4,
                          compiler_params=_params("parallel", "parallel"))(w, m, v, *parts)


def _allgather8(x, *, name, after=()):
    after = tuple(after)

    def body(*refs):
        x_ref, (o_ref, ssem, rsem) = refs[0], refs[1 + len(after):]
        pos = _mesh_pos()
        me = 4 * pos[0] + 2 * pos[1] + pos[2]
        o_ref[me] = x_ref[...]
        cps = []
        for k in range(1, N_DEV):
            cp = pltpu.make_async_remote_copy(src_ref=x_ref, dst_ref=o_ref.at[me], send_sem=ssem.at[k - 1],
                                              recv_sem=rsem.at[k - 1], device_id=_flip(pos, k), device_id_type=MESH)
            cp.start()
            cps.append(cp)
        for cp in cps:
            cp.wait()

    return pl.pallas_call(
        body, name=name,
        out_shape=jax.ShapeDtypeStruct((N_DEV,) + x.shape, x.dtype),
        in_specs=[pl.BlockSpec(memory_space=pltpu.VMEM)] + [pl.BlockSpec(memory_space=pl.ANY)] * len(after),
        out_specs=pl.BlockSpec(memory_space=pltpu.VMEM),
        scratch_shapes=[pltpu.SemaphoreType.DMA((N_DEV - 1,)), pltpu.SemaphoreType.DMA((N_DEV - 1,))],
        compiler_params=pltpu.CompilerParams(vmem_limit_bytes=VMEM_LIMIT),
    )(x, *after)


def _sibling_swap(p, *, name):
    def body(p_ref, o_ref, ssem, rsem):
        cp = pltpu.make_async_remote_copy(src_ref=p_ref, dst_ref=o_ref, send_sem=ssem, recv_sem=rsem,
                                          device_id=_flip(_mesh_pos(), 1), device_id_type=MESH)
        cp.start()
        cp.wait()

    return pl.pallas_call(
        body, name=name,
        out_shape=jax.ShapeDtypeStruct(p.shape, p.dtype),
        in_specs=[pl.BlockSpec(memory_space=pl.ANY)],
        out_specs=pl.BlockSpec(memory_space=pl.ANY),
        scratch_shapes=[pltpu.SemaphoreType.DMA, pltpu.SemaphoreType.DMA],
    )(p)


_SMALL = ("b_ada", "g_norm1", "g_v", "w_s", "b_s", "g_q_lat", "g_kv_lat", "g_qn", "g_kn", "g_norm2")
_BIG = ("w_in", "w_uq", "w_ukv", "w_branch_a", "w_branch_b", "w_out", "w_ff1", "w_ff2")
_SHARD_AXIS = {"w_in": 1, "w_uq": 1, "w_ukv": 1, "w_branch_a": 1, "w_branch_b": 0, "w_out": 0, "w_ff1": 1, "w_ff2": 0}
_WEIGHTS = ("w_ada", "b_ada", "g_norm1", "w_in", "g_v", "w_s", "b_s", "g_q_lat", "g_kv_lat", "w_uq", "w_ukv",
            "g_qn", "g_kn", "w_branch_a", "w_branch_b", "w_out", "g_norm2", "w_ff1", "w_ff2")


def kernel(x, c, positions, w_ada, b_ada, g_norm1, w_in, g_v, w_s, b_s, g_q_lat, g_kv_lat, w_uq, w_ukv, g_qn, g_kn, w_branch_a, w_branch_b, w_out, g_norm2, w_ff1, w_ff2, loss_target, m_w_ada, m_b_ada, m_g_norm1, m_w_in, m_g_v, m_w_s, m_b_s, m_g_q_lat, m_g_kv_lat, m_w_uq, m_w_ukv, m_g_qn, m_g_kn, m_w_branch_a, m_w_branch_b, m_w_out, m_g_norm2, m_w_ff1, m_w_ff2, v_w_ada, v_b_ada, v_g_norm1, v_w_in, v_g_v, v_w_s, v_b_s, v_g_q_lat, v_g_kv_lat, v_w_uq, v_w_ukv, v_g_qn, v_g_kn, v_w_branch_a, v_w_branch_b, v_w_out, v_g_norm2, v_w_ff1, v_w_ff2):
    wts = dict(w_ada=w_ada, b_ada=b_ada, g_norm1=g_norm1, w_in=w_in, g_v=g_v, w_s=w_s, b_s=b_s, g_q_lat=g_q_lat,
               g_kv_lat=g_kv_lat, w_uq=w_uq, w_ukv=w_ukv, g_qn=g_qn, g_kn=g_kn, w_branch_a=w_branch_a,
               w_branch_b=w_branch_b, w_out=w_out, g_norm2=g_norm2, w_ff1=w_ff1, w_ff2=w_ff2)
    mom1 = dict(w_ada=m_w_ada, b_ada=m_b_ada, g_norm1=m_g_norm1, w_in=m_w_in, g_v=m_g_v, w_s=m_w_s, b_s=m_b_s,
                g_q_lat=m_g_q_lat, g_kv_lat=m_g_kv_lat, w_uq=m_w_uq, w_ukv=m_w_ukv, g_qn=m_g_qn, g_kn=m_g_kn,
                w_branch_a=m_w_branch_a, w_branch_b=m_w_branch_b, w_out=m_w_out, g_norm2=m_g_norm2,
                w_ff1=m_w_ff1, w_ff2=m_w_ff2)
    mom2 = dict(w_ada=v_w_ada, b_ada=v_b_ada, g_norm1=v_g_norm1, w_in=v_w_in, g_v=v_g_v, w_s=v_w_s, b_s=v_b_s,
                g_q_lat=v_g_q_lat, g_kv_lat=v_g_kv_lat, w_uq=v_w_uq, w_ukv=v_w_ukv, g_qn=v_g_qn, g_kn=v_g_kn,
                w_branch_a=v_w_branch_a, w_branch_b=v_w_branch_b, w_out=v_w_out, g_norm2=v_g_norm2,
                w_ff1=v_w_ff1, w_ff2=v_w_ff2)

    nb, seq, dm = x.shape
    t = nb * seq
    nh, qkh = MLA_HEADS, QK_NOPE + QK_ROPE
    bh = nb * nh
    off_q = 2 * GM_WIDTH
    off_kv = off_q + Q_LORA
    off_kpe = off_kv + KV_LORA
    off_gate = off_kpe + QK_ROPE
    in_cols = off_gate + 2 * dm
    ins = in_cols // N_CHIP
    insp = _round_up(ins, LANE)
    ada_cols = N_MOD * dm // N_CHIP

    ix, iy, ic = _mesh_pos()
    me = 4 * ix + 2 * iy + ic
    chip = 2 * ix + iy
    xf = x.reshape(t, dm)
    tgt = loss_target.reshape(t, dm)

    c_all = _allgather8(c, name="ag_cond").reshape(N_DEV * nb, dm)
    cond = _silu(c_all)
    b_sh = lax.dynamic_slice(b_ada, (0, chip * ada_cols), (1, ada_cols))
    mod_sh = _mm(cond, w_ada[0], name="ada_fwd", tn=2048, tk=512) + b_sh
    mod8 = _allgather8(mod_sh, name="ag_mod")
    mod_all = jnp.concatenate([mod8[2 * s] for s in range(N_CHIP)], axis=1)
    mod = lax.dynamic_slice(mod_all, (nb * me, 0), (nb, N_MOD * dm))
    sh1, sc1, ga1, sh2, sc2, ga2 = [mod[:, j * dm:(j + 1) * dm].reshape(nb, 1, dm) for j in range(N_MOD)]

    shards = {n: wts[n][0].astype(BF16) for n in _BIG}
    shards["w_in"] = jnp.pad(shards["w_in"], ((0, 0), (0, insp - ins)))

    def gathers(*names):
        return _Side([_Gather(shards[n], _SHARD_AXIS[n]) for n in names])

    full = {}
    (full["w_in"],) = _comm_call(gathers("w_in"), name="ag_w_in")
    groups = (("w_branch_a", "w_uq", "w_ukv"), ("w_branch_b", "w_out"), ("w_ff1",), ("w_ff2",))
    flying, tokens, dep = [], [], full["w_in"]
    for gi, names in enumerate(groups):
        state, tok = _split_start(gathers(*names), name="ag_start_%d" % gi, after=(dep,))
        flying.append(state)
        tokens.append(tok)
        dep = tok

    def gathered(gi, relay_after, finish_after):
        state = _split_step(flying[gi], "arrive", name="ag_arrive_%d" % gi, after=(relay_after,))
        state = _split_step(state, "forward", name="ag_forward_%d" % gi)
        state = _split_step(state, "finish", name="ag_finish_%d" % gi, after=(finish_after,))
        full.update(zip(groups[gi], _split_results(state)))

    h1 = _norm_mod_fwd(xf, g_norm1, sh1, sc1, seq, name="norm1_fwd")
    proj_p = _mm(h1, full["w_in"], name="in_fwd", after=tokens)
    gathered(0, proj_p, proj_p)

    def shard_pieces(lo, hi):
        out = []
        for s in range(N_CHIP):
            a, b = max(lo, s * ins), min(hi, (s + 1) * ins)
            if a < b:
                out.append((s, a - s * ins, b - a))
        return out

    def seg(lo, hi):
        parts = [proj_p[:, s * insp + a:s * insp + a + w] for s, a, w in shard_pieces(lo, hi)]
        return parts[0] if len(parts) == 1 else jnp.concatenate(parts, axis=1)

    bounds = (0, off_q, off_kv, off_kpe, off_gate, off_gate + dm, in_cols)
    uv, q_lat, kv_lat, k_pe, gpa, gpb = [seg(lo, hi) for lo, hi in zip(bounds[:-1], bounds[1:])]

    b_col = b_s[0].reshape(GM_GROUPS, CHUNK, 1)
    a_out = _gmlp_fwd(uv, g_v, w_s[0], b_col)
    y_a = _mm(a_out, full["w_branch_a"], name="ba_fwd")

    zq = jnp.zeros((nb, 1, Q_LORA), F32)
    zkv = jnp.zeros((nb, 1, KV_LORA), F32)
    ql = _norm_mod_fwd(q_lat, g_q_lat, zq, zq, seq, name="qlat_norm_fwd")
    kvl = _norm_mod_fwd(kv_lat, g_kv_lat, zkv, zkv, seq, name="kvlat_norm_fwd")
    w_uq_p = jnp.pad(full["w_uq"].reshape(Q_LORA, nh, qkh), ((0, 0), (0, 0), (0, QK_PAD - qkh)))
    w_uq_p = w_uq_p.reshape(Q_LORA, nh * QK_PAD)
    q_raw = _mm(ql, w_uq_p, name="uq_fwd")
    kv_raw = _mm(kvl, full["w_ukv"], name="ukv_fwd")
    kpe_p = jnp.pad(k_pe, ((0, 0), (0, LANE - QK_ROPE)))
    pos = positions.astype(F32).reshape(nb, seq, 1)
    inv_freq = 1.0 / (ROPE_THETA ** (jnp.arange(0, QK_ROPE, 2, dtype=F32) / QK_ROPE))
    invf = jnp.concatenate([inv_freq, inv_freq, jnp.zeros((LANE - QK_ROPE,), F32)]).reshape(1, LANE)
    gq_pad = jnp.pad(g_qn, ((0, 0), (0, QK_PAD - qkh)))
    gk_pad = jnp.pad(g_kn, ((0, 0), (0, QK_PAD - qkh)))
    tabs = _rope_tables(pos, invf).reshape(t, 3 * LANE)
    qn = _qk_norm_fwd(q_raw, None, gq_pad, tabs, name="qnorm_fwd")
    kn = _qk_norm_fwd(kv_raw, kpe_p, gk_pad, tabs, name="knorm_fwd")
    attn = _attn_fwd(qn, kn, kv_raw, nb)
    gathered(1, qn, attn)
    y_b = _mm(attn, full["w_branch_b"], name="bb_fwd")

    mixed = _gatemix_fwd(gpa, gpb, y_a, y_b)
    o1 = _mm(mixed, full["w_out"], name="out_fwd")
    x1, h2 = _norm_mod_fwd(xf, g_norm2, sh2, sc2, seq, name="norm2_fwd", resid=(o1, ga1))
    gathered(2, attn, o1)
    f1, act = _mm(h2, full["w_ff1"], name="ff1_fwd", out_dtypes=(BF16, BF16), epilogue=_relu2_epilogue)
    gathered(3, h2, act)
    f2 = _mm(act, full["w_ff2"], name="ff2_fwd")
    loss_acc, dy, dff, dga2 = _final(x1, f2, ga2, tgt, seq)
    loss = lax.psum(loss_acc[0, 0], ("x", "y", "c"))

    sent = {}

    def scatter_start(n, g, axis=None):
        job = _Scatter(g, _SHARD_AXIS[n] if axis is None else axis, (1, 2, 3), True)
        sent[n], token = _split_start(_Side([job]), name="rs_start_" + n)
        return token

    def scattered(n, after):
        return _split_results(_split_step(sent[n], "finish", name="rs_wait_" + n, after=(after,)))[0]

    tok = scatter_start("w_ff2", _mm(act, dff, name="ff2_dw", ta=True, out_dtypes=(BF16,)))
    df1 = _mm(dff, full["w_ff2"], name="ff2_dx", tb=True, out_dtypes=(BF16,), epilogue=_relu2_bwd_epilogue,
              extras=(f1,), after=(tok,))
    tok = scatter_start("w_ff1", _mm(h2, df1, name="ff1_dw", ta=True, out_dtypes=(BF16,)))
    dh2 = _mm(df1, full["w_ff1"], name="ff1_dx", tb=True, after=(tok,))
    dx1, dsh2, dsc2, dg_norm2, do1, dga1 = _norm_mod_bwd(x1, g_norm2, sc2, dh2, seq, name="norm2_bwd", dres=dy,
                                                         gate_o=(o1, ga1))
    tok = scatter_start("w_out", _mm(mixed, do1, name="out_dw", ta=True, out_dtypes=(BF16,)))
    dmixed = _mm(do1, full["w_out"], name="out_dx", tb=True, after=(tok,))
    dya, dyb, dgpa, dgpb = _gatemix_bwd(gpa, gpb, y_a, y_b, dmixed)

    tok = scatter_start("w_branch_b", _mm(attn, dyb, name="bb_dw", ta=True, out_dtypes=(BF16,)))
    dattn = _mm(dyb, full["w_branch_b"], name="bb_dx", tb=True, out_dtypes=(BF16,), after=(tok,))
    dqn, dkn, dv = _attn_bwd(qn, kn, kv_raw, dattn, nb)
    dq_raw, dg_qn = _qk_norm_bwd(q_raw, None, gq_pad, tabs, dqn, None, name="qnorm_bwd")
    dkv_raw, dkpe_p, dg_kn = _qk_norm_bwd(kv_raw, kpe_p, gk_pad, tabs, dkn, dv, name="knorm_bwd")
    dk_pe = dkpe_p[:, :QK_ROPE]
    gw_uq_p = _mm(ql, dq_raw, name="uq_dw", ta=True, out_dtypes=(BF16,))
    tok = scatter_start("w_uq", gw_uq_p.reshape(Q_LORA, nh, QK_PAD)[:, :, :qkh].reshape(Q_LORA, nh * qkh))
    dql = _mm(dq_raw, w_uq_p, name="uq_dx", tb=True, after=(tok,))
    tok = scatter_start("w_ukv", _mm(kvl, dkv_raw, name="ukv_dw", ta=True, out_dtypes=(BF16,)))
    dkvl = _mm(dkv_raw, full["w_ukv"], name="ukv_dx", tb=True, after=(tok,))
    dq_lat, _, _, dg_q_lat = _norm_mod_bwd(q_lat, g_q_lat, zq, dql, seq, name="qlat_norm_bwd")
    dkv_lat, _, _, dg_kv_lat = _norm_mod_bwd(kv_lat, g_kv_lat, zkv, dkvl, seq, name="kvlat_norm_bwd")

    tok = scatter_start("w_branch_a", _mm(a_out, dya, name="ba_dw", ta=True, out_dtypes=(BF16,)))
    da = _mm(dya, full["w_branch_a"], name="ba_dx", tb=True, after=(tok,))
    duv, dg_v, dw_s, db_col = _gmlp_bwd(uv, g_v, w_s[0], b_col, da)

    dsegs = (duv, dq_lat.astype(BF16), dkv_lat.astype(BF16), dk_pe.astype(BF16), dgpa, dgpb)
    by_shard = [[] for _ in range(N_CHIP)]
    for dseg, lo, hi in zip(dsegs, bounds[:-1], bounds[1:]):
        for s, a, w in shard_pieces(lo, hi):
            by_shard[s].append(dseg[:, s * ins + a - lo:s * ins + a - lo + w])
    dproj_p = jnp.concatenate([p for s in range(N_CHIP) for p in by_shard[s] + [jnp.zeros((t, insp - ins), BF16)]],
                              axis=1)
    half = dm // 2
    tok = scatter_start("in0", _mm(h1, dproj_p, name="in_dw0", ta=True, out_dtypes=(BF16,), m_rows=half), 1)
    tok = scatter_start("in1", _mm(h1, dproj_p, name="in_dw1", ta=True, out_dtypes=(BF16,), m_rows=half,
                                   m_start=half, after=(tok,)), 1)
    dh1 = _mm(dproj_p, full["w_in"], name="in_dx", tb=True, after=(tok,))
    dx, dsh1, dsc1, dg_norm1 = _norm_mod_bwd(xf, g_norm1, sc1, dh1, seq, name="norm1_bwd", dres=dx1)
    grad_x = dx.reshape(nb, seq, dm)

    done_early = ("w_ff2", "w_ff1", "w_out", "w_branch_b", "w_uq", "w_ukv", "w_branch_a")
    partial = {n: _sum_slots([scattered(n, dx)], name="sum4_" + n) for n in done_early}

    gmod = jnp.concatenate([dsh1, dsc1, dga1, dsh2, dsc2, dga2], axis=-1).reshape(nb, N_MOD * dm)
    gmod_all = _allgather8(gmod, name="ag_gmod", after=tuple(partial.values())).reshape(N_DEV * nb, N_MOD * dm)
    gmod_sh = lax.dynamic_slice(gmod_all, (0, chip * ada_cols), (N_DEV * nb, ada_cols))
    g_w_ada = _mm(cond, gmod_sh, name="ada_dw", ta=True, tn=2048)

    out_g, out_d, out_m, out_v = {}, {}, {}, {}
    out_g["w_ada"], out_d["w_ada"], out_m["w_ada"], out_v["w_ada"] = _adamw(
        w_ada, m_w_ada, v_w_ada, [g_w_ada], name="adamw_w_ada")
    late = out_g["w_ada"]
    for n in done_early + ("w_in",):
        if n == "w_in":
            got = [scattered("in0", late), scattered("in1", late)]
            part = _sum_slots([jnp.concatenate(got, axis=1)], name="sum4_" + n)
        else:
            part = partial[n]
        other = _sibling_swap(part, name="swap_" + n)
        if n == "w_in":
            part, other = part[:, :ins], other[:, :ins]
        out_g[n], out_d[n], out_m[n], out_v[n] = _adamw(wts[n], mom1[n], mom2[n], [part, other], name="adamw_" + n)
        late = out_g[n]

    small_g = dict(b_ada=jnp.sum(gmod, axis=0), g_norm1=dg_norm1, g_v=dg_v, w_s=dw_s, b_s=db_col,
                   g_q_lat=dg_q_lat, g_kv_lat=dg_kv_lat, g_qn=dg_qn[:, :qkh], g_kn=dg_kn[:, :qkh],
                   g_norm2=dg_norm2)

    def pack(d):
        flat = jnp.concatenate([d[n].reshape(-1) for n in _SMALL])
        return jnp.pad(flat, (0, _round_up(flat.shape[0], 8 * LANE) - flat.shape[0])).reshape(-1, LANE)

    g8 = _allgather8(pack(small_g), name="ag_small_grads")
    sg, sd, sm, sv = _adamw(pack({n: wts[n] for n in _SMALL})[None], pack({n: mom1[n] for n in _SMALL})[None],
                            pack({n: mom2[n] for n in _SMALL})[None], [g8[d] for d in range(N_DEV)],
                            name="adamw_small")
    off = 0
    for n in _SMALL:
        size = wts[n].size
        for dst, src in ((out_g, sg), (out_d, sd), (out_m, sm), (out_v, sv)):
            dst[n] = src.reshape(-1)[off:off + size].reshape(wts[n].shape)
        off += size

    def shaped(d, n):
        return d[n].reshape(wts[n].shape)

    return (loss, grad_x, *[shaped(out_g, n) for n in _WEIGHTS], *[shaped(out_d, n) for n in _WEIGHTS],
            *[shaped(out_m, n) for n in _WEIGHTS], *[shaped(out_v, n) for n in _WEIGHTS])
```

```python
import functools
import math

import jax
import jax.numpy as jnp
from jax import lax
from jax.experimental import pallas as pl
from jax.experimental.pallas import tpu as pltpu

F32 = jnp.float32
BF16 = jnp.bfloat16

GM_WIDTH = 2048
GM_GROUPS = 8
CHUNK = 128
MLA_HEADS = 32
QK_NOPE = 128
QK_ROPE = 64
V_HEAD = 128
Q_LORA = 1024
KV_LORA = 512
ROPE_THETA = 10000.0
N_MOD = 6
EPS = 1e-6
ADAM_LR = 0.001
ADAM_B1 = 0.9
ADAM_B2 = 0.999
ADAM_EPS = 1e-08
ADAM_WD = 0.01
ADAM_STEP = 10

N_CHIP = 4
N_DEV = 8
LANE = 128
QK_PAD = 256
VMEM_LIMIT = 56 * 1024 * 1024
MESH = pl.DeviceIdType.MESH


def _round_up(n, m):
    return (n + m - 1) // m * m


def _tile(n, target, align):
    t = min(target, n) // align * align
    while t >= align:
        if n % t == 0:
            return t
        t -= align
    return n


def _params(*sem):
    return pltpu.CompilerParams(dimension_semantics=sem, vmem_limit_bytes=VMEM_LIMIT)


def _mesh_pos():
    return lax.axis_index("x"), lax.axis_index("y"), lax.axis_index("c")


def _flip(pos, k):
    ix, iy, ic = pos
    return (1 - ix if k & 4 else ix, 1 - iy if k & 2 else iy, 1 - ic if k & 1 else ic)


def _chip_of(pos):
    return 2 * pos[0] + pos[1]


def _window(ref, axis, chip, size):
    start = pl.multiple_of(chip * size, LANE if axis == 1 else 16)
    if axis == 1:
        return ref.at[:, pl.ds(start, size)]
    return ref.at[pl.ds(start, size), :]


def _remote(src, dst, sem, send, recv, peer):
    return pltpu.make_async_remote_copy(src_ref=src, dst_ref=dst, send_sem=sem.at[send], recv_sem=sem.at[recv],
                                        device_id=peer, device_id_type=MESH)


class _Gather:
    n_sem = 13

    def __init__(self, wsh, axis):
        self.axis, self.size, self.rows = axis, wsh.shape[axis], wsh.shape[0]
        self.inputs = [wsh]
        self.out_shape = [jax.ShapeDtypeStruct(
            tuple(d * N_CHIP if a == axis else d for a, d in enumerate(wsh.shape)), wsh.dtype)]

    def _half(self, ref, chip, c, local=False):
        h = self.rows // 2
        if self.axis == 1:
            rows = pl.ds(pl.multiple_of(c * h, 16), h)
            return ref.at[rows, :] if local else ref.at[rows, pl.ds(pl.multiple_of(chip * self.size, LANE), self.size)]
        base = 0 if local else chip * self.size
        return ref.at[pl.ds(pl.multiple_of(base + c * h, 16), h), :]

    def start(self, ins, outs, sem, s0):
        (w_ref,), (o_ref,) = ins, outs
        pos = _mesh_pos()
        chip, c = _chip_of(pos), pos[2]
        pltpu.make_async_copy(w_ref, _window(o_ref, self.axis, chip, self.size), sem.at[s0]).start()
        for j in range(1, N_CHIP):
            _remote(self._half(w_ref, chip, c, local=True), self._half(o_ref, chip, c), sem, s0 + j, s0 + 3 + j,
                    _flip(pos, 2 * j)).start()

    def _landed(self, outs, j):
        pos = _mesh_pos()
        peer = _flip(pos, 2 * j)
        return self._half(outs[0], _chip_of(peer), pos[2]), peer

    def arrive(self, ins, outs, sem, s0):
        for j in range(1, N_CHIP):
            blk, peer = self._landed(outs, j)
            _remote(blk, blk, sem, s0 + j, s0 + 3 + j, peer).wait_recv()

    def forward(self, ins, outs, sem, s0):
        for j in range(1, N_CHIP):
            blk, _ = self._landed(outs, j)
            _remote(blk, blk, sem, s0 + 6 + j, s0 + 9 + j, _flip(_mesh_pos(), 1)).start()

    def relay(self, ins, outs, sem, s0):
        for j in range(1, N_CHIP):
            blk, peer = self._landed(outs, j)
            _remote(blk, blk, sem, s0 + j, s0 + 3 + j, peer).wait_recv()
            _remote(blk, blk, sem, s0 + 6 + j, s0 + 9 + j, _flip(_mesh_pos(), 1)).start()

    def finish(self, ins, outs, sem, s0):
        (w_ref,), (o_ref,) = ins, outs
        pos = _mesh_pos()
        chip, c = _chip_of(pos), pos[2]
        sibling = _flip(pos, 1)
        for j in range(1, N_CHIP):
            peer = _flip(pos, 2 * j)
            _remote(self._half(w_ref, chip, c, local=True), self._half(o_ref, chip, c), sem, s0 + j, s0 + 3 + j,
                    peer).wait_send()
            blk = self._half(o_ref, _chip_of(peer), c)
            _remote(blk, blk, sem, s0 + 6 + j, s0 + 9 + j, sibling).wait_send()
            got = self._half(o_ref, _chip_of(peer), 1 - c)
            _remote(got, got, sem, s0 + 6 + j, s0 + 9 + j, sibling).wait_recv()
        pltpu.make_async_copy(w_ref, _window(o_ref, self.axis, chip, self.size), sem.at[s0]).wait()


class _Scatter:
    def __init__(self, g, axis, peers, own):
        self.axis, self.size, self.peers, self.own = axis, g.shape[axis] // N_CHIP, tuple(peers), int(own)
        shard = tuple(self.size if a == axis else d for a, d in enumerate(g.shape))
        self.n_sem = self.own + 2 * len(self.peers)
        self.inputs = [g]
        self.out_shape = [jax.ShapeDtypeStruct((self.own + len(self.peers),) + shard, g.dtype)]

    def _copies(self, ins, outs, sem, s0):
        (g_ref,), (o_ref,) = ins, outs
        pos = _mesh_pos()
        cps = []
        if self.own:
            cps.append(pltpu.make_async_copy(_window(g_ref, self.axis, _chip_of(pos), self.size), o_ref.at[0],
                                             sem.at[s0]))
        for n, j in enumerate(self.peers):
            peer = _flip(pos, 2 * j)
            cps.append(_remote(_window(g_ref, self.axis, _chip_of(peer), self.size), o_ref.at[self.own + n], sem,
                               s0 + self.own + 2 * n, s0 + self.own + 2 * n + 1, peer))
        return cps

    def start(self, ins, outs, sem, s0):
        for cp in self._copies(ins, outs, sem, s0):
            cp.start()

    def relay(self, ins, outs, sem, s0):
        pass

    def finish(self, ins, outs, sem, s0):
        for cp in self._copies(ins, outs, sem, s0):
            cp.wait()


class _Swap:
    n_sem = 2

    def __init__(self, p):
        self.inputs = [p]
        self.out_shape = [jax.ShapeDtypeStruct(p.shape, p.dtype)]

    def _copy(self, ins, outs, sem, s0):
        return _remote(ins[0], outs[0], sem, s0, s0 + 1, _flip(_mesh_pos(), 1))

    def start(self, ins, outs, sem, s0):
        self._copy(ins, outs, sem, s0).start()

    def relay(self, ins, outs, sem, s0):
        pass

    def finish(self, ins, outs, sem, s0):
        self._copy(ins, outs, sem, s0).wait()


class _Side:
    def __init__(self, jobs):
        self.jobs = list(jobs)
        self.inputs = [a for job in self.jobs for a in job.inputs]
        self.out_shape = [s for job in self.jobs for s in job.out_shape]

    def scratch(self):
        return [pltpu.SemaphoreType.DMA((sum(job.n_sem for job in self.jobs),))]

    def _run(self, phase, ins, outs, sem):
        i = o = s = 0
        for job in self.jobs:
            ni, no = len(job.inputs), len(job.out_shape)
            getattr(job, phase)(ins[i:i + ni], outs[o:o + no], sem, s)
            i, o, s = i + ni, o + no, s + job.n_sem

    def begin(self, ins, outs, sem, step, nsteps):
        if nsteps == 1:
            self._run("start", ins, outs, sem)
        else:
            pl.when(step == 0)(lambda: self._run("start", ins, outs, sem))

    def end(self, ins, outs, sem, step, nsteps):
        if nsteps == 1:
            self._run("relay", ins, outs, sem)
            self._run("finish", ins, outs, sem)
        else:
            pl.when(step == nsteps - max(1, nsteps // 8))(lambda: self._run("relay", ins, outs, sem))
            pl.when(step == nsteps - 1)(lambda: self._run("finish", ins, outs, sem))


def _comm_call(side, *, name):
    n_in = len(side.inputs)

    def body(*refs):
        ins, outs, sem = refs[:n_in], refs[n_in:-1], refs[-1]
        side.begin(ins, outs, sem, 0, 1)
        side.end(ins, outs, sem, 0, 1)

    any_spec = pl.BlockSpec(memory_space=pl.ANY)
    return pl.pallas_call(body, name=name, out_shape=list(side.out_shape), in_specs=[any_spec] * n_in,
                          out_specs=[any_spec] * len(side.out_shape), scratch_shapes=side.scratch())(*side.inputs)


_HBM = pl.BlockSpec(memory_space=pltpu.HBM)
_SEM = pl.BlockSpec(memory_space=pltpu.SEMAPHORE)
_DATAFLOW = pltpu.SideEffectType.DATAFLOW_SIDE_EFFECTING


def _split_start(side, *, name, after=()):
    after = tuple(after)
    ni, no = len(side.inputs), len(side.out_shape)
    bufs = list(side.inputs) + [lax.empty(s.shape, s.dtype) for s in side.out_shape]

    def body(*refs):
        side._run("start", refs[:ni], refs[ni:ni + no], refs[ni + no + len(after)])
        refs[-1][...] = jnp.zeros_like(refs[-1])

    res = pl.pallas_call(
        body, name=name,
        out_shape=[side.scratch()[0]] + [pltpu.HBM(b.shape, b.dtype) for b in bufs]
        + [jax.ShapeDtypeStruct((8, LANE), F32)],
        in_specs=[_HBM] * (ni + no) + [pl.BlockSpec(memory_space=pl.ANY)] * len(after),
        out_specs=[_SEM] + [_HBM] * (ni + no) + [pl.BlockSpec(memory_space=pltpu.VMEM)],
        input_output_aliases={i: 1 + i for i in range(ni + no)},
        compiler_params=pltpu.CompilerParams(has_side_effects=_DATAFLOW),
    )(*[pltpu.with_memory_space_constraint(b, pltpu.HBM) for b in bufs], *after)
    return (side, res[0], list(res[1:-1])), res[-1]


def _split_step(state, phase, *, name, after=()):
    side, sem, bufs = state
    after = tuple(after)
    ni, n = len(side.inputs), len(bufs)

    def body(*refs):
        side._run(phase, refs[:ni], refs[ni:n], refs[n])

    res = pl.pallas_call(
        body, name=name,
        out_shape=[pltpu.HBM(b.shape, b.dtype) for b in bufs],
        in_specs=[_HBM] * n + [_SEM] + [pl.BlockSpec(memory_space=pl.ANY)] * len(after),
        out_specs=[_HBM] * n,
        input_output_aliases={i: i for i in range(n)},
        compiler_params=pltpu.CompilerParams(has_side_effects=_DATAFLOW),
    )(*bufs, sem, *after)
    return (side, sem, list(res))


def _split_results(state):
    side, _, bufs = state
    return bufs[len(side.inputs):]


def _refs_split(refs, counts):
    out, p = [], 0
    for n in counts:
        out.append(refs[p:p + n])
        p += n
    return out


def _mm(a, b, *, name, ta=False, tb=False, out_dtypes=(F32,), epilogue=None, extras=(),
        tm=1024, tn=1024, tk=1024, m_rows=None, m_start=0, side=None, after=()):
    after = tuple(after)
    m, k = (a.shape[1], a.shape[0]) if ta else a.shape
    n = b.shape[0] if tb else b.shape[1]
    assert k == (b.shape[1] if tb else b.shape[0]), (a.shape, b.shape)
    m = m if m_rows is None else m_rows
    tm = _tile(m, tm, LANE if ta else 16)
    tn = _tile(n, tn, LANE)
    tk = _tile(k, tk, LANE)
    assert m_start % tm == 0
    mo = m_start // tm
    gi, gj, nk = m // tm, n // tn, k // tk
    nsteps = gi * gj * nk
    ne, no = len(extras), len(out_dtypes)
    ns_in, ns_out = (len(side.inputs), len(side.out_shape)) if side else (0, 0)
    dims = (((0 if ta else 1,), (1 if tb else 0,)), ((), ()))

    def body(*refs):
        (a_ref, b_ref), ex, sins, _, outs, souts, (acc,), sems = _refs_split(
            refs, (2, ne, ns_in, len(after), no, ns_out, 1, 1 if side else 0))
        kk = pl.program_id(2)
        step = (pl.program_id(0) * gj + pl.program_id(1)) * nk + kk
        if side:
            side.begin(sins, souts, sems[0], step, nsteps)

        def prod():
            return lax.dot_general(a_ref[...].astype(BF16), b_ref[...].astype(BF16), dims,
                                   preferred_element_type=F32)

        def emit(r):
            res = epilogue(r, *[e[...] for e in ex]) if epilogue is not None else (r,)
            for o, val in zip(outs, res):
                o[...] = val.astype(o.dtype)

        if nk == 1:
            emit(prod())
        else:
            @pl.when(kk == 0)
            def _():
                acc[...] = prod()

            @pl.when((kk > 0) & (kk < nk - 1))
            def _():
                acc[...] += prod()

            @pl.when(kk == nk - 1)
            def _():
                emit(acc[...] + prod())

        if side:
            side.end(sins, souts, sems[0], step, nsteps)

    a_spec = (pl.BlockSpec((tk, tm), lambda i, j, q: (q, i + mo)) if ta
              else pl.BlockSpec((tm, tk), lambda i, j, q: (i + mo, q)))
    b_spec = pl.BlockSpec((tn, tk), lambda i, j, q: (j, q)) if tb else pl.BlockSpec((tk, tn), lambda i, j, q: (q, j))
    o_spec = pl.BlockSpec((tm, tn), lambda i, j, q: (i, j))
    any_spec = pl.BlockSpec(memory_space=pl.ANY)
    res = pl.pallas_call(
        body, name=name,
        grid=(gi, gj, nk),
        in_specs=[a_spec, b_spec] + [o_spec] * ne + [any_spec] * (ns_in + len(after)),
        out_specs=[o_spec] * no + [any_spec] * ns_out,
        out_shape=[jax.ShapeDtypeStruct((m, n), dt) for dt in out_dtypes] + (list(side.out_shape) if side else []),
        scratch_shapes=[pltpu.VMEM((tm, tn), F32)] + (side.scratch() if side else []),
        compiler_params=_params(*(("arbitrary",) * 3 if side else ("parallel", "parallel", "arbitrary"))),
    )(a, b, *extras, *(side.inputs if side else ()), *after)
    return res[0] if len(res) == 1 else res


def _relu2_epilogue(r):
    return r, jnp.square(jnp.maximum(r, 0.0))


def _relu2_bwd_epilogue(r, f1):
    return (r * (2.0 * jnp.maximum(f1.astype(F32), 0.0)),)


def _row_tile(seq, width, nbytes=2 * 1024 * 1024):
    return _tile(seq, max(8, nbytes // (4 * width)), 8)


def _silu(c_all):
    def body(c_ref, o_ref):
        v = c_ref[...]
        o_ref[...] = v * jax.nn.sigmoid(v)

    return pl.pallas_call(body, name="silu", out_shape=jax.ShapeDtypeStruct(c_all.shape, F32))(c_all)


def _norm_mod_fwd(x, g, shift, scale, seq, *, name, resid=None):
    t, w = x.shape
    tr = _row_tile(seq, w, (1 if resid is not None else 2) * 1024 * 1024)
    nb = seq // tr
    has_res = resid is not None

    def body(*refs):
        if has_res:
            x_ref, o_ref, gate_ref, g_ref, sh_ref, sc_ref, x1_ref, h_ref = refs
            xv = x_ref[...] + gate_ref[0] * o_ref[...]
            x1_ref[...] = xv
        else:
            x_ref, g_ref, sh_ref, sc_ref, h_ref = refs
            xv = x_ref[...]
        r = lax.rsqrt(jnp.mean(xv * xv, axis=-1, keepdims=True) + EPS)
        nrm = xv * r * g_ref[...]
        h_ref[...] = (nrm * (1.0 + sc_ref[0]) + sh_ref[0]).astype(BF16)

    row = pl.BlockSpec((tr, w), lambda i: (i, 0))
    vec = pl.BlockSpec((1, w), lambda i: (0, 0))
    per_ex = pl.BlockSpec((1, 1, w), lambda i: (i // nb, 0, 0))
    if has_res:
        o, gate = resid
        ins, in_specs = (x, o, gate, g, shift, scale), [row, row, per_ex, vec, per_ex, per_ex]
        out_shape = [jax.ShapeDtypeStruct((t, w), F32), jax.ShapeDtypeStruct((t, w), BF16)]
        out_specs = [row, row]
    else:
        ins, in_specs = (x, g, shift, scale), [row, vec, per_ex, per_ex]
        out_shape = jax.ShapeDtypeStruct((t, w), BF16)
        out_specs = row
    return pl.pallas_call(body, name=name, grid=(t // tr,), in_specs=in_specs, out_specs=out_specs,
                          out_shape=out_shape, compiler_params=_params("parallel"))(*ins)


def _norm_mod_bwd(x, g, scale, dh, seq, *, name, dres=None, gate_o=None):
    t, w = x.shape
    nex = t // seq
    tr = _row_tile(seq, w, 1024 * 1024)
    nb = seq // tr
    has_res, has_gate = dres is not None, gate_o is not None

    def body(*refs):
        refs = list(refs)
        x_ref, g_ref, sc_ref, dh_ref = refs[:4]
        p = 4
        dres_ref = None
        if has_res:
            dres_ref = refs[p]
            p += 1
        if has_gate:
            o_ref, gate_ref = refs[p:p + 2]
            p += 2
        dx_ref, dsh_ref, dsc_ref, dg_ref = refs[p:p + 4]
        p += 4
        i = pl.program_id(0)

        @pl.when(i % nb == 0)
        def _():
            dsh_ref[...] = jnp.zeros_like(dsh_ref)
            dsc_ref[...] = jnp.zeros_like(dsc_ref)
            if has_gate:
                refs[p + 1][...] = jnp.zeros_like(refs[p + 1])

        @pl.when(i == 0)
        def _():
            dg_ref[...] = jnp.zeros_like(dg_ref)

        xv = x_ref[...]
        gv = g_ref[...]
        dhv = dh_ref[...]
        r = lax.rsqrt(jnp.mean(xv * xv, axis=-1, keepdims=True) + EPS)
        xh = xv * r
        dsh_ref[0] += jnp.sum(dhv, axis=0, keepdims=True)
        dsc_ref[0] += jnp.sum(dhv * (xh * gv), axis=0, keepdims=True)
        dn = dhv * (1.0 + sc_ref[0])
        dg_ref[...] += jnp.sum(dn * xh, axis=0, keepdims=True)
        dxh = dn * gv
        dx = r * (dxh - xh * jnp.mean(dxh * xh, axis=-1, keepdims=True))
        if has_res:
            dx = dx + dres_ref[...]
        dx_ref[...] = dx
        if has_gate:
            do_ref, dgate_ref = refs[p:p + 2]
            do_ref[...] = (dx * gate_ref[0]).astype(BF16)
            dgate_ref[0] += jnp.sum(dx * o_ref[...], axis=0, keepdims=True)

    row = pl.BlockSpec((tr, w), lambda i: (i, 0))
    vec = pl.BlockSpec((1, w), lambda i: (0, 0))
    per_ex = pl.BlockSpec((1, 1, w), lambda i: (i // nb, 0, 0))
    ins, in_specs = [x, g, scale, dh], [row, vec, per_ex, row]
    if has_res:
        ins.append(dres)
        in_specs.append(row)
    if has_gate:
        ins += list(gate_o)
        in_specs += [row, per_ex]
    ex_shape = jax.ShapeDtypeStruct((nex, 1, w), F32)
    out_shape = [jax.ShapeDtypeStruct((t, w), F32), ex_shape, ex_shape, jax.ShapeDtypeStruct((1, w), F32)]
    out_specs = [row, per_ex, per_ex, vec]
    if has_gate:
        out_shape += [jax.ShapeDtypeStruct((t, w), BF16), ex_shape]
        out_specs += [row, per_ex]
    return pl.pallas_call(body, name=name, grid=(t // tr,), in_specs=in_specs, out_specs=out_specs,
                          out_shape=out_shape, compiler_params=_params("arbitrary"))(*ins)


def _gelu_parts(xv):
    cdf = 0.5 * (1.0 + lax.erf(xv * (1.0 / math.sqrt(2.0))))
    return cdf


def _tril_mask():
    r = lax.broadcasted_iota(jnp.int32, (CHUNK, CHUNK), 0)
    c = lax.broadcasted_iota(jnp.int32, (CHUNK, CHUNK), 1)
    return c <= r


def _gmlp_fwd(uv, g_v, w_s, b_col):
    t = uv.shape[0]
    gw = GM_WIDTH // GM_GROUPS
    nck = 2
    tr = nck * CHUNK

    def body(uv_ref, gv_ref, w_ref, b_ref, o_ref):
        mask = _tril_mask()
        for ck in range(nck):
            rows = pl.ds(ck * CHUNK, CHUNK)
            xv = uv_ref[rows, :]
            z = xv * _gelu_parts(xv)
            u, v = z[:, :GM_WIDTH], z[:, GM_WIDTH:]
            r = lax.rsqrt(jnp.mean(v * v, axis=-1, keepdims=True) + EPS)
            vn = (v * r * gv_ref[...]).astype(BF16)
            for gi in range(GM_GROUPS):
                cols = slice(gi * gw, (gi + 1) * gw)
                wg = jnp.where(mask, w_ref[gi], 0.0).astype(BF16)
                mx = jnp.dot(wg, vn[:, cols], preferred_element_type=F32) + b_ref[gi]
                o_ref[rows, cols] = (u[:, cols] * mx).astype(BF16)

    return pl.pallas_call(
        body, name="gmlp_fwd", grid=(t // tr,),
        in_specs=[pl.BlockSpec((tr, 2 * GM_WIDTH), lambda i: (i, 0)),
                  pl.BlockSpec((1, GM_WIDTH), lambda i: (0, 0)),
                  pl.BlockSpec((GM_GROUPS, CHUNK, CHUNK), lambda i: (0, 0, 0)),
                  pl.BlockSpec((GM_GROUPS, CHUNK, 1), lambda i: (0, 0, 0))],
        out_specs=pl.BlockSpec((tr, GM_WIDTH), lambda i: (i, 0)),
        out_shape=jax.ShapeDtypeStruct((t, GM_WIDTH), BF16),
        compiler_params=_params("parallel"))(uv, g_v, w_s, b_col)


def _gmlp_bwd(uv, g_v, w_s, b_col, da):
    t = uv.shape[0]
    gw = GM_WIDTH // GM_GROUPS
    nck = 2
    tr = nck * CHUNK
    inv_sqrt_2pi = 1.0 / math.sqrt(2.0 * math.pi)

    def body(uv_ref, gv_ref, w_ref, b_ref, da_ref, duv_ref, dgv_ref, dw_ref, db_ref, dvn_ref):
        @pl.when(pl.program_id(0) == 0)
        def _():
            dgv_ref[...] = jnp.zeros_like(dgv_ref)
            dw_ref[...] = jnp.zeros_like(dw_ref)
            db_ref[...] = jnp.zeros_like(db_ref)

        mask = _tril_mask()
        for ck in range(nck):
            rows = pl.ds(ck * CHUNK, CHUNK)
            xv = uv_ref[rows, :]
            cdf = _gelu_parts(xv)
            z = xv * cdf
            u, v = z[:, :GM_WIDTH], z[:, GM_WIDTH:]
            r = lax.rsqrt(jnp.mean(v * v, axis=-1, keepdims=True) + EPS)
            vh = v * r
            gv = gv_ref[...]
            vn = (vh * gv).astype(BF16)
            dav = da_ref[rows, :]
            du_parts = []
            for gi in range(GM_GROUPS):
                cols = slice(gi * gw, (gi + 1) * gw)
                wg = jnp.where(mask, w_ref[gi], 0.0).astype(BF16)
                vng = vn[:, cols]
                mx = jnp.dot(wg, vng, preferred_element_type=F32) + b_ref[gi]
                du_parts.append(dav[:, cols] * mx)
                dmx = dav[:, cols] * u[:, cols]
                db_ref[gi] += jnp.sum(dmx, axis=1, keepdims=True)
                dmb = dmx.astype(BF16)
                dwg = lax.dot_general(dmb, vng, (((1,), (1,)), ((), ())), preferred_element_type=F32)
                dw_ref[gi] += jnp.where(mask, dwg, 0.0)
                dvn_ref[:, cols] = lax.dot_general(wg, dmb, (((0,), (0,)), ((), ())),
                                                   preferred_element_type=F32)
            dvn = dvn_ref[...]
            dgv_ref[...] += jnp.sum(dvn * vh, axis=0, keepdims=True)
            dvh = dvn * gv
            dv = r * (dvh - vh * jnp.mean(dvh * vh, axis=-1, keepdims=True))
            dz = jnp.concatenate(du_parts + [dv], axis=1)
            dgelu = cdf + xv * (jnp.exp(-0.5 * xv * xv) * inv_sqrt_2pi)
            duv_ref[rows, :] = (dz * dgelu).astype(BF16)

    return pl.pallas_call(
        body, name="gmlp_bwd", grid=(t // tr,),
        in_specs=[pl.BlockSpec((tr, 2 * GM_WIDTH), lambda i: (i, 0)),
                  pl.BlockSpec((1, GM_WIDTH), lambda i: (0, 0)),
                  pl.BlockSpec((GM_GROUPS, CHUNK, CHUNK), lambda i: (0, 0, 0)),
                  pl.BlockSpec((GM_GROUPS, CHUNK, 1), lambda i: (0, 0, 0)),
                  pl.BlockSpec((tr, GM_WIDTH), lambda i: (i, 0))],
        out_specs=[pl.BlockSpec((tr, 2 * GM_WIDTH), lambda i: (i, 0)),
                   pl.BlockSpec((1, GM_WIDTH), lambda i: (0, 0)),
                   pl.BlockSpec((GM_GROUPS, CHUNK, CHUNK), lambda i: (0, 0, 0)),
                   pl.BlockSpec((GM_GROUPS, CHUNK, 1), lambda i: (0, 0, 0))],
        out_shape=[jax.ShapeDtypeStruct((t, 2 * GM_WIDTH), BF16),
                   jax.ShapeDtypeStruct((1, GM_WIDTH), F32),
                   jax.ShapeDtypeStruct((GM_GROUPS, CHUNK, CHUNK), F32),
                   jax.ShapeDtypeStruct((GM_GROUPS, CHUNK, 1), F32)],
        scratch_shapes=[pltpu.VMEM((CHUNK, GM_WIDTH), F32)],
        compiler_params=_params("arbitrary"))(uv, g_v, w_s, b_col, da)


def _rope_tables(pos, invf):
    nb, s, _ = pos.shape
    ts = _tile(s, 512, 8)
    half = QK_ROPE // 2

    def body(pos_ref, invf_ref, o_ref):
        ang = pos_ref[0] * invf_ref[...]
        cs, sn = jnp.cos(ang), jnp.sin(ang)
        lane = lax.broadcasted_iota(jnp.int32, (1, LANE), 1)
        o_ref[0, :, :LANE] = jnp.where(lane < QK_ROPE, cs, 0.0)
        o_ref[0, :, LANE:2 * LANE] = jnp.where((lane >= half) & (lane < QK_ROPE), sn, 0.0)
        o_ref[0, :, 2 * LANE:] = jnp.where(lane < half, -sn, 0.0)

    return pl.pallas_call(
        body, name="rope_tables", grid=(nb, s // ts),
        in_specs=[pl.BlockSpec((1, ts, 1), lambda b, i: (b, i, 0)), pl.BlockSpec((1, LANE), lambda b, i: (0, 0))],
        out_specs=pl.BlockSpec((1, ts, 3 * LANE), lambda b, i: (b, i, 0)),
        out_shape=jax.ShapeDtypeStruct((nb, s, 3 * LANE), F32),
        compiler_params=_params("parallel", "parallel"))(pos, invf)


def _head_rstd(lo, hi):
    ss = jnp.sum(lo * lo, axis=-1, keepdims=True) + jnp.sum(hi * hi, axis=-1, keepdims=True)
    return lax.rsqrt(ss * (1.0 / (QK_NOPE + QK_ROPE)) + EPS)


def _head_specs(ts, kpe):
    if kpe is None:
        return [pl.BlockSpec((ts, QK_PAD), lambda r, h: (r, h))]
    return [pl.BlockSpec((ts, QK_NOPE), lambda r, h: (r, 2 * h)), pl.BlockSpec((ts, LANE), lambda r, h: (r, 0))]


def _head_tiles(x_refs):
    if len(x_refs) == 1:
        return x_refs[0][:, :QK_NOPE], x_refs[0][:, QK_NOPE:]
    return x_refs[0][...], x_refs[1][...]


def _qk_norm_fwd(x, kpe, g_pad, tabs, *, name):
    t = x.shape[0]
    ts = _tile(t, 1024, 8)
    half = QK_ROPE // 2
    nx = 1 if kpe is None else 2

    def body(*refs):
        x_refs, (g_ref, tab_ref, o_ref) = refs[:nx], refs[nx:]
        lo, hi = _head_tiles(x_refs)
        r = _head_rstd(lo, hi)
        hi = hi * r * g_ref[:, QK_NOPE:]
        hi = (hi * tab_ref[:, :LANE] + pltpu.roll(hi, half, 1) * tab_ref[:, LANE:2 * LANE]
              + pltpu.roll(hi, LANE - half, 1) * tab_ref[:, 2 * LANE:])
        o_ref[:, :QK_NOPE] = (lo * r * g_ref[:, :QK_NOPE]).astype(BF16)
        o_ref[:, QK_NOPE:] = hi.astype(BF16)

    return pl.pallas_call(
        body, name=name, grid=(t // ts, MLA_HEADS),
        in_specs=_head_specs(ts, kpe) + [pl.BlockSpec((1, QK_PAD), lambda r, h: (0, 0)),
                                         pl.BlockSpec((ts, 3 * LANE), lambda r, h: (r, 0))],
        out_specs=pl.BlockSpec((ts, QK_PAD), lambda r, h: (r, h)),
        out_shape=jax.ShapeDtypeStruct((t, MLA_HEADS * QK_PAD), BF16),
        compiler_params=_params("parallel", "parallel"))(*((x,) if kpe is None else (x, kpe)), g_pad, tabs)


def _qk_norm_bwd(x, kpe, g_pad, tabs, dout, dv, *, name):
    t = x.shape[0]
    ts = _tile(t, 1024, 8)
    half = QK_ROPE // 2
    is_k = kpe is not None
    nx = 2 if is_k else 1
    inv_width = 1.0 / (QK_NOPE + QK_ROPE)

    def body(*refs):
        x_refs, (g_ref, tab_ref, do_ref), rest = refs[:nx], refs[nx:nx + 3], refs[nx + 3:]
        if is_k:
            dv_ref, dx_ref, dkpe_ref, dg_ref = rest
        else:
            dx_ref, dg_ref = rest

        @pl.when((pl.program_id(0) == 0) & (pl.program_id(1) == 0))
        def _():
            dg_ref[...] = jnp.zeros_like(dg_ref)

        lo, hi = _head_tiles(x_refs)
        r = _head_rstd(lo, hi)
        lo, hi = lo * r, hi * r
        dlo, dhi = do_ref[:, :QK_NOPE], do_ref[:, QK_NOPE:]
        dhi = (dhi * tab_ref[:, :LANE] + pltpu.roll(dhi * tab_ref[:, LANE:2 * LANE], LANE - half, 1)
               + pltpu.roll(dhi * tab_ref[:, 2 * LANE:], half, 1))
        dg_ref[:, :QK_NOPE] += jnp.sum(dlo * lo, axis=0, keepdims=True)
        dg_ref[:, QK_NOPE:] += jnp.sum(dhi * hi, axis=0, keepdims=True)
        dlo, dhi = dlo * g_ref[:, :QK_NOPE], dhi * g_ref[:, QK_NOPE:]
        mean = (jnp.sum(dlo * lo, axis=-1, keepdims=True) + jnp.sum(dhi * hi, axis=-1, keepdims=True)) * inv_width
        dx_ref[:, :QK_NOPE] = (r * (dlo - lo * mean)).astype(BF16)
        dxhi = r * (dhi - hi * mean)
        if is_k:
            dx_ref[:, QK_NOPE:] = dv_ref[...].astype(BF16)

            @pl.when(pl.program_id(1) == 0)
            def _():
                dkpe_ref[...] = jnp.zeros_like(dkpe_ref)

            dkpe_ref[...] += dxhi
        else:
            dx_ref[:, QK_NOPE:] = dxhi.astype(BF16)

    head = pl.BlockSpec((ts, QK_PAD), lambda r, h: (r, h))
    vec = pl.BlockSpec((1, QK_PAD), lambda r, h: (0, 0))
    in_specs = _head_specs(ts, kpe) + [vec, pl.BlockSpec((ts, 3 * LANE), lambda r, h: (r, 0)), head]
    ins = [x] + ([kpe] if is_k else []) + [g_pad, tabs, dout]
    out_specs = [head]
    out_shape = [jax.ShapeDtypeStruct((t, MLA_HEADS * QK_PAD), BF16)]
    if is_k:
        ins.append(dv)
        in_specs.append(pl.BlockSpec((ts, V_HEAD), lambda r, h: (r, h)))
        out_specs.append(pl.BlockSpec((ts, LANE), lambda r, h: (r, 0)))
        out_shape.append(jax.ShapeDtypeStruct((t, LANE), F32))
    out_specs.append(vec)
    out_shape.append(jax.ShapeDtypeStruct((1, QK_PAD), F32))
    return pl.pallas_call(body, name=name, grid=(t // ts, MLA_HEADS), in_specs=in_specs, out_specs=out_specs,
                          out_shape=out_shape, compiler_params=_params("arbitrary", "arbitrary"))(*ins)


def _causal_probs(q, k, q0):
    scale = (QK_NOPE + QK_ROPE) ** -0.5
    sc = lax.dot_general(q, k, (((1,), (1,)), ((), ())), preferred_element_type=F32) * scale
    qi = q0 + lax.broadcasted_iota(jnp.int32, sc.shape, 0)
    ki = lax.broadcasted_iota(jnp.int32, sc.shape, 1)
    sc = jnp.where(ki <= qi, sc, -1e30)
    e = jnp.exp(sc - jnp.max(sc, axis=-1, keepdims=True))
    return e / jnp.sum(e, axis=-1, keepdims=True)


def _attn_specs(s, tq):
    nq = s // tq
    q_spec = pl.BlockSpec((tq, QK_PAD), lambda b, h, i: (b * nq + i, h))
    o_spec = pl.BlockSpec((tq, V_HEAD), lambda b, h, i: (b * nq + i, h))
    k_spec = pl.BlockSpec((s, QK_PAD), lambda b, h, i: (b, h))
    v_spec = pl.BlockSpec((s, V_HEAD), lambda b, h, i: (b, 2 * h + 1))
    dv_spec = pl.BlockSpec((s, V_HEAD), lambda b, h, i: (b, h))
    return q_spec, o_spec, k_spec, v_spec, dv_spec


def _attn_fwd(qn, kn, kv_raw, nb):
    t = qn.shape[0]
    s = t // nb
    tq = _tile(s, 256, 8)
    nq = s // tq
    q_spec, o_spec, k_spec, v_spec, _ = _attn_specs(s, tq)

    def body(q_ref, k_ref, v_ref, o_ref):
        def block(g):
            keys = (g + 1) * tq
            p = _causal_probs(q_ref[...], k_ref[:keys, :], g * tq)
            o_ref[...] = jnp.dot(p.astype(BF16), v_ref[:keys, :].astype(BF16),
                                 preferred_element_type=F32).astype(BF16)

        for g in range(nq):
            pl.when(pl.program_id(2) == g)(functools.partial(block, g))

    return pl.pallas_call(
        body, name="attn_fwd", grid=(nb, MLA_HEADS, nq),
        in_specs=[q_spec, k_spec, v_spec], out_specs=o_spec,
        out_shape=jax.ShapeDtypeStruct((t, MLA_HEADS * V_HEAD), BF16),
        compiler_params=_params("parallel", "parallel", "parallel"))(qn, kn, kv_raw)


def _attn_bwd(qn, kn, kv_raw, dattn, nb):
    t = qn.shape[0]
    s = t // nb
    tq = _tile(s, 256, 8)
    nq = s // tq
    scale = (QK_NOPE + QK_ROPE) ** -0.5
    q_spec, o_spec, k_spec, v_spec, dv_spec = _attn_specs(s, tq)

    def body(q_ref, k_ref, v_ref, do_ref, dq_ref, dk_ref, dv_ref):
        @pl.when(pl.program_id(2) == 0)
        def _():
            dk_ref[...] = jnp.zeros_like(dk_ref)
            dv_ref[...] = jnp.zeros_like(dv_ref)

        def block(g):
            keys = (g + 1) * tq
            q, k, v, do = q_ref[...], k_ref[:keys, :], v_ref[:keys, :].astype(BF16), do_ref[...]
            p = _causal_probs(q, k, g * tq)
            dv_ref[:keys, :] += lax.dot_general(p.astype(BF16), do, (((0,), (0,)), ((), ())),
                                                preferred_element_type=F32)
            dp = lax.dot_general(do, v, (((1,), (1,)), ((), ())), preferred_element_type=F32)
            ds = (p * (dp - jnp.sum(p * dp, axis=-1, keepdims=True)) * scale).astype(BF16)
            dq_ref[...] = jnp.dot(ds, k, preferred_element_type=F32)
            dk_ref[:keys, :] += lax.dot_general(ds, q, (((0,), (0,)), ((), ())), preferred_element_type=F32)

        for g in range(nq):
            pl.when(pl.program_id(2) == g)(functools.partial(block, g))

    return pl.pallas_call(
        body, name="attn_bwd", grid=(nb, MLA_HEADS, nq),
        in_specs=[q_spec, k_spec, v_spec, o_spec],
        out_specs=[q_spec, k_spec, dv_spec],
        out_shape=[jax.ShapeDtypeStruct((t, MLA_HEADS * QK_PAD), F32),
                   jax.ShapeDtypeStruct((t, MLA_HEADS * QK_PAD), F32),
                   jax.ShapeDtypeStruct((t, MLA_HEADS * V_HEAD), F32)],
        compiler_params=_params("parallel", "parallel", "arbitrary"))(qn, kn, kv_raw, dattn)


def _gatemix_fwd(gpa, gpb, ya, yb):
    t, w = ya.shape
    tr = _tile(t, 128, 8)

    def body(ga_ref, gb_ref, ya_ref, yb_ref, o_ref):
        o_ref[...] = (jax.nn.sigmoid(ga_ref[...]) * ya_ref[...]
                      + jax.nn.sigmoid(gb_ref[...]) * yb_ref[...]).astype(BF16)

    row = pl.BlockSpec((tr, w), lambda i: (i, 0))
    return pl.pallas_call(body, name="gatemix_fwd", grid=(t // tr,), in_specs=[row] * 4, out_specs=row,
                          out_shape=jax.ShapeDtypeStruct((t, w), BF16),
                          compiler_params=_params("parallel"))(gpa, gpb, ya, yb)


def _gatemix_bwd(gpa, gpb, ya, yb, dmix):
    t, w = ya.shape
    tr = _tile(t, 128, 8)

    def body(ga_ref, gb_ref, ya_ref, yb_ref, dm_ref, dya_ref, dyb_ref, dga_ref, dgb_ref):
        dm = dm_ref[...]
        sa = jax.nn.sigmoid(ga_ref[...])
        sb = jax.nn.sigmoid(gb_ref[...])
        dya_ref[...] = (dm * sa).astype(BF16)
        dyb_ref[...] = (dm * sb).astype(BF16)
        dga_ref[...] = (dm * ya_ref[...] * sa * (1.0 - sa)).astype(BF16)
        dgb_ref[...] = (dm * yb_ref[...] * sb * (1.0 - sb)).astype(BF16)

    row = pl.BlockSpec((tr, w), lambda i: (i, 0))
    return pl.pallas_call(body, name="gatemix_bwd", grid=(t // tr,), in_specs=[row] * 5, out_specs=[row] * 4,
                          out_shape=[jax.ShapeDtypeStruct((t, w), BF16)] * 4,
                          compiler_params=_params("parallel"))(gpa, gpb, ya, yb, dmix)


def _final(x1, f2, gate, target, seq):
    t, w = x1.shape
    nex = t // seq
    tr = _row_tile(seq, w, 1024 * 1024)
    nb = seq // tr

    def body(x_ref, f_ref, gate_ref, t_ref, loss_ref, dy_ref, dff_ref, dgate_ref):
        i = pl.program_id(0)

        @pl.when(i == 0)
        def _():
            loss_ref[...] = jnp.zeros_like(loss_ref)

        @pl.when(i % nb == 0)
        def _():
            dgate_ref[...] = jnp.zeros_like(dgate_ref)

        fv = f_ref[...]
        gv = gate_ref[0]
        err = x_ref[...] + gv * fv - t_ref[...]
        sq = jnp.sum(err * err, axis=1, keepdims=True)
        loss_ref[...] += jnp.sum(sq, axis=0, keepdims=True) * (0.5 / w)
        dy = err * (1.0 / w)
        dy_ref[...] = dy
        dff_ref[...] = (dy * gv).astype(BF16)
        dgate_ref[0] += jnp.sum(dy * fv, axis=0, keepdims=True)

    row = pl.BlockSpec((tr, w), lambda i: (i, 0))
    per_ex = pl.BlockSpec((1, 1, w), lambda i: (i // nb, 0, 0))
    return pl.pallas_call(
        body, name="loss_head", grid=(t // tr,),
        in_specs=[row, row, per_ex, row],
        out_specs=[pl.BlockSpec((1, LANE), lambda i: (0, 0)), row, row, per_ex],
        out_shape=[jax.ShapeDtypeStruct((1, LANE), F32), jax.ShapeDtypeStruct((t, w), F32),
                   jax.ShapeDtypeStruct((t, w), BF16), jax.ShapeDtypeStruct((nex, 1, w), F32)],
        compiler_params=_params("arbitrary"))(x1, f2, gate, target)


def _sum_slots(parts, *, name):
    _, r, c = parts[0].shape
    tr, tc = _tile(r, 256, 16), _tile(c, 1024, LANE)

    def body(*refs):
        acc = None
        for p_ref in refs[:-1]:
            for j in range(p_ref.shape[0]):
                val = p_ref[j].astype(F32)
                acc = val if acc is None else acc + val
        refs[-1][...] = acc

    return pl.pallas_call(body, name=name, grid=(r // tr, c // tc),
                          in_specs=[pl.BlockSpec((p.shape[0], tr, tc), lambda i, j: (0, i, j)) for p in parts],
                          out_specs=pl.BlockSpec((tr, tc), lambda i, j: (i, j)),
                          out_shape=jax.ShapeDtypeStruct((r, c), F32),
                          compiler_params=_params("parallel", "parallel"))(*parts)


def _adamw(w, m, v, parts, *, name):
    _, r, c = w.shape
    parts = [p.reshape((1,) + p.shape[-2:]) for p in parts]
    tc = _tile(c, 1024, LANE) if c % LANE == 0 else c
    rows = functools.reduce(math.gcd, [p.shape[1] for p in parts], r)
    tr = _tile(rows, max(8, (256 * 1024) // tc // 8 * 8), 8)
    npart = len(parts)
    b1c = 1.0 - ADAM_B1 ** ADAM_STEP
    b2c = 1.0 - ADAM_B2 ** ADAM_STEP

    def body(*refs):
        w_ref, m_ref, v_ref = refs[:3]
        g_ref, d_ref, mo_ref, vo_ref = refs[3 + npart:]
        g = refs[3][...].astype(F32)
        for p_ref in refs[4:3 + npart]:
            g = g + p_ref[...].astype(F32)
        m2 = ADAM_B1 * m_ref[...] + (1.0 - ADAM_B1) * g
        v2 = ADAM_B2 * v_ref[...] + (1.0 - ADAM_B2) * jnp.square(g)
        m_hat = m2 / b1c
        v_hat = v2 / b2c
        g_ref[...] = g
        d_ref[...] = -ADAM_LR * (m_hat / (jnp.sqrt(v_hat) + ADAM_EPS) + ADAM_WD * w_ref[...])
        mo_ref[...] = m2
        vo_ref[...] = v2

    blk = pl.BlockSpec((1, tr, tc), lambda i, j: (0, i, j))
    return pl.pallas_call(body, name=name, grid=(r // tr, c // tc),
                          in_specs=[blk] * (3 + npart), out_specs=[blk] * 4,
                          out_shape=[jax.ShapeDtypeStruct((1, r, c), F32)] * 4,
                          compiler_params=_params("parallel", "parallel"))(w, m, v, *parts)


def _allgather8(x, *, name, after=()):
    after = tuple(after)

    def body(*refs):
        x_ref, (o_ref, ssem, rsem) = refs[0], refs[1 + len(after):]
        pos = _mesh_pos()
        me = 4 * pos[0] + 2 * pos[1] + pos[2]
        o_ref[me] = x_ref[...]
        cps = []
        for k in range(1, N_DEV):
            cp = pltpu.make_async_remote_copy(src_ref=x_ref, dst_ref=o_ref.at[me], send_sem=ssem.at[k - 1],
                                              recv_sem=rsem.at[k - 1], device_id=_flip(pos, k), device_id_type=MESH)
            cp.start()
            cps.append(cp)
        for cp in cps:
            cp.wait()

    return pl.pallas_call(
        body, name=name,
        out_shape=jax.ShapeDtypeStruct((N_DEV,) + x.shape, x.dtype),
        in_specs=[pl.BlockSpec(memory_space=pltpu.VMEM)] + [pl.BlockSpec(memory_space=pl.ANY)] * len(after),
        out_specs=pl.BlockSpec(memory_space=pltpu.VMEM),
        scratch_shapes=[pltpu.SemaphoreType.DMA((N_DEV - 1,)), pltpu.SemaphoreType.DMA((N_DEV - 1,))],
        compiler_params=pltpu.CompilerParams(vmem_limit_bytes=VMEM_LIMIT),
    )(x, *after)


def _sibling_swap(p, *, name):
    def body(p_ref, o_ref, ssem, rsem):
        cp = pltpu.make_async_remote_copy(src_ref=p_ref, dst_ref=o_ref, send_sem=ssem, recv_sem=rsem,
                                          device_id=_flip(_mesh_pos(), 1), device_id_type=MESH)
        cp.start()
        cp.wait()

    return pl.pallas_call(
        body, name=name,
        out_shape=jax.ShapeDtypeStruct(p.shape, p.dtype),
        in_specs=[pl.BlockSpec(memory_space=pl.ANY)],
        out_specs=pl.BlockSpec(memory_space=pl.ANY),
        scratch_shapes=[pltpu.SemaphoreType.DMA, pltpu.SemaphoreType.DMA],
    )(p)


_SMALL = ("b_ada", "g_norm1", "g_v", "w_s", "b_s", "g_q_lat", "g_kv_lat", "g_qn", "g_kn", "g_norm2")
_BIG = ("w_in", "w_uq", "w_ukv", "w_branch_a", "w_branch_b", "w_out", "w_ff1", "w_ff2")
_SHARD_AXIS = {"w_in": 1, "w_uq": 1, "w_ukv": 1, "w_branch_a": 1, "w_branch_b": 0, "w_out": 0, "w_ff1": 1, "w_ff2": 0}
_WEIGHTS = ("w_ada", "b_ada", "g_norm1", "w_in", "g_v", "w_s", "b_s", "g_q_lat", "g_kv_lat", "w_uq", "w_ukv",
            "g_qn", "g_kn", "w_branch_a", "w_branch_b", "w_out", "g_norm2", "w_ff1", "w_ff2")


def kernel(x, c, positions, w_ada, b_ada, g_norm1, w_in, g_v, w_s, b_s, g_q_lat, g_kv_lat, w_uq, w_ukv, g_qn, g_kn, w_branch_a, w_branch_b, w_out, g_norm2, w_ff1, w_ff2, loss_target, m_w_ada, m_b_ada, m_g_norm1, m_w_in, m_g_v, m_w_s, m_b_s, m_g_q_lat, m_g_kv_lat, m_w_uq, m_w_ukv, m_g_qn, m_g_kn, m_w_branch_a, m_w_branch_b, m_w_out, m_g_norm2, m_w_ff1, m_w_ff2, v_w_ada, v_b_ada, v_g_norm1, v_w_in, v_g_v, v_w_s, v_b_s, v_g_q_lat, v_g_kv_lat, v_w_uq, v_w_ukv, v_g_qn, v_g_kn, v_w_branch_a, v_w_branch_b, v_w_out, v_g_norm2, v_w_ff1, v_w_ff2):
    wts = dict(w_ada=w_ada, b_ada=b_ada, g_norm1=g_norm1, w_in=w_in, g_v=g_v, w_s=w_s, b_s=b_s, g_q_lat=g_q_lat,
               g_kv_lat=g_kv_lat, w_uq=w_uq, w_ukv=w_ukv, g_qn=g_qn, g_kn=g_kn, w_branch_a=w_branch_a,
               w_branch_b=w_branch_b, w_out=w_out, g_norm2=g_norm2, w_ff1=w_ff1, w_ff2=w_ff2)
    mom1 = dict(w_ada=m_w_ada, b_ada=m_b_ada, g_norm1=m_g_norm1, w_in=m_w_in, g_v=m_g_v, w_s=m_w_s, b_s=m_b_s,
                g_q_lat=m_g_q_lat, g_kv_lat=m_g_kv_lat, w_uq=m_w_uq, w_ukv=m_w_ukv, g_qn=m_g_qn, g_kn=m_g_kn,
                w_branch_a=m_w_branch_a, w_branch_b=m_w_branch_b, w_out=m_w_out, g_norm2=m_g_norm2,
                w_ff1=m_w_ff1, w_ff2=m_w_ff2)
    mom2 = dict(w_ada=v_w_ada, b_ada=v_b_ada, g_norm1=v_g_norm1, w_in=v_w_in, g_v=v_g_v, w_s=v_w_s, b_s=v_b_s,
                g_q_lat=v_g_q_lat, g_kv_lat=v_g_kv_lat, w_uq=v_w_uq, w_ukv=v_w_ukv, g_qn=v_g_qn, g_kn=v_g_kn,
                w_branch_a=v_w_branch_a, w_branch_b=v_w_branch_b, w_out=v_w_out, g_norm2=v_g_norm2,
                w_ff1=v_w_ff1, w_ff2=v_w_ff2)

    nb, seq, dm = x.shape
    t = nb * seq
    nh, qkh = MLA_HEADS, QK_NOPE + QK_ROPE
    bh = nb * nh
    off_q = 2 * GM_WIDTH
    off_kv = off_q + Q_LORA
    off_kpe = off_kv + KV_LORA
    off_gate = off_kpe + QK_ROPE
    in_cols = off_gate + 2 * dm
    ins = in_cols // N_CHIP
    insp = _round_up(ins, LANE)
    ada_cols = N_MOD * dm // N_CHIP

    ix, iy, ic = _mesh_pos()
    me = 4 * ix + 2 * iy + ic
    chip = 2 * ix + iy
    xf = x.reshape(t, dm)
    tgt = loss_target.reshape(t, dm)

    c_all = _allgather8(c, name="ag_cond").reshape(N_DEV * nb, dm)
    cond = _silu(c_all)
    b_sh = lax.dynamic_slice(b_ada, (0, chip * ada_cols), (1, ada_cols))
    mod_sh = _mm(cond, w_ada[0], name="ada_fwd", tn=2048, tk=512) + b_sh
    mod8 = _allgather8(mod_sh, name="ag_mod")
    mod_all = jnp.concatenate([mod8[2 * s] for s in range(N_CHIP)], axis=1)
    mod = lax.dynamic_slice(mod_all, (nb * me, 0), (nb, N_MOD * dm))
    sh1, sc1, ga1, sh2, sc2, ga2 = [mod[:, j * dm:(j + 1) * dm].reshape(nb, 1, dm) for j in range(N_MOD)]

    shards = {n: wts[n][0].astype(BF16) for n in _BIG}
    shards["w_in"] = jnp.pad(shards["w_in"], ((0, 0), (0, insp - ins)))

    def gathers(*names):
        return _Side([_Gather(shards[n], _SHARD_AXIS[n]) for n in names])

    full = {}
    (full["w_in"],) = _comm_call(gathers("w_in"), name="ag_w_in")
    groups = (("w_branch_a", "w_uq", "w_ukv"), ("w_branch_b", "w_out"), ("w_ff1",), ("w_ff2",))
    flying, tokens, dep = [], [], full["w_in"]
    for gi, names in enumerate(groups):
        state, tok = _split_start(gathers(*names), name="ag_start_%d" % gi, after=(dep,))
        flying.append(state)
        tokens.append(tok)
        dep = tok

    def gathered(gi, relay_after, finish_after):
        state = _split_step(flying[gi], "arrive", name="ag_arrive_%d" % gi, after=(relay_after,))
        state = _split_step(state, "forward", name="ag_forward_%d" % gi)
        state = _split_step(state, "finish", name="ag_finish_%d" % gi, after=(finish_after,))
        full.update(zip(groups[gi], _split_results(state)))

    h1 = _norm_mod_fwd(xf, g_norm1, sh1, sc1, seq, name="norm1_fwd")
    proj_p = _mm(h1, full["w_in"], name="in_fwd", after=tokens)
    gathered(0, proj_p, proj_p)

    def shard_pieces(lo, hi):
        out = []
        for s in range(N_CHIP):
            a, b = max(lo, s * ins), min(hi, (s + 1) * ins)
            if a < b:
                out.append((s, a - s * ins, b - a))
        return out

    def seg(lo, hi):
        parts = [proj_p[:, s * insp + a:s * insp + a + w] for s, a, w in shard_pieces(lo, hi)]
        return parts[0] if len(parts) == 1 else jnp.concatenate(parts, axis=1)

    bounds = (0, off_q, off_kv, off_kpe, off_gate, off_gate + dm, in_cols)
    uv, q_lat, kv_lat, k_pe, gpa, gpb = [seg(lo, hi) for lo, hi in zip(bounds[:-1], bounds[1:])]

    b_col = b_s[0].reshape(GM_GROUPS, CHUNK, 1)
    a_out = _gmlp_fwd(uv, g_v, w_s[0], b_col)
    y_a = _mm(a_out, full["w_branch_a"], name="ba_fwd")

    zq = jnp.zeros((nb, 1, Q_LORA), F32)
    zkv = jnp.zeros((nb, 1, KV_LORA), F32)
    ql = _norm_mod_fwd(q_lat, g_q_lat, zq, zq, seq, name="qlat_norm_fwd")
    kvl = _norm_mod_fwd(kv_lat, g_kv_lat, zkv, zkv, seq, name="kvlat_norm_fwd")
    w_uq_p = jnp.pad(full["w_uq"].reshape(Q_LORA, nh, qkh), ((0, 0), (0, 0), (0, QK_PAD - qkh)))
    w_uq_p = w_uq_p.reshape(Q_LORA, nh * QK_PAD)
    q_raw = _mm(ql, w_uq_p, name="uq_fwd")
    kv_raw = _mm(kvl, full["w_ukv"], name="ukv_fwd")
    kpe_p = jnp.pad(k_pe, ((0, 0), (0, LANE - QK_ROPE)))
    pos = positions.astype(F32).reshape(nb, seq, 1)
    inv_freq = 1.0 / (ROPE_THETA ** (jnp.arange(0, QK_ROPE, 2, dtype=F32) / QK_ROPE))
    invf = jnp.concatenate([inv_freq, inv_freq, jnp.zeros((LANE - QK_ROPE,), F32)]).reshape(1, LANE)
    gq_pad = jnp.pad(g_qn, ((0, 0), (0, QK_PAD - qkh)))
    gk_pad = jnp.pad(g_kn, ((0, 0), (0, QK_PAD - qkh)))
    tabs = _rope_tables(pos, invf).reshape(t, 3 * LANE)
    qn = _qk_norm_fwd(q_raw, None, gq_pad, tabs, name="qnorm_fwd")
    kn = _qk_norm_fwd(kv_raw, kpe_p, gk_pad, tabs, name="knorm_fwd")
    attn = _attn_fwd(qn, kn, kv_raw, nb)
    gathered(1, qn, attn)
    y_b = _mm(attn, full["w_branch_b"], name="bb_fwd")

    mixed = _gatemix_fwd(gpa, gpb, y_a, y_b)
    o1 = _mm(mixed, full["w_out"], name="out_fwd")
    x1, h2 = _norm_mod_fwd(xf, g_norm2, sh2, sc2, seq, name="norm2_fwd", resid=(o1, ga1))
    gathered(2, attn, o1)
    f1, act = _mm(h2, full["w_ff1"], name="ff1_fwd", out_dtypes=(BF16, BF16), epilogue=_relu2_epilogue)
    gathered(3, h2, act)
    f2 = _mm(act, full["w_ff2"], name="ff2_fwd")
    loss_acc, dy, dff, dga2 = _final(x1, f2, ga2, tgt, seq)
    loss = lax.psum(loss_acc[0, 0], ("x", "y", "c"))

    sent = {}

    def scatter_start(n, g, axis=None):
        job = _Scatter(g, _SHARD_AXIS[n] if axis is None else axis, (1, 2, 3), True)
        sent[n], token = _split_start(_Side([job]), name="rs_start_" + n)
        return token

    def scattered(n, after):
        return _split_results(_split_step(sent[n], "finish", name="rs_wait_" + n, after=(after,)))[0]

    tok = scatter_start("w_ff2", _mm(act, dff, name="ff2_dw", ta=True, out_dtypes=(BF16,)))
    df1 = _mm(dff, full["w_ff2"], name="ff2_dx", tb=True, out_dtypes=(BF16,), epilogue=_relu2_bwd_epilogue,
              extras=(f1,), after=(tok,))
    tok = scatter_start("w_ff1", _mm(h2, df1, name="ff1_dw", ta=True, out_dtypes=(BF16,)))
    dh2 = _mm(df1, full["w_ff1"], name="ff1_dx", tb=True, after=(tok,))
    dx1, dsh2, dsc2, dg_norm2, do1, dga1 = _norm_mod_bwd(x1, g_norm2, sc2, dh2, seq, name="norm2_bwd", dres=dy,
                                                         gate_o=(o1, ga1))
    tok = scatter_start("w_out", _mm(mixed, do1, name="out_dw", ta=True, out_dtypes=(BF16,)))
    dmixed = _mm(do1, full["w_out"], name="out_dx", tb=True, after=(tok,))
    dya, dyb, dgpa, dgpb = _gatemix_bwd(gpa, gpb, y_a, y_b, dmixed)

    tok = scatter_start("w_branch_b", _mm(attn, dyb, name="bb_dw", ta=True, out_dtypes=(BF16,)))
    dattn = _mm(dyb, full["w_branch_b"], name="bb_dx", tb=True, out_dtypes=(BF16,), after=(tok,))
    dqn, dkn, dv = _attn_bwd(qn, kn, kv_raw, dattn, nb)
    dq_raw, dg_qn = _qk_norm_bwd(q_raw, None, gq_pad, tabs, dqn, None, name="qnorm_bwd")
    dkv_raw, dkpe_p, dg_kn = _qk_norm_bwd(kv_raw, kpe_p, gk_pad, tabs, dkn, dv, name="knorm_bwd")
    dk_pe = dkpe_p[:, :QK_ROPE]
    gw_uq_p = _mm(ql, dq_raw, name="uq_dw", ta=True, out_dtypes=(BF16,))
    tok = scatter_start("w_uq", gw_uq_p.reshape(Q_LORA, nh, QK_PAD)[:, :, :qkh].reshape(Q_LORA, nh * qkh))
    dql = _mm(dq_raw, w_uq_p, name="uq_dx", tb=True, after=(tok,))
    tok = scatter_start("w_ukv", _mm(kvl, dkv_raw, name="ukv_dw", ta=True, out_dtypes=(BF16,)))
    dkvl = _mm(dkv_raw, full["w_ukv"], name="ukv_dx", tb=True, after=(tok,))
    dq_lat, _, _, dg_q_lat = _norm_mod_bwd(q_lat, g_q_lat, zq, dql, seq, name="qlat_norm_bwd")
    dkv_lat, _, _, dg_kv_lat = _norm_mod_bwd(kv_lat, g_kv_lat, zkv, dkvl, seq, name="kvlat_norm_bwd")

    tok = scatter_start("w_branch_a", _mm(a_out, dya, name="ba_dw", ta=True, out_dtypes=(BF16,)))
    da = _mm(dya, full["w_branch_a"], name="ba_dx", tb=True, after=(tok,))
    duv, dg_v, dw_s, db_col = _gmlp_bwd(uv, g_v, w_s[0], b_col, da)

    dsegs = (duv, dq_lat.astype(BF16), dkv_lat.astype(BF16), dk_pe.astype(BF16), dgpa, dgpb)
    by_shard = [[] for _ in range(N_CHIP)]
    for dseg, lo, hi in zip(dsegs, bounds[:-1], bounds[1:]):
        for s, a, w in shard_pieces(lo, hi):
            by_shard[s].append(dseg[:, s * ins + a - lo:s * ins + a - lo + w])
    dproj_p = jnp.concatenate([p for s in range(N_CHIP) for p in by_shard[s] + [jnp.zeros((t, insp - ins), BF16)]],
                              axis=1)
    half = dm // 2
    tok = scatter_start("in0", _mm(dproj_p, h1[:, :half], name="in_dw0", ta=True, out_dtypes=(BF16,)), 0)
    tok = scatter_start("in1", _mm(dproj_p, h1[:, half:], name="in_dw1", ta=True, out_dtypes=(BF16,),
                                   after=(tok,)), 0)
    dh1 = _mm(dproj_p, full["w_in"], name="in_dx", tb=True, after=(tok,))
    dx, dsh1, dsc1, dg_norm1 = _norm_mod_bwd(xf, g_norm1, sc1, dh1, seq, name="norm1_bwd", dres=dx1)
    grad_x = dx.reshape(nb, seq, dm)

    done_early = ("w_ff2", "w_ff1", "w_out", "w_branch_b", "w_uq", "w_ukv", "w_branch_a")
    partial = {n: _sum_slots([scattered(n, dx)], name="sum4_" + n) for n in done_early}

    gmod = jnp.concatenate([dsh1, dsc1, dga1, dsh2, dsc2, dga2], axis=-1).reshape(nb, N_MOD * dm)
    gmod_all = _allgather8(gmod, name="ag_gmod", after=tuple(partial.values())).reshape(N_DEV * nb, N_MOD * dm)
    gmod_sh = lax.dynamic_slice(gmod_all, (0, chip * ada_cols), (N_DEV * nb, ada_cols))
    swapping, tok = _split_start(_Side([_Swap(partial[n]) for n in done_early]), name="swap_start", after=(gmod_all,))
    g_w_ada = _mm(cond, gmod_sh, name="ada_dw", ta=True, tn=2048, after=(tok,))

    out_g, out_d, out_m, out_v = {}, {}, {}, {}
    out_g["w_ada"], out_d["w_ada"], out_m["w_ada"], out_v["w_ada"] = _adamw(
        w_ada, m_w_ada, v_w_ada, [g_w_ada], name="adamw_w_ada")
    late = out_g["w_ada"]
    swapped = dict(zip(done_early, _split_results(_split_step(swapping, "finish", name="swap_wait", after=(late,)))))
    for n in done_early:
        out_g[n], out_d[n], out_m[n], out_v[n] = _adamw(wts[n], mom1[n], mom2[n], [partial[n], swapped[n]],
                                                        name="adamw_" + n)
        late = out_g[n]
    got = [scattered("in0", late), scattered("in1", late)]
    part = _sum_slots([jnp.concatenate(got, axis=2)], name="sum4_w_in")
    other = _sibling_swap(part, name="swap_w_in")
    res = _adamw(*[jnp.swapaxes(a, 1, 2) for a in (w_in, m_w_in, v_w_in)], [part, other], name="adamw_w_in")
    out_g["w_in"], out_d["w_in"], out_m["w_in"], out_v["w_in"] = [jnp.swapaxes(a, 1, 2) for a in res]

    small_g = dict(b_ada=jnp.sum(gmod, axis=0), g_norm1=dg_norm1, g_v=dg_v, w_s=dw_s, b_s=db_col,
                   g_q_lat=dg_q_lat, g_kv_lat=dg_kv_lat, g_qn=dg_qn[:, :qkh], g_kn=dg_kn[:, :qkh],
                   g_norm2=dg_norm2)

    def pack(d):
        flat = jnp.concatenate([d[n].reshape(-1) for n in _SMALL])
        return jnp.pad(flat, (0, _round_up(flat.shape[0], 8 * LANE) - flat.shape[0])).reshape(-1, LANE)

    g8 = _allgather8(pack(small_g), name="ag_small_grads")
    sg, sd, sm, sv = _adamw(pack({n: wts[n] for n in _SMALL})[None], pack({n: mom1[n] for n in _SMALL})[None],
                            pack({n: mom2[n] for n in _SMALL})[None], [g8[d] for d in range(N_DEV)],
                            name="adamw_small")
    off = 0
    for n in _SMALL:
        size = wts[n].size
        for dst, src in ((out_g, sg), (out_d, sd), (out_m, sm), (out_v, sv)):
            dst[n] = src.reshape(-1)[off:off + size].reshape(wts[n].shape)
        off += size

    def shaped(d, n):
        return d[n].reshape(wts[n].shape)

    return (loss, grad_x, *[shaped(out_g, n) for n in _WEIGHTS], *[shaped(out_d, n) for n in _WEIGHTS],
            *[shaped(out_m, n) for n in _WEIGHTS], *[shaped(out_v, n) for n in _WEIGHTS])
```

```python
import functools
import math

import jax
import jax.numpy as jnp
from jax import lax
from jax.experimental import pallas as pl
from jax.experimental.pallas import tpu as pltpu

F32 = jnp.float32
BF16 = jnp.bfloat16

GM_WIDTH = 2048
GM_GROUPS = 8
CHUNK = 128
MLA_HEADS = 32
QK_NOPE = 128
QK_ROPE = 64
V_HEAD = 128
Q_LORA = 1024
KV_LORA = 512
ROPE_THETA = 10000.0
N_MOD = 6
EPS = 1e-6
ADAM_LR = 0.001
ADAM_B1 = 0.9
ADAM_B2 = 0.999
ADAM_EPS = 1e-08
ADAM_WD = 0.01
ADAM_STEP = 10

N_CHIP = 4
N_DEV = 8
LANE = 128
QK_PAD = 256
VMEM_LIMIT = 56 * 1024 * 1024
MESH = pl.DeviceIdType.MESH


def _round_up(n, m):
    return (n + m - 1) // m * m


def _tile(n, target, align):
    t = min(target, n) // align * align
    while t >= align:
        if n % t == 0:
            return t
        t -= align
    return n


def _params(*sem):
    return pltpu.CompilerParams(dimension_semantics=sem, vmem_limit_bytes=VMEM_LIMIT)


def _mesh_pos():
    return lax.axis_index("x"), lax.axis_index("y"), lax.axis_index("c")


def _flip(pos, k):
    ix, iy, ic = pos
    return (1 - ix if k & 4 else ix, 1 - iy if k & 2 else iy, 1 - ic if k & 1 else ic)


def _chip_of(pos):
    return 2 * pos[0] + pos[1]


def _window(ref, axis, chip, size):
    start = pl.multiple_of(chip * size, LANE if axis == 1 else 16)
    if axis == 1:
        return ref.at[:, pl.ds(start, size)]
    return ref.at[pl.ds(start, size), :]


def _remote(src, dst, sem, send, recv, peer):
    return pltpu.make_async_remote_copy(src_ref=src, dst_ref=dst, send_sem=sem.at[send], recv_sem=sem.at[recv],
                                        device_id=peer, device_id_type=MESH)


class _Gather:
    n_sem = 13

    def __init__(self, wsh, axis):
        self.axis, self.size, self.rows = axis, wsh.shape[axis], wsh.shape[0]
        self.inputs = [wsh]
        self.out_shape = [jax.ShapeDtypeStruct(
            tuple(d * N_CHIP if a == axis else d for a, d in enumerate(wsh.shape)), wsh.dtype)]

    def _half(self, ref, chip, c, local=False):
        h = self.rows // 2
        if self.axis == 1:
            rows = pl.ds(pl.multiple_of(c * h, 16), h)
            return ref.at[rows, :] if local else ref.at[rows, pl.ds(pl.multiple_of(chip * self.size, LANE), self.size)]
        base = 0 if local else chip * self.size
        return ref.at[pl.ds(pl.multiple_of(base + c * h, 16), h), :]

    def start(self, ins, outs, sem, s0):
        (w_ref,), (o_ref,) = ins, outs
        pos = _mesh_pos()
        chip, c = _chip_of(pos), pos[2]
        pltpu.make_async_copy(w_ref, _window(o_ref, self.axis, chip, self.size), sem.at[s0]).start()
        for j in range(1, N_CHIP):
            _remote(self._half(w_ref, chip, c, local=True), self._half(o_ref, chip, c), sem, s0 + j, s0 + 3 + j,
                    _flip(pos, 2 * j)).start()

    def _landed(self, outs, j):
        pos = _mesh_pos()
        peer = _flip(pos, 2 * j)
        return self._half(outs[0], _chip_of(peer), pos[2]), peer

    def arrive(self, ins, outs, sem, s0):
        for j in range(1, N_CHIP):
            blk, peer = self._landed(outs, j)
            _remote(blk, blk, sem, s0 + j, s0 + 3 + j, peer).wait_recv()

    def forward(self, ins, outs, sem, s0):
        for j in range(1, N_CHIP):
            blk, _ = self._landed(outs, j)
            _remote(blk, blk, sem, s0 + 6 + j, s0 + 9 + j, _flip(_mesh_pos(), 1)).start()

    def relay(self, ins, outs, sem, s0):
        for j in range(1, N_CHIP):
            blk, peer = self._landed(outs, j)
            _remote(blk, blk, sem, s0 + j, s0 + 3 + j, peer).wait_recv()
            _remote(blk, blk, sem, s0 + 6 + j, s0 + 9 + j, _flip(_mesh_pos(), 1)).start()

    def finish(self, ins, outs, sem, s0):
        (w_ref,), (o_ref,) = ins, outs
        pos = _mesh_pos()
        chip, c = _chip_of(pos), pos[2]
        sibling = _flip(pos, 1)
        for j in range(1, N_CHIP):
            peer = _flip(pos, 2 * j)
            _remote(self._half(w_ref, chip, c, local=True), self._half(o_ref, chip, c), sem, s0 + j, s0 + 3 + j,
                    peer).wait_send()
            blk = self._half(o_ref, _chip_of(peer), c)
            _remote(blk, blk, sem, s0 + 6 + j, s0 + 9 + j, sibling).wait_send()
            got = self._half(o_ref, _chip_of(peer), 1 - c)
            _remote(got, got, sem, s0 + 6 + j, s0 + 9 + j, sibling).wait_recv()
        pltpu.make_async_copy(w_ref, _window(o_ref, self.axis, chip, self.size), sem.at[s0]).wait()


class _Scatter:
    def __init__(self, g, axis, peers, own):
        self.axis, self.size, self.peers, self.own = axis, g.shape[axis] // N_CHIP, tuple(peers), int(own)
        shard = tuple(self.size if a == axis else d for a, d in enumerate(g.shape))
        self.n_sem = self.own + 2 * len(self.peers)
        self.inputs = [g]
        self.out_shape = [jax.ShapeDtypeStruct((self.own + len(self.peers),) + shard, g.dtype)]

    def _copies(self, ins, outs, sem, s0):
        (g_ref,), (o_ref,) = ins, outs
        pos = _mesh_pos()
        cps = []
        if self.own:
            cps.append(pltpu.make_async_copy(_window(g_ref, self.axis, _chip_of(pos), self.size), o_ref.at[0],
                                             sem.at[s0]))
        for n, j in enumerate(self.peers):
            peer = _flip(pos, 2 * j)
            cps.append(_remote(_window(g_ref, self.axis, _chip_of(peer), self.size), o_ref.at[self.own + n], sem,
                               s0 + self.own + 2 * n, s0 + self.own + 2 * n + 1, peer))
        return cps

    def start(self, ins, outs, sem, s0):
        for cp in self._copies(ins, outs, sem, s0):
            cp.start()

    def relay(self, ins, outs, sem, s0):
        pass

    def finish(self, ins, outs, sem, s0):
        for cp in self._copies(ins, outs, sem, s0):
            cp.wait()


class _Swap:
    n_sem = 2

    def __init__(self, p):
        self.inputs = [p]
        self.out_shape = [jax.ShapeDtypeStruct(p.shape, p.dtype)]

    def _copy(self, ins, outs, sem, s0):
        return _remote(ins[0], outs[0], sem, s0, s0 + 1, _flip(_mesh_pos(), 1))

    def start(self, ins, outs, sem, s0):
        self._copy(ins, outs, sem, s0).start()

    def relay(self, ins, outs, sem, s0):
        pass

    def finish(self, ins, outs, sem, s0):
        self._copy(ins, outs, sem, s0).wait()


class _Side:
    def __init__(self, jobs):
        self.jobs = list(jobs)
        self.inputs = [a for job in self.jobs for a in job.inputs]
        self.out_shape = [s for job in self.jobs for s in job.out_shape]

    def scratch(self):
        return [pltpu.SemaphoreType.DMA((sum(job.n_sem for job in self.jobs),))]

    def _run(self, phase, ins, outs, sem):
        i = o = s = 0
        for job in self.jobs:
            ni, no = len(job.inputs), len(job.out_shape)
            getattr(job, phase)(ins[i:i + ni], outs[o:o + no], sem, s)
            i, o, s = i + ni, o + no, s + job.n_sem

    def begin(self, ins, outs, sem, step, nsteps):
        if nsteps == 1:
            self._run("start", ins, outs, sem)
        else:
            pl.when(step == 0)(lambda: self._run("start", ins, outs, sem))

    def end(self, ins, outs, sem, step, nsteps):
        if nsteps == 1:
            self._run("relay", ins, outs, sem)
            self._run("finish", ins, outs, sem)
        else:
            pl.when(step == nsteps - max(1, nsteps // 8))(lambda: self._run("relay", ins, outs, sem))
            pl.when(step == nsteps - 1)(lambda: self._run("finish", ins, outs, sem))


def _comm_call(side, *, name):
    n_in = len(side.inputs)

    def body(*refs):
        ins, outs, sem = refs[:n_in], refs[n_in:-1], refs[-1]
        side.begin(ins, outs, sem, 0, 1)
        side.end(ins, outs, sem, 0, 1)

    any_spec = pl.BlockSpec(memory_space=pl.ANY)
    return pl.pallas_call(body, name=name, out_shape=list(side.out_shape), in_specs=[any_spec] * n_in,
                          out_specs=[any_spec] * len(side.out_shape), scratch_shapes=side.scratch())(*side.inputs)


_HBM = pl.BlockSpec(memory_space=pltpu.HBM)
_SEM = pl.BlockSpec(memory_space=pltpu.SEMAPHORE)
_DATAFLOW = pltpu.SideEffectType.DATAFLOW_SIDE_EFFECTING


def _split_start(side, *, name, after=()):
    after = tuple(after)
    ni, no = len(side.inputs), len(side.out_shape)
    bufs = list(side.inputs) + [lax.empty(s.shape, s.dtype) for s in side.out_shape]

    def body(*refs):
        side._run("start", refs[:ni], refs[ni:ni + no], refs[ni + no + len(after)])
        refs[-1][...] = jnp.zeros_like(refs[-1])

    res = pl.pallas_call(
        body, name=name,
        out_shape=[side.scratch()[0]] + [pltpu.HBM(b.shape, b.dtype) for b in bufs]
        + [jax.ShapeDtypeStruct((8, LANE), F32)],
        in_specs=[_HBM] * (ni + no) + [pl.BlockSpec(memory_space=pl.ANY)] * len(after),
        out_specs=[_SEM] + [_HBM] * (ni + no) + [pl.BlockSpec(memory_space=pltpu.VMEM)],
        input_output_aliases={i: 1 + i for i in range(ni + no)},
        compiler_params=pltpu.CompilerParams(has_side_effects=_DATAFLOW),
    )(*[pltpu.with_memory_space_constraint(b, pltpu.HBM) for b in bufs], *after)
    return (side, res[0], list(res[1:-1])), res[-1]


def _split_step(state, phase, *, name, after=()):
    side, sem, bufs = state
    after = tuple(after)
    ni, n = len(side.inputs), len(bufs)

    def body(*refs):
        side._run(phase, refs[:ni], refs[ni:n], refs[n])

    res = pl.pallas_call(
        body, name=name,
        out_shape=[pltpu.HBM(b.shape, b.dtype) for b in bufs],
        in_specs=[_HBM] * n + [_SEM] + [pl.BlockSpec(memory_space=pl.ANY)] * len(after),
        out_specs=[_HBM] * n,
        input_output_aliases={i: i for i in range(n)},
        compiler_params=pltpu.CompilerParams(has_side_effects=_DATAFLOW),
    )(*bufs, sem, *after)
    return (side, sem, list(res))


def _split_results(state):
    side, _, bufs = state
    return bufs[len(side.inputs):]


def _refs_split(refs, counts):
    out, p = [], 0
    for n in counts:
        out.append(refs[p:p + n])
        p += n
    return out


def _mm(a, b, *, name, ta=False, tb=False, out_dtypes=(F32,), epilogue=None, extras=(),
        tm=1024, tn=1024, tk=1024, m_rows=None, m_start=0, side=None, after=()):
    after = tuple(after)
    m, k = (a.shape[1], a.shape[0]) if ta else a.shape
    n = b.shape[0] if tb else b.shape[1]
    assert k == (b.shape[1] if tb else b.shape[0]), (a.shape, b.shape)
    m = m if m_rows is None else m_rows
    tm = _tile(m, tm, LANE if ta else 16)
    tn = _tile(n, tn, LANE)
    tk = _tile(k, tk, LANE)
    assert m_start % tm == 0
    mo = m_start // tm
    gi, gj, nk = m // tm, n // tn, k // tk
    nsteps = gi * gj * nk
    ne, no = len(extras), len(out_dtypes)
    ns_in, ns_out = (len(side.inputs), len(side.out_shape)) if side else (0, 0)
    dims = (((0 if ta else 1,), (1 if tb else 0,)), ((), ()))

    def body(*refs):
        (a_ref, b_ref), ex, sins, _, outs, souts, (acc,), sems = _refs_split(
            refs, (2, ne, ns_in, len(after), no, ns_out, 1, 1 if side else 0))
        kk = pl.program_id(2)
        step = (pl.program_id(0) * gj + pl.program_id(1)) * nk + kk
        if side:
            side.begin(sins, souts, sems[0], step, nsteps)

        def prod():
            return lax.dot_general(a_ref[...].astype(BF16), b_ref[...].astype(BF16), dims,
                                   preferred_element_type=F32)

        def emit(r):
            res = epilogue(r, *[e[...] for e in ex]) if epilogue is not None else (r,)
            for o, val in zip(outs, res):
                o[...] = val.astype(o.dtype)

        if nk == 1:
            emit(prod())
        else:
            @pl.when(kk == 0)
            def _():
                acc[...] = prod()

            @pl.when((kk > 0) & (kk < nk - 1))
            def _():
                acc[...] += prod()

            @pl.when(kk == nk - 1)
            def _():
                emit(acc[...] + prod())

        if side:
            side.end(sins, souts, sems[0], step, nsteps)

    a_spec = (pl.BlockSpec((tk, tm), lambda i, j, q: (q, i + mo)) if ta
              else pl.BlockSpec((tm, tk), lambda i, j, q: (i + mo, q)))
    b_spec = pl.BlockSpec((tn, tk), lambda i, j, q: (j, q)) if tb else pl.BlockSpec((tk, tn), lambda i, j, q: (q, j))
    o_spec = pl.BlockSpec((tm, tn), lambda i, j, q: (i, j))
    any_spec = pl.BlockSpec(memory_space=pl.ANY)
    res = pl.pallas_call(
        body, name=name,
        grid=(gi, gj, nk),
        in_specs=[a_spec, b_spec] + [o_spec] * ne + [any_spec] * (ns_in + len(after)),
        out_specs=[o_spec] * no + [any_spec] * ns_out,
        out_shape=[jax.ShapeDtypeStruct((m, n), dt) for dt in out_dtypes] + (list(side.out_shape) if side else []),
        scratch_shapes=[pltpu.VMEM((tm, tn), F32)] + (side.scratch() if side else []),
        compiler_params=_params(*(("arbitrary",) * 3 if side else ("parallel", "parallel", "arbitrary"))),
    )(a, b, *extras, *(side.inputs if side else ()), *after)
    return res[0] if len(res) == 1 else res


def _relu2_epilogue(r):
    return r, jnp.square(jnp.maximum(r, 0.0))


def _relu2_bwd_epilogue(r, f1):
    return (r * (2.0 * jnp.maximum(f1.astype(F32), 0.0)),)


def _row_tile(seq, width, nbytes=2 * 1024 * 1024):
    return _tile(seq, max(8, nbytes // (4 * width)), 8)


def _silu(c_all):
    def body(c_ref, o_ref):
        v = c_ref[...]
        o_ref[...] = v * jax.nn.sigmoid(v)

    return pl.pallas_call(body, name="silu", out_shape=jax.ShapeDtypeStruct(c_all.shape, F32))(c_all)


def _norm_mod_fwd(x, g, shift, scale, seq, *, name, resid=None):
    t, w = x.shape
    tr = _row_tile(seq, w, (1 if resid is not None else 2) * 1024 * 1024)
    nb = seq // tr
    has_res = resid is not None

    def body(*refs):
        if has_res:
            x_ref, o_ref, gate_ref, g_ref, sh_ref, sc_ref, x1_ref, h_ref = refs
            xv = x_ref[...] + gate_ref[0] * o_ref[...]
            x1_ref[...] = xv
        else:
            x_ref, g_ref, sh_ref, sc_ref, h_ref = refs
            xv = x_ref[...]
        r = lax.rsqrt(jnp.mean(xv * xv, axis=-1, keepdims=True) + EPS)
        nrm = xv * r * g_ref[...]
        h_ref[...] = (nrm * (1.0 + sc_ref[0]) + sh_ref[0]).astype(BF16)

    row = pl.BlockSpec((tr, w), lambda i: (i, 0))
    vec = pl.BlockSpec((1, w), lambda i: (0, 0))
    per_ex = pl.BlockSpec((1, 1, w), lambda i: (i // nb, 0, 0))
    if has_res:
        o, gate = resid
        ins, in_specs = (x, o, gate, g, shift, scale), [row, row, per_ex, vec, per_ex, per_ex]
        out_shape = [jax.ShapeDtypeStruct((t, w), F32), jax.ShapeDtypeStruct((t, w), BF16)]
        out_specs = [row, row]
    else:
        ins, in_specs = (x, g, shift, scale), [row, vec, per_ex, per_ex]
        out_shape = jax.ShapeDtypeStruct((t, w), BF16)
        out_specs = row
    return pl.pallas_call(body, name=name, grid=(t // tr,), in_specs=in_specs, out_specs=out_specs,
                          out_shape=out_shape, compiler_params=_params("parallel"))(*ins)


def _norm_mod_bwd(x, g, scale, dh, seq, *, name, dres=None, gate_o=None):
    t, w = x.shape
    nex = t // seq
    tr = _row_tile(seq, w, 1024 * 1024)
    nb = seq // tr
    has_res, has_gate = dres is not None, gate_o is not None

    def body(*refs):
        refs = list(refs)
        x_ref, g_ref, sc_ref, dh_ref = refs[:4]
        p = 4
        dres_ref = None
        if has_res:
            dres_ref = refs[p]
            p += 1
        if has_gate:
            o_ref, gate_ref = refs[p:p + 2]
            p += 2
        dx_ref, dsh_ref, dsc_ref, dg_ref = refs[p:p + 4]
        p += 4
        i = pl.program_id(0)

        @pl.when(i % nb == 0)
        def _():
            dsh_ref[...] = jnp.zeros_like(dsh_ref)
            dsc_ref[...] = jnp.zeros_like(dsc_ref)
            if has_gate:
                refs[p + 1][...] = jnp.zeros_like(refs[p + 1])

        @pl.when(i == 0)
        def _():
            dg_ref[...] = jnp.zeros_like(dg_ref)

        xv = x_ref[...]
        gv = g_ref[...]
        dhv = dh_ref[...]
        r = lax.rsqrt(jnp.mean(xv * xv, axis=-1, keepdims=True) + EPS)
        xh = xv * r
        dsh_ref[0] += jnp.sum(dhv, axis=0, keepdims=True)
        dsc_ref[0] += jnp.sum(dhv * (xh * gv), axis=0, keepdims=True)
        dn = dhv * (1.0 + sc_ref[0])
        dg_ref[...] += jnp.sum(dn * xh, axis=0, keepdims=True)
        dxh = dn * gv
        dx = r * (dxh - xh * jnp.mean(dxh * xh, axis=-1, keepdims=True))
        if has_res:
            dx = dx + dres_ref[...]
        dx_ref[...] = dx
        if has_gate:
            do_ref, dgate_ref = refs[p:p + 2]
            do_ref[...] = (dx * gate_ref[0]).astype(BF16)
            dgate_ref[0] += jnp.sum(dx * o_ref[...], axis=0, keepdims=True)

    row = pl.BlockSpec((tr, w), lambda i: (i, 0))
    vec = pl.BlockSpec((1, w), lambda i: (0, 0))
    per_ex = pl.BlockSpec((1, 1, w), lambda i: (i // nb, 0, 0))
    ins, in_specs = [x, g, scale, dh], [row, vec, per_ex, row]
    if has_res:
        ins.append(dres)
        in_specs.append(row)
    if has_gate:
        ins += list(gate_o)
        in_specs += [row, per_ex]
    ex_shape = jax.ShapeDtypeStruct((nex, 1, w), F32)
    out_shape = [jax.ShapeDtypeStruct((t, w), F32), ex_shape, ex_shape, jax.ShapeDtypeStruct((1, w), F32)]
    out_specs = [row, per_ex, per_ex, vec]
    if has_gate:
        out_shape += [jax.ShapeDtypeStruct((t, w), BF16), ex_shape]
        out_specs += [row, per_ex]
    return pl.pallas_call(body, name=name, grid=(t // tr,), in_specs=in_specs, out_specs=out_specs,
                          out_shape=out_shape, compiler_params=_params("arbitrary"))(*ins)


def _gelu_parts(xv):
    cdf = 0.5 * (1.0 + lax.erf(xv * (1.0 / math.sqrt(2.0))))
    return cdf


def _tril_mask():
    r = lax.broadcasted_iota(jnp.int32, (CHUNK, CHUNK), 0)
    c = lax.broadcasted_iota(jnp.int32, (CHUNK, CHUNK), 1)
    return c <= r


def _gmlp_fwd(uv, g_v, w_s, b_col):
    t = uv.shape[0]
    gw = GM_WIDTH // GM_GROUPS
    nck = 2
    tr = nck * CHUNK

    def body(uv_ref, gv_ref, w_ref, b_ref, o_ref):
        mask = _tril_mask()
        for ck in range(nck):
            rows = pl.ds(ck * CHUNK, CHUNK)
            xv = uv_ref[rows, :]
            z = xv * _gelu_parts(xv)
            u, v = z[:, :GM_WIDTH], z[:, GM_WIDTH:]
            r = lax.rsqrt(jnp.mean(v * v, axis=-1, keepdims=True) + EPS)
            vn = (v * r * gv_ref[...]).astype(BF16)
            for gi in range(GM_GROUPS):
                cols = slice(gi * gw, (gi + 1) * gw)
                wg = jnp.where(mask, w_ref[gi], 0.0).astype(BF16)
                mx = jnp.dot(wg, vn[:, cols], preferred_element_type=F32) + b_ref[gi]
                o_ref[rows, cols] = (u[:, cols] * mx).astype(BF16)

    return pl.pallas_call(
        body, name="gmlp_fwd", grid=(t // tr,),
        in_specs=[pl.BlockSpec((tr, 2 * GM_WIDTH), lambda i: (i, 0)),
                  pl.BlockSpec((1, GM_WIDTH), lambda i: (0, 0)),
                  pl.BlockSpec((GM_GROUPS, CHUNK, CHUNK), lambda i: (0, 0, 0)),
                  pl.BlockSpec((GM_GROUPS, CHUNK, 1), lambda i: (0, 0, 0))],
        out_specs=pl.BlockSpec((tr, GM_WIDTH), lambda i: (i, 0)),
        out_shape=jax.ShapeDtypeStruct((t, GM_WIDTH), BF16),
        compiler_params=_params("parallel"))(uv, g_v, w_s, b_col)


def _gmlp_bwd(uv, g_v, w_s, b_col, da):
    t = uv.shape[0]
    gw = GM_WIDTH // GM_GROUPS
    nck = 2
    tr = nck * CHUNK
    inv_sqrt_2pi = 1.0 / math.sqrt(2.0 * math.pi)

    def body(uv_ref, gv_ref, w_ref, b_ref, da_ref, duv_ref, dgv_ref, dw_ref, db_ref, dvn_ref):
        @pl.when(pl.program_id(0) == 0)
        def _():
            dgv_ref[...] = jnp.zeros_like(dgv_ref)
            dw_ref[...] = jnp.zeros_like(dw_ref)
            db_ref[...] = jnp.zeros_like(db_ref)

        mask = _tril_mask()
        for ck in range(nck):
            rows = pl.ds(ck * CHUNK, CHUNK)
            xv = uv_ref[rows, :]
            cdf = _gelu_parts(xv)
            z = xv * cdf
            u, v = z[:, :GM_WIDTH], z[:, GM_WIDTH:]
            r = lax.rsqrt(jnp.mean(v * v, axis=-1, keepdims=True) + EPS)
            vh = v * r
            gv = gv_ref[...]
            vn = (vh * gv).astype(BF16)
            dav = da_ref[rows, :]
            du_parts = []
            for gi in range(GM_GROUPS):
                cols = slice(gi * gw, (gi + 1) * gw)
                wg = jnp.where(mask, w_ref[gi], 0.0).astype(BF16)
                vng = vn[:, cols]
                mx = jnp.dot(wg, vng, preferred_element_type=F32) + b_ref[gi]
                du_parts.append(dav[:, cols] * mx)
                dmx = dav[:, cols] * u[:, cols]
                db_ref[gi] += jnp.sum(dmx, axis=1, keepdims=True)
                dmb = dmx.astype(BF16)
                dwg = lax.dot_general(dmb, vng, (((1,), (1,)), ((), ())), preferred_element_type=F32)
                dw_ref[gi] += jnp.where(mask, dwg, 0.0)
                dvn_ref[:, cols] = lax.dot_general(wg, dmb, (((0,), (0,)), ((), ())),
                                                   preferred_element_type=F32)
            dvn = dvn_ref[...]
            dgv_ref[...] += jnp.sum(dvn * vh, axis=0, keepdims=True)
            dvh = dvn * gv
            dv = r * (dvh - vh * jnp.mean(dvh * vh, axis=-1, keepdims=True))
            dz = jnp.concatenate(du_parts + [dv], axis=1)
            dgelu = cdf + xv * (jnp.exp(-0.5 * xv * xv) * inv_sqrt_2pi)
            duv_ref[rows, :] = (dz * dgelu).astype(BF16)

    return pl.pallas_call(
        body, name="gmlp_bwd", grid=(t // tr,),
        in_specs=[pl.BlockSpec((tr, 2 * GM_WIDTH), lambda i: (i, 0)),
                  pl.BlockSpec((1, GM_WIDTH), lambda i: (0, 0)),
                  pl.BlockSpec((GM_GROUPS, CHUNK, CHUNK), lambda i: (0, 0, 0)),
                  pl.BlockSpec((GM_GROUPS, CHUNK, 1), lambda i: (0, 0, 0)),
                  pl.BlockSpec((tr, GM_WIDTH), lambda i: (i, 0))],
        out_specs=[pl.BlockSpec((tr, 2 * GM_WIDTH), lambda i: (i, 0)),
                   pl.BlockSpec((1, GM_WIDTH), lambda i: (0, 0)),
                   pl.BlockSpec((GM_GROUPS, CHUNK, CHUNK), lambda i: (0, 0, 0)),
                   pl.BlockSpec((GM_GROUPS, CHUNK, 1), lambda i: (0, 0, 0))],
        out_shape=[jax.ShapeDtypeStruct((t, 2 * GM_WIDTH), BF16),
                   jax.ShapeDtypeStruct((1, GM_WIDTH), F32),
                   jax.ShapeDtypeStruct((GM_GROUPS, CHUNK, CHUNK), F32),
                   jax.ShapeDtypeStruct((GM_GROUPS, CHUNK, 1), F32)],
        scratch_shapes=[pltpu.VMEM((CHUNK, GM_WIDTH), F32)],
        compiler_params=_params("arbitrary"))(uv, g_v, w_s, b_col, da)


def _rope_tables(pos, invf):
    nb, s, _ = pos.shape
    ts = _tile(s, 512, 8)
    half = QK_ROPE // 2

    def body(pos_ref, invf_ref, o_ref):
        ang = pos_ref[0] * invf_ref[...]
        cs, sn = jnp.cos(ang), jnp.sin(ang)
        lane = lax.broadcasted_iota(jnp.int32, (1, LANE), 1)
        o_ref[0, :, :LANE] = jnp.where(lane < QK_ROPE, cs, 0.0)
        o_ref[0, :, LANE:2 * LANE] = jnp.where((lane >= half) & (lane < QK_ROPE), sn, 0.0)
        o_ref[0, :, 2 * LANE:] = jnp.where(lane < half, -sn, 0.0)

    return pl.pallas_call(
        body, name="rope_tables", grid=(nb, s // ts),
        in_specs=[pl.BlockSpec((1, ts, 1), lambda b, i: (b, i, 0)), pl.BlockSpec((1, LANE), lambda b, i: (0, 0))],
        out_specs=pl.BlockSpec((1, ts, 3 * LANE), lambda b, i: (b, i, 0)),
        out_shape=jax.ShapeDtypeStruct((nb, s, 3 * LANE), F32),
        compiler_params=_params("parallel", "parallel"))(pos, invf)


def _head_rstd(lo, hi):
    ss = jnp.sum(lo * lo, axis=-1, keepdims=True) + jnp.sum(hi * hi, axis=-1, keepdims=True)
    return lax.rsqrt(ss * (1.0 / (QK_NOPE + QK_ROPE)) + EPS)


def _head_specs(ts, kpe):
    if kpe is None:
        return [pl.BlockSpec((ts, QK_PAD), lambda r, h: (r, h))]
    return [pl.BlockSpec((ts, QK_NOPE), lambda r, h: (r, 2 * h)), pl.BlockSpec((ts, LANE), lambda r, h: (r, 0))]


def _head_tiles(x_refs):
    if len(x_refs) == 1:
        return x_refs[0][:, :QK_NOPE], x_refs[0][:, QK_NOPE:]
    return x_refs[0][...], x_refs[1][...]


def _qk_norm_fwd(x, kpe, g_pad, tabs, *, name):
    t = x.shape[0]
    ts = _tile(t, 1024, 8)
    half = QK_ROPE // 2
    nx = 1 if kpe is None else 2

    def body(*refs):
        x_refs, (g_ref, tab_ref, o_ref) = refs[:nx], refs[nx:]
        lo, hi = _head_tiles(x_refs)
        r = _head_rstd(lo, hi)
        hi = hi * r * g_ref[:, QK_NOPE:]
        hi = (hi * tab_ref[:, :LANE] + pltpu.roll(hi, half, 1) * tab_ref[:, LANE:2 * LANE]
              + pltpu.roll(hi, LANE - half, 1) * tab_ref[:, 2 * LANE:])
        o_ref[:, :QK_NOPE] = (lo * r * g_ref[:, :QK_NOPE]).astype(BF16)
        o_ref[:, QK_NOPE:] = hi.astype(BF16)

    return pl.pallas_call(
        body, name=name, grid=(t // ts, MLA_HEADS),
        in_specs=_head_specs(ts, kpe) + [pl.BlockSpec((1, QK_PAD), lambda r, h: (0, 0)),
                                         pl.BlockSpec((ts, 3 * LANE), lambda r, h: (r, 0))],
        out_specs=pl.BlockSpec((ts, QK_PAD), lambda r, h: (r, h)),
        out_shape=jax.ShapeDtypeStruct((t, MLA_HEADS * QK_PAD), BF16),
        compiler_params=_params("parallel", "parallel"))(*((x,) if kpe is None else (x, kpe)), g_pad, tabs)


def _qk_norm_bwd(x, kpe, g_pad, tabs, dout, dv, *, name):
    t = x.shape[0]
    ts = _tile(t, 1024, 8)
    half = QK_ROPE // 2
    is_k = kpe is not None
    nx = 2 if is_k else 1
    inv_width = 1.0 / (QK_NOPE + QK_ROPE)

    def body(*refs):
        x_refs, (g_ref, tab_ref, do_ref), rest = refs[:nx], refs[nx:nx + 3], refs[nx + 3:]
        if is_k:
            dv_ref, dx_ref, dkpe_ref, dg_ref = rest
        else:
            dx_ref, dg_ref = rest

        @pl.when((pl.program_id(0) == 0) & (pl.program_id(1) == 0))
        def _():
            dg_ref[...] = jnp.zeros_like(dg_ref)

        lo, hi = _head_tiles(x_refs)
        r = _head_rstd(lo, hi)
        lo, hi = lo * r, hi * r
        dlo, dhi = do_ref[:, :QK_NOPE], do_ref[:, QK_NOPE:]
        dhi = (dhi * tab_ref[:, :LANE] + pltpu.roll(dhi * tab_ref[:, LANE:2 * LANE], LANE - half, 1)
               + pltpu.roll(dhi * tab_ref[:, 2 * LANE:], half, 1))
        dg_ref[:, :QK_NOPE] += jnp.sum(dlo * lo, axis=0, keepdims=True)
        dg_ref[:, QK_NOPE:] += jnp.sum(dhi * hi, axis=0, keepdims=True)
        dlo, dhi = dlo * g_ref[:, :QK_NOPE], dhi * g_ref[:, QK_NOPE:]
        mean = (jnp.sum(dlo * lo, axis=-1, keepdims=True) + jnp.sum(dhi * hi, axis=-1, keepdims=True)) * inv_width
        dx_ref[:, :QK_NOPE] = (r * (dlo - lo * mean)).astype(BF16)
        dxhi = r * (dhi - hi * mean)
        if is_k:
            dx_ref[:, QK_NOPE:] = dv_ref[...].astype(BF16)

            @pl.when(pl.program_id(1) == 0)
            def _():
                dkpe_ref[...] = jnp.zeros_like(dkpe_ref)

            dkpe_ref[...] += dxhi
        else:
            dx_ref[:, QK_NOPE:] = dxhi.astype(BF16)

    head = pl.BlockSpec((ts, QK_PAD), lambda r, h: (r, h))
    vec = pl.BlockSpec((1, QK_PAD), lambda r, h: (0, 0))
    in_specs = _head_specs(ts, kpe) + [vec, pl.BlockSpec((ts, 3 * LANE), lambda r, h: (r, 0)), head]
    ins = [x] + ([kpe] if is_k else []) + [g_pad, tabs, dout]
    out_specs = [head]
    out_shape = [jax.ShapeDtypeStruct((t, MLA_HEADS * QK_PAD), BF16)]
    if is_k:
        ins.append(dv)
        in_specs.append(pl.BlockSpec((ts, V_HEAD), lambda r, h: (r, h)))
        out_specs.append(pl.BlockSpec((ts, LANE), lambda r, h: (r, 0)))
        out_shape.append(jax.ShapeDtypeStruct((t, LANE), F32))
    out_specs.append(vec)
    out_shape.append(jax.ShapeDtypeStruct((1, QK_PAD), F32))
    return pl.pallas_call(body, name=name, grid=(t // ts, MLA_HEADS), in_specs=in_specs, out_specs=out_specs,
                          out_shape=out_shape, compiler_params=_params("arbitrary", "arbitrary"))(*ins)


def _causal_probs(q, k, q0):
    scale = (QK_NOPE + QK_ROPE) ** -0.5
    sc = lax.dot_general(q, k, (((1,), (1,)), ((), ())), preferred_element_type=F32) * scale
    qi = q0 + lax.broadcasted_iota(jnp.int32, sc.shape, 0)
    ki = lax.broadcasted_iota(jnp.int32, sc.shape, 1)
    sc = jnp.where(ki <= qi, sc, -1e30)
    e = jnp.exp(sc - jnp.max(sc, axis=-1, keepdims=True))
    return e / jnp.sum(e, axis=-1, keepdims=True)


def _attn_specs(s, tq):
    nq = s // tq
    q_spec = pl.BlockSpec((tq, QK_PAD), lambda b, h, i: (b * nq + i, h))
    o_spec = pl.BlockSpec((tq, V_HEAD), lambda b, h, i: (b * nq + i, h))
    k_spec = pl.BlockSpec((s, QK_PAD), lambda b, h, i: (b, h))
    v_spec = pl.BlockSpec((s, V_HEAD), lambda b, h, i: (b, 2 * h + 1))
    dv_spec = pl.BlockSpec((s, V_HEAD), lambda b, h, i: (b, h))
    return q_spec, o_spec, k_spec, v_spec, dv_spec


def _attn_fwd(qn, kn, kv_raw, nb):
    t = qn.shape[0]
    s = t // nb
    tq = _tile(s, 256, 8)
    nq = s // tq
    q_spec, o_spec, k_spec, v_spec, _ = _attn_specs(s, tq)

    def body(q_ref, k_ref, v_ref, o_ref):
        def block(g):
            keys = (g + 1) * tq
            p = _causal_probs(q_ref[...], k_ref[:keys, :], g * tq)
            o_ref[...] = jnp.dot(p.astype(BF16), v_ref[:keys, :].astype(BF16),
                                 preferred_element_type=F32).astype(BF16)

        for g in range(nq):
            pl.when(pl.program_id(2) == g)(functools.partial(block, g))

    return pl.pallas_call(
        body, name="attn_fwd", grid=(nb, MLA_HEADS, nq),
        in_specs=[q_spec, k_spec, v_spec], out_specs=o_spec,
        out_shape=jax.ShapeDtypeStruct((t, MLA_HEADS * V_HEAD), BF16),
        compiler_params=_params("parallel", "parallel", "parallel"))(qn, kn, kv_raw)


def _attn_bwd(qn, kn, kv_raw, dattn, nb):
    t = qn.shape[0]
    s = t // nb
    tq = _tile(s, 256, 8)
    nq = s // tq
    scale = (QK_NOPE + QK_ROPE) ** -0.5
    q_spec, o_spec, k_spec, v_spec, dv_spec = _attn_specs(s, tq)

    def body(q_ref, k_ref, v_ref, do_ref, dq_ref, dk_ref, dv_ref):
        @pl.when(pl.program_id(2) == 0)
        def _():
            dk_ref[...] = jnp.zeros_like(dk_ref)
            dv_ref[...] = jnp.zeros_like(dv_ref)

        def block(g):
            keys = (g + 1) * tq
            q, k, v, do = q_ref[...], k_ref[:keys, :], v_ref[:keys, :].astype(BF16), do_ref[...]
            p = _causal_probs(q, k, g * tq)
            dv_ref[:keys, :] += lax.dot_general(p.astype(BF16), do, (((0,), (0,)), ((), ())),
                                                preferred_element_type=F32)
            dp = lax.dot_general(do, v, (((1,), (1,)), ((), ())), preferred_element_type=F32)
            ds = (p * (dp - jnp.sum(p * dp, axis=-1, keepdims=True)) * scale).astype(BF16)
            dq_ref[...] = jnp.dot(ds, k, preferred_element_type=F32)
            dk_ref[:keys, :] += lax.dot_general(ds, q, (((0,), (0,)), ((), ())), preferred_element_type=F32)

        for g in range(nq):
            pl.when(pl.program_id(2) == g)(functools.partial(block, g))

    return pl.pallas_call(
        body, name="attn_bwd", grid=(nb, MLA_HEADS, nq),
        in_specs=[q_spec, k_spec, v_spec, o_spec],
        out_specs=[q_spec, k_spec, dv_spec],
        out_shape=[jax.ShapeDtypeStruct((t, MLA_HEADS * QK_PAD), F32),
                   jax.ShapeDtypeStruct((t, MLA_HEADS * QK_PAD), F32),
                   jax.ShapeDtypeStruct((t, MLA_HEADS * V_HEAD), F32)],
        compiler_params=_params("parallel", "parallel", "arbitrary"))(qn, kn, kv_raw, dattn)


def _gatemix_fwd(gpa, gpb, ya, yb):
    t, w = ya.shape
    tr = _tile(t, 128, 8)

    def body(ga_ref, gb_ref, ya_ref, yb_ref, o_ref):
        o_ref[...] = (jax.nn.sigmoid(ga_ref[...]) * ya_ref[...]
                      + jax.nn.sigmoid(gb_ref[...]) * yb_ref[...]).astype(BF16)

    row = pl.BlockSpec((tr, w), lambda i: (i, 0))
    return pl.pallas_call(body, name="gatemix_fwd", grid=(t // tr,), in_specs=[row] * 4, out_specs=row,
                          out_shape=jax.ShapeDtypeStruct((t, w), BF16),
                          compiler_params=_params("parallel"))(gpa, gpb, ya, yb)


def _gatemix_bwd(gpa, gpb, ya, yb, dmix):
    t, w = ya.shape
    tr = _tile(t, 128, 8)

    def body(ga_ref, gb_ref, ya_ref, yb_ref, dm_ref, dya_ref, dyb_ref, dga_ref, dgb_ref):
        dm = dm_ref[...]
        sa = jax.nn.sigmoid(ga_ref[...])
        sb = jax.nn.sigmoid(gb_ref[...])
        dya_ref[...] = (dm * sa).astype(BF16)
        dyb_ref[...] = (dm * sb).astype(BF16)
        dga_ref[...] = (dm * ya_ref[...] * sa * (1.0 - sa)).astype(BF16)
        dgb_ref[...] = (dm * yb_ref[...] * sb * (1.0 - sb)).astype(BF16)

    row = pl.BlockSpec((tr, w), lambda i: (i, 0))
    return pl.pallas_call(body, name="gatemix_bwd", grid=(t // tr,), in_specs=[row] * 5, out_specs=[row] * 4,
                          out_shape=[jax.ShapeDtypeStruct((t, w), BF16)] * 4,
                          compiler_params=_params("parallel"))(gpa, gpb, ya, yb, dmix)


def _final(x1, f2, gate, target, seq):
    t, w = x1.shape
    nex = t // seq
    tr = _row_tile(seq, w, 1024 * 1024)
    nb = seq // tr

    def body(x_ref, f_ref, gate_ref, t_ref, loss_ref, dy_ref, dff_ref, dgate_ref):
        i = pl.program_id(0)

        @pl.when(i == 0)
        def _():
            loss_ref[...] = jnp.zeros_like(loss_ref)

        @pl.when(i % nb == 0)
        def _():
            dgate_ref[...] = jnp.zeros_like(dgate_ref)

        fv = f_ref[...]
        gv = gate_ref[0]
        err = x_ref[...] + gv * fv - t_ref[...]
        sq = jnp.sum(err * err, axis=1, keepdims=True)
        loss_ref[...] += jnp.sum(sq, axis=0, keepdims=True) * (0.5 / w)
        dy = err * (1.0 / w)
        dy_ref[...] = dy
        dff_ref[...] = (dy * gv).astype(BF16)
        dgate_ref[0] += jnp.sum(dy * fv, axis=0, keepdims=True)

    row = pl.BlockSpec((tr, w), lambda i: (i, 0))
    per_ex = pl.BlockSpec((1, 1, w), lambda i: (i // nb, 0, 0))
    return pl.pallas_call(
        body, name="loss_head", grid=(t // tr,),
        in_specs=[row, row, per_ex, row],
        out_specs=[pl.BlockSpec((1, LANE), lambda i: (0, 0)), row, row, per_ex],
        out_shape=[jax.ShapeDtypeStruct((1, LANE), F32), jax.ShapeDtypeStruct((t, w), F32),
                   jax.ShapeDtypeStruct((t, w), BF16), jax.ShapeDtypeStruct((nex, 1, w), F32)],
        compiler_params=_params("arbitrary"))(x1, f2, gate, target)


def _sum_slots(parts, *, name):
    _, r, c = parts[0].shape
    tr, tc = _tile(r, 256, 16), _tile(c, 1024, LANE)

    def body(*refs):
        acc = None
        for p_ref in refs[:-1]:
            for j in range(p_ref.shape[0]):
                val = p_ref[j].astype(F32)
                acc = val if acc is None else acc + val
        refs[-1][...] = acc

    return pl.pallas_call(body, name=name, grid=(r // tr, c // tc),
                          in_specs=[pl.BlockSpec((p.shape[0], tr, tc), lambda i, j: (0, i, j)) for p in parts],
                          out_specs=pl.BlockSpec((tr, tc), lambda i, j: (i, j)),
                          out_shape=jax.ShapeDtypeStruct((r, c), F32),
                          compiler_params=_params("parallel", "parallel"))(*parts)


def _adamw(w, m, v, parts, *, name):
    _, r, c = w.shape
    parts = [p.reshape((1,) + p.shape[-2:]) for p in parts]
    tc = _tile(c, 1024, LANE) if c % LANE == 0 else c
    rows = functools.reduce(math.gcd, [p.shape[1] for p in parts], r)
    tr = _tile(rows, max(8, (256 * 1024) // tc // 8 * 8), 8)
    npart = len(parts)
    b1c = 1.0 - ADAM_B1 ** ADAM_STEP
    b2c = 1.0 - ADAM_B2 ** ADAM_STEP

    def body(*refs):
        w_ref, m_ref, v_ref = refs[:3]
        g_ref, d_ref, mo_ref, vo_ref = refs[3 + npart:]
        g = refs[3][...].astype(F32)
        for p_ref in refs[4:3 + npart]:
            g = g + p_ref[...].astype(F32)
        m2 = ADAM_B1 * m_ref[...] + (1.0 - ADAM_B1) * g
        v2 = ADAM_B2 * v_ref[...] + (1.0 - ADAM_B2) * jnp.square(g)
        m_hat = m2 / b1c
        v_hat = v2 / b2c
        g_ref[...] = g
        d_ref[...] = -ADAM_LR * (m_hat / (jnp.sqrt(v_hat) + ADAM_EPS) + ADAM_WD * w_ref[...])
        mo_ref[...] = m2
        vo_ref[...] = v2

    blk = pl.BlockSpec((1, tr, tc), lambda i, j: (0, i, j))
    return pl.pallas_call(body, name=name, grid=(r // tr, c // tc),
                          in_specs=[blk] * (3 + npart), out_specs=[blk] * 4,
                          out_shape=[jax.ShapeDtypeStruct((1, r, c), F32)] * 4,
                          compiler_params=_params("parallel", "parallel"))(w, m, v, *parts)


def _allgather8(x, *, name, after=()):
    after = tuple(after)

    def body(*refs):
        x_ref, (o_ref, ssem, rsem) = refs[0], refs[1 + len(after):]
        pos = _mesh_pos()
        me = 4 * pos[0] + 2 * pos[1] + pos[2]
        o_ref[me] = x_ref[...]
        cps = []
        for k in range(1, N_DEV):
            cp = pltpu.make_async_remote_copy(src_ref=x_ref, dst_ref=o_ref.at[me], send_sem=ssem.at[k - 1],
                                              recv_sem=rsem.at[k - 1], device_id=_flip(pos, k), device_id_type=MESH)
            cp.start()
            cps.append(cp)
        for cp in cps:
            cp.wait()

    return pl.pallas_call(
        body, name=name,
        out_shape=jax.ShapeDtypeStruct((N_DEV,) + x.shape, x.dtype),
        in_specs=[pl.BlockSpec(memory_space=pltpu.VMEM)] + [pl.BlockSpec(memory_space=pl.ANY)] * len(after),
        out_specs=pl.BlockSpec(memory_space=pltpu.VMEM),
        scratch_shapes=[pltpu.SemaphoreType.DMA((N_DEV - 1,)), pltpu.SemaphoreType.DMA((N_DEV - 1,))],
        compiler_params=pltpu.CompilerParams(vmem_limit_bytes=VMEM_LIMIT),
    )(x, *after)


def _sibling_swap(p, *, name):
    def body(p_ref, o_ref, ssem, rsem):
        cp = pltpu.make_async_remote_copy(src_ref=p_ref, dst_ref=o_ref, send_sem=ssem, recv_sem=rsem,
                                          device_id=_flip(_mesh_pos(), 1), device_id_type=MESH)
        cp.start()
        cp.wait()

    return pl.pallas_call(
        body, name=name,
        out_shape=jax.ShapeDtypeStruct(p.shape, p.dtype),
        in_specs=[pl.BlockSpec(memory_space=pl.ANY)],
        out_specs=pl.BlockSpec(memory_space=pl.ANY),
        scratch_shapes=[pltpu.SemaphoreType.DMA, pltpu.SemaphoreType.DMA],
    )(p)


_SMALL = ("b_ada", "g_norm1", "g_v", "w_s", "b_s", "g_q_lat", "g_kv_lat", "g_qn", "g_kn", "g_norm2")
_BIG = ("w_in", "w_uq", "w_ukv", "w_branch_a", "w_branch_b", "w_out", "w_ff1", "w_ff2")
_SHARD_AXIS = {"w_in": 1, "w_uq": 1, "w_ukv": 1, "w_branch_a": 1, "w_branch_b": 0, "w_out": 0, "w_ff1": 1, "w_ff2": 0}
_WEIGHTS = ("w_ada", "b_ada", "g_norm1", "w_in", "g_v", "w_s", "b_s", "g_q_lat", "g_kv_lat", "w_uq", "w_ukv",
            "g_qn", "g_kn", "w_branch_a", "w_branch_b", "w_out", "g_norm2", "w_ff1", "w_ff2")


def kernel(x, c, positions, w_ada, b_ada, g_norm1, w_in, g_v, w_s, b_s, g_q_lat, g_kv_lat, w_uq, w_ukv, g_qn, g_kn, w_branch_a, w_branch_b, w_out, g_norm2, w_ff1, w_ff2, loss_target, m_w_ada, m_b_ada, m_g_norm1, m_w_in, m_g_v, m_w_s, m_b_s, m_g_q_lat, m_g_kv_lat, m_w_uq, m_w_ukv, m_g_qn, m_g_kn, m_w_branch_a, m_w_branch_b, m_w_out, m_g_norm2, m_w_ff1, m_w_ff2, v_w_ada, v_b_ada, v_g_norm1, v_w_in, v_g_v, v_w_s, v_b_s, v_g_q_lat, v_g_kv_lat, v_w_uq, v_w_ukv, v_g_qn, v_g_kn, v_w_branch_a, v_w_branch_b, v_w_out, v_g_norm2, v_w_ff1, v_w_ff2):
    wts = dict(w_ada=w_ada, b_ada=b_ada, g_norm1=g_norm1, w_in=w_in, g_v=g_v, w_s=w_s, b_s=b_s, g_q_lat=g_q_lat,
               g_kv_lat=g_kv_lat, w_uq=w_uq, w_ukv=w_ukv, g_qn=g_qn, g_kn=g_kn, w_branch_a=w_branch_a,
               w_branch_b=w_branch_b, w_out=w_out, g_norm2=g_norm2, w_ff1=w_ff1, w_ff2=w_ff2)
    mom1 = dict(w_ada=m_w_ada, b_ada=m_b_ada, g_norm1=m_g_norm1, w_in=m_w_in, g_v=m_g_v, w_s=m_w_s, b_s=m_b_s,
                g_q_lat=m_g_q_lat, g_kv_lat=m_g_kv_lat, w_uq=m_w_uq, w_ukv=m_w_ukv, g_qn=m_g_qn, g_kn=m_g_kn,
                w_branch_a=m_w_branch_a, w_branch_b=m_w_branch_b, w_out=m_w_out, g_norm2=m_g_norm2,
                w_ff1=m_w_ff1, w_ff2=m_w_ff2)
    mom2 = dict(w_ada=v_w_ada, b_ada=v_b_ada, g_norm1=v_g_norm1, w_in=v_w_in, g_v=v_g_v, w_s=v_w_s, b_s=v_b_s,
                g_q_lat=v_g_q_lat, g_kv_lat=v_g_kv_lat, w_uq=v_w_uq, w_ukv=v_w_ukv, g_qn=v_g_qn, g_kn=v_g_kn,
                w_branch_a=v_w_branch_a, w_branch_b=v_w_branch_b, w_out=v_w_out, g_norm2=v_g_norm2,
                w_ff1=v_w_ff1, w_ff2=v_w_ff2)

    nb, seq, dm = x.shape
    t = nb * seq
    nh, qkh = MLA_HEADS, QK_NOPE + QK_ROPE
    bh = nb * nh
    off_q = 2 * GM_WIDTH
    off_kv = off_q + Q_LORA
    off_kpe = off_kv + KV_LORA
    off_gate = off_kpe + QK_ROPE
    in_cols = off_gate + 2 * dm
    ins = in_cols // N_CHIP
    insp = _round_up(ins, LANE)
    ada_cols = N_MOD * dm // N_CHIP

    ix, iy, ic = _mesh_pos()
    me = 4 * ix + 2 * iy + ic
    chip = 2 * ix + iy
    xf = x.reshape(t, dm)
    tgt = loss_target.reshape(t, dm)

    shards = {n: wts[n][0].astype(BF16) for n in _BIG}
    shards["w_in"] = jnp.pad(shards["w_in"], ((0, 0), (0, insp - ins)))

    def gathers(*names):
        return _Side([_Gather(shards[n], _SHARD_AXIS[n]) for n in names])

    in_flight, tok = _split_start(gathers("w_in"), name="ag_start_in")

    c_all = _allgather8(c, name="ag_cond", after=(tok,)).reshape(N_DEV * nb, dm)
    cond = _silu(c_all)
    b_sh = lax.dynamic_slice(b_ada, (0, chip * ada_cols), (1, ada_cols))
    mod_sh = _mm(cond, w_ada[0], name="ada_fwd", tn=2048, tk=512) + b_sh
    mod8 = _allgather8(mod_sh, name="ag_mod")
    mod_all = jnp.concatenate([mod8[2 * s] for s in range(N_CHIP)], axis=1)
    mod = lax.dynamic_slice(mod_all, (nb * me, 0), (nb, N_MOD * dm))
    sh1, sc1, ga1, sh2, sc2, ga2 = [mod[:, j * dm:(j + 1) * dm].reshape(nb, 1, dm) for j in range(N_MOD)]

    full = {}
    state = _split_step(in_flight, "arrive", name="ag_arrive_in", after=(mod8,))
    state = _split_step(state, "forward", name="ag_forward_in")
    (full["w_in"],) = _split_results(_split_step(state, "finish", name="ag_finish_in"))
    groups = (("w_branch_a", "w_uq", "w_ukv"), ("w_branch_b", "w_out"), ("w_ff1",), ("w_ff2",))
    flying, tokens, dep = [], [], full["w_in"]
    for gi, names in enumerate(groups):
        state, tok = _split_start(gathers(*names), name="ag_start_%d" % gi, after=(dep,))
        flying.append(state)
        tokens.append(tok)
        dep = tok

    def gathered(gi, relay_after, finish_after):
        state = _split_step(flying[gi], "arrive", name="ag_arrive_%d" % gi, after=(relay_after,))
        state = _split_step(state, "forward", name="ag_forward_%d" % gi)
        state = _split_step(state, "finish", name="ag_finish_%d" % gi, after=(finish_after,))
        full.update(zip(groups[gi], _split_results(state)))

    h1 = _norm_mod_fwd(xf, g_norm1, sh1, sc1, seq, name="norm1_fwd")
    proj_p = _mm(h1, full["w_in"], name="in_fwd", after=tokens)
    gathered(0, proj_p, proj_p)

    def shard_pieces(lo, hi):
        out = []
        for s in range(N_CHIP):
            a, b = max(lo, s * ins), min(hi, (s + 1) * ins)
            if a < b:
                out.append((s, a - s * ins, b - a))
        return out

    def seg(lo, hi):
        parts = [proj_p[:, s * insp + a:s * insp + a + w] for s, a, w in shard_pieces(lo, hi)]
        return parts[0] if len(parts) == 1 else jnp.concatenate(parts, axis=1)

    bounds = (0, off_q, off_kv, off_kpe, off_gate, off_gate + dm, in_cols)
    uv, q_lat, kv_lat, k_pe, gpa, gpb = [seg(lo, hi) for lo, hi in zip(bounds[:-1], bounds[1:])]

    b_col = b_s[0].reshape(GM_GROUPS, CHUNK, 1)
    a_out = _gmlp_fwd(uv, g_v, w_s[0], b_col)
    y_a = _mm(a_out, full["w_branch_a"], name="ba_fwd")

    zq = jnp.zeros((nb, 1, Q_LORA), F32)
    zkv = jnp.zeros((nb, 1, KV_LORA), F32)
    ql = _norm_mod_fwd(q_lat, g_q_lat, zq, zq, seq, name="qlat_norm_fwd")
    kvl = _norm_mod_fwd(kv_lat, g_kv_lat, zkv, zkv, seq, name="kvlat_norm_fwd")
    w_uq_p = jnp.pad(full["w_uq"].reshape(Q_LORA, nh, qkh), ((0, 0), (0, 0), (0, QK_PAD - qkh)))
    w_uq_p = w_uq_p.reshape(Q_LORA, nh * QK_PAD)
    q_raw = _mm(ql, w_uq_p, name="uq_fwd")
    kv_raw = _mm(kvl, full["w_ukv"], name="ukv_fwd")
    kpe_p = jnp.pad(k_pe, ((0, 0), (0, LANE - QK_ROPE)))
    pos = positions.astype(F32).reshape(nb, seq, 1)
    inv_freq = 1.0 / (ROPE_THETA ** (jnp.arange(0, QK_ROPE, 2, dtype=F32) / QK_ROPE))
    invf = jnp.concatenate([inv_freq, inv_freq, jnp.zeros((LANE - QK_ROPE,), F32)]).reshape(1, LANE)
    gq_pad = jnp.pad(g_qn, ((0, 0), (0, QK_PAD - qkh)))
    gk_pad = jnp.pad(g_kn, ((0, 0), (0, QK_PAD - qkh)))
    tabs = _rope_tables(pos, invf).reshape(t, 3 * LANE)
    qn = _qk_norm_fwd(q_raw, None, gq_pad, tabs, name="qnorm_fwd")
    kn = _qk_norm_fwd(kv_raw, kpe_p, gk_pad, tabs, name="knorm_fwd")
    attn = _attn_fwd(qn, kn, kv_raw, nb)
    gathered(1, qn, attn)
    y_b = _mm(attn, full["w_branch_b"], name="bb_fwd")

    mixed = _gatemix_fwd(gpa, gpb, y_a, y_b)
    o1 = _mm(mixed, full["w_out"], name="out_fwd")
    x1, h2 = _norm_mod_fwd(xf, g_norm2, sh2, sc2, seq, name="norm2_fwd", resid=(o1, ga1))
    gathered(2, attn, o1)
    f1, act = _mm(h2, full["w_ff1"], name="ff1_fwd", out_dtypes=(BF16, BF16), epilogue=_relu2_epilogue)
    gathered(3, h2, act)
    f2 = _mm(act, full["w_ff2"], name="ff2_fwd")
    loss_acc, dy, dff, dga2 = _final(x1, f2, ga2, tgt, seq)
    loss = lax.psum(loss_acc[0, 0], ("x", "y", "c"))

    sent = {}

    def scatter_start(n, g, axis=None):
        job = _Scatter(g, _SHARD_AXIS[n] if axis is None else axis, (1, 2, 3), True)
        sent[n], token = _split_start(_Side([job]), name="rs_start_" + n)
        return token

    def scattered(n, after):
        return _split_results(_split_step(sent[n], "finish", name="rs_wait_" + n, after=(after,)))[0]

    tok = scatter_start("w_ff2", _mm(act, dff, name="ff2_dw", ta=True, out_dtypes=(BF16,)))
    df1 = _mm(dff, full["w_ff2"], name="ff2_dx", tb=True, out_dtypes=(BF16,), epilogue=_relu2_bwd_epilogue,
              extras=(f1,), after=(tok,))
    tok = scatter_start("w_ff1", _mm(h2, df1, name="ff1_dw", ta=True, out_dtypes=(BF16,)))
    dh2 = _mm(df1, full["w_ff1"], name="ff1_dx", tb=True, after=(tok,))
    dx1, dsh2, dsc2, dg_norm2, do1, dga1 = _norm_mod_bwd(x1, g_norm2, sc2, dh2, seq, name="norm2_bwd", dres=dy,
                                                         gate_o=(o1, ga1))
    tok = scatter_start("w_out", _mm(mixed, do1, name="out_dw", ta=True, out_dtypes=(BF16,)))
    dmixed = _mm(do1, full["w_out"], name="out_dx", tb=True, after=(tok,))
    dya, dyb, dgpa, dgpb = _gatemix_bwd(gpa, gpb, y_a, y_b, dmixed)

    tok = scatter_start("w_branch_b", _mm(attn, dyb, name="bb_dw", ta=True, out_dtypes=(BF16,)))
    dattn = _mm(dyb, full["w_branch_b"], name="bb_dx", tb=True, out_dtypes=(BF16,), after=(tok,))
    dqn, dkn, dv = _attn_bwd(qn, kn, kv_raw, dattn, nb)
    dq_raw, dg_qn = _qk_norm_bwd(q_raw, None, gq_pad, tabs, dqn, None, name="qnorm_bwd")
    dkv_raw, dkpe_p, dg_kn = _qk_norm_bwd(kv_raw, kpe_p, gk_pad, tabs, dkn, dv, name="knorm_bwd")
    dk_pe = dkpe_p[:, :QK_ROPE]
    gw_uq_p = _mm(ql, dq_raw, name="uq_dw", ta=True, out_dtypes=(BF16,))
    tok = scatter_start("w_uq", gw_uq_p.reshape(Q_LORA, nh, QK_PAD)[:, :, :qkh].reshape(Q_LORA, nh * qkh))
    dql = _mm(dq_raw, w_uq_p, name="uq_dx", tb=True, after=(tok,))
    tok = scatter_start("w_ukv", _mm(kvl, dkv_raw, name="ukv_dw", ta=True, out_dtypes=(BF16,)))
    dkvl = _mm(dkv_raw, full["w_ukv"], name="ukv_dx", tb=True, after=(tok,))
    dq_lat, _, _, dg_q_lat = _norm_mod_bwd(q_lat, g_q_lat, zq, dql, seq, name="qlat_norm_bwd")
    dkv_lat, _, _, dg_kv_lat = _norm_mod_bwd(kv_lat, g_kv_lat, zkv, dkvl, seq, name="kvlat_norm_bwd")

    tok = scatter_start("w_branch_a", _mm(a_out, dya, name="ba_dw", ta=True, out_dtypes=(BF16,)))
    da = _mm(dya, full["w_branch_a"], name="ba_dx", tb=True, after=(tok,))
    duv, dg_v, dw_s, db_col = _gmlp_bwd(uv, g_v, w_s[0], b_col, da)

    dsegs = (duv, dq_lat.astype(BF16), dkv_lat.astype(BF16), dk_pe.astype(BF16), dgpa, dgpb)
    by_shard = [[] for _ in range(N_CHIP)]
    for dseg, lo, hi in zip(dsegs, bounds[:-1], bounds[1:]):
        for s, a, w in shard_pieces(lo, hi):
            by_shard[s].append(dseg[:, s * ins + a - lo:s * ins + a - lo + w])
    dproj_p = jnp.concatenate([p for s in range(N_CHIP) for p in by_shard[s] + [jnp.zeros((t, insp - ins), BF16)]],
                              axis=1)
    half = dm // 2
    tok = scatter_start("in0", _mm(dproj_p, h1[:, :half], name="in_dw0", ta=True, out_dtypes=(BF16,)), 0)
    tok = scatter_start("in1", _mm(dproj_p, h1[:, half:], name="in_dw1", ta=True, out_dtypes=(BF16,),
                                   after=(tok,)), 0)
    dh1 = _mm(dproj_p, full["w_in"], name="in_dx", tb=True, after=(tok,))
    dx, dsh1, dsc1, dg_norm1 = _norm_mod_bwd(xf, g_norm1, sc1, dh1, seq, name="norm1_bwd", dres=dx1)
    grad_x = dx.reshape(nb, seq, dm)

    done_early = ("w_ff2", "w_ff1", "w_out", "w_branch_b", "w_uq", "w_ukv", "w_branch_a")
    partial = {n: _sum_slots([scattered(n, dx)], name="sum4_" + n) for n in done_early}

    gmod = jnp.concatenate([dsh1, dsc1, dga1, dsh2, dsc2, dga2], axis=-1).reshape(nb, N_MOD * dm)
    gmod_all = _allgather8(gmod, name="ag_gmod", after=tuple(partial.values())).reshape(N_DEV * nb, N_MOD * dm)
    gmod_sh = lax.dynamic_slice(gmod_all, (0, chip * ada_cols), (N_DEV * nb, ada_cols))
    swapping, tok = _split_start(_Side([_Swap(partial[n]) for n in done_early]), name="swap_start", after=(gmod_all,))
    g_w_ada = _mm(cond, gmod_sh, name="ada_dw", ta=True, tn=2048, after=(tok,))

    out_g, out_d, out_m, out_v = {}, {}, {}, {}
    out_g["w_ada"], out_d["w_ada"], out_m["w_ada"], out_v["w_ada"] = _adamw(
        w_ada, m_w_ada, v_w_ada, [g_w_ada], name="adamw_w_ada")
    late = out_g["w_ada"]
    swapped = dict(zip(done_early, _split_results(_split_step(swapping, "finish", name="swap_wait", after=(late,)))))
    for n in done_early:
        out_g[n], out_d[n], out_m[n], out_v[n] = _adamw(wts[n], mom1[n], mom2[n], [partial[n], swapped[n]],
                                                        name="adamw_" + n)
        late = out_g[n]
    got = [scattered("in0", late), scattered("in1", late)]
    part = _sum_slots([jnp.concatenate(got, axis=2)], name="sum4_w_in")
    other = _sibling_swap(part, name="swap_w_in")
    res = _adamw(*[jnp.swapaxes(a, 1, 2) for a in (w_in, m_w_in, v_w_in)], [part, other], name="adamw_w_in")
    out_g["w_in"], out_d["w_in"], out_m["w_in"], out_v["w_in"] = [jnp.swapaxes(a, 1, 2) for a in res]

    small_g = dict(b_ada=jnp.sum(gmod, axis=0), g_norm1=dg_norm1, g_v=dg_v, w_s=dw_s, b_s=db_col,
                   g_q_lat=dg_q_lat, g_kv_lat=dg_kv_lat, g_qn=dg_qn[:, :qkh], g_kn=dg_kn[:, :qkh],
                   g_norm2=dg_norm2)

    def pack(d):
        flat = jnp.concatenate([d[n].reshape(-1) for n in _SMALL])
        return jnp.pad(flat, (0, _round_up(flat.shape[0], 8 * LANE) - flat.shape[0])).reshape(-1, LANE)

    g8 = _allgather8(pack(small_g), name="ag_small_grads")
    sg, sd, sm, sv = _adamw(pack({n: wts[n] for n in _SMALL})[None], pack({n: mom1[n] for n in _SMALL})[None],
                            pack({n: mom2[n] for n in _SMALL})[None], [g8[d] for d in range(N_DEV)],
                            name="adamw_small")
    off = 0
    for n in _SMALL:
        size = wts[n].size
        for dst, src in ((out_g, sg), (out_d, sd), (out_m, sm), (out_v, sv)):
            dst[n] = src.reshape(-1)[off:off + size].reshape(wts[n].shape)
        off += size

    def shaped(d, n):
        return d[n].reshape(wts[n].shape)

    return (loss, grad_x, *[shaped(out_g, n) for n in _WEIGHTS], *[shaped(out_d, n) for n in _WEIGHTS],
            *[shaped(out_m, n) for n in _WEIGHTS], *[shaped(out_v, n) for n in _WEIGHTS])
```
